```python
import jax
import jax.numpy as jnp
from jax import lax
import numpy as np

D_MODEL = 4096
BATCH = 4
SEQ = 2048
DEPTH = 2
DEC_BATCH = 8
DEC_SEQ = 1
PAST_LEN = 16384
PAGE_SIZE = 128

PLE_DIM = 256
D_FF = 256 * ((8 * D_MODEL // 3 + 255) // 256)
EPS = 1e-6
GLA_HEADS = 4
GLA_DK = D_MODEL // (2 * GLA_HEADS)
GLA_DV = D_MODEL // GLA_HEADS
GLA_RANK = 16
GLA_TAU = 16.0
GLA_CHUNK = 64
ATT_GROUPS = ((128, 1), (512, 4), (2048, 16))
ATT_GROUP_HEADS = 8
ATT_HEAD_DIM = 128
ATT_HEADS = len(ATT_GROUPS) * ATT_GROUP_HEADS
GLA_QK = GLA_HEADS * GLA_DK
GLA_V = GLA_HEADS * GLA_DV
ATT_W = ATT_HEADS * ATT_HEAD_DIM
ATT_OUT = ATT_GROUP_HEADS * ATT_HEAD_DIM
IN_SPLITS = (GLA_QK, GLA_QK, GLA_V, GLA_RANK, GLA_V, ATT_W, ATT_W, ATT_W, D_MODEL, D_MODEL)
N_IN = sum(IN_SPLITS)

kernel_name = 'hybrid_gla_dilated_attn_step'


def _split_points():
    return [int(c) for c in np.cumsum(IN_SPLITS)[:-1]]


def _alibi_slopes():
    return jnp.exp2(-8.0 * jnp.arange(1, ATT_HEADS + 1, dtype=jnp.float32) / ATT_HEADS)


def _rmsnorm(x, g):
    xf = x.astype(jnp.float32)
    y = xf * lax.rsqrt(jnp.mean(xf * xf, axis=-1, keepdims=True) + EPS)
    return (y * g.astype(jnp.float32)).astype(x.dtype)


def _swiglu(h, w_in, w_out):
    g, u = jnp.split(h @ w_in, 2, axis=-1)
    return (jax.nn.silu(g) * u) @ w_out


def _gla(q, k, v, log_a, s0):
    B, T, H, DK = q.shape
    DV = v.shape[-1]
    C = min(GLA_CHUNK, T)
    n_c = -(-T // C)
    pad = n_c * C - T

    def prep(a):
        a = jnp.pad(a.astype(jnp.float32), ((0, 0), (0, pad), (0, 0), (0, 0)))
        return a.reshape(B, n_c, C, H, a.shape[-1]).swapaxes(0, 1)

    qc, kc, vc, ac = prep(q), prep(k), prep(v), prep(log_a)
    causal = jnp.tril(jnp.ones((C, C), bool))[None, :, :, None, None]

    def step(S, inp):
        qi, ki, vi, ai = inp
        b = jnp.cumsum(ai, axis=1)
        o_inter = jnp.einsum('bthk,bhkv->bthv', qi * jnp.exp(b), S)
        diff = b[:, :, None] - b[:, None, :]
        decay = jnp.where(causal, jnp.exp(jnp.where(causal, diff, 0.0)), 0.0)
        A = jnp.einsum('bthk,bshk,btshk->bhts', qi, ki, decay)
        o_intra = jnp.einsum('bhts,bshv->bthv', A, vi)
        b_last = b[:, -1]
        S_new = S * jnp.exp(b_last)[..., None] + jnp.einsum(
            'bshk,bshv->bhkv', ki * jnp.exp(b_last[:, None] - b), vi)
        return S_new, o_inter + o_intra

    S_T, o = lax.scan(step, s0.astype(jnp.float32), (qc, kc, vc, ac))
    o = o.swapaxes(0, 1).reshape(B, n_c * C, H, DV)[:, :T]
    return o, S_T


def _dilated_prompt(q, k, v, window, dil, slopes):
    B, T, G, HD = q.shape
    nb = window // dil
    span = nb * dil
    n_blk = -(-T // span)
    pad = n_blk * span - T

    def blk(a):
        a = jnp.pad(a, ((0, 0), (0, pad), (0, 0), (0, 0)))
        return a.reshape(B, n_blk, nb, dil, G, HD)

    def with_prev(a):
        prev = jnp.pad(a[:, :-1], ((0, 0), (1, 0), (0, 0), (0, 0), (0, 0), (0, 0)))
        return jnp.concatenate([prev, a], axis=2)

    qb = blk(q)
    kc, vc = with_prev(blk(k)), with_prev(blk(v))
    s = jnp.einsum('bnjrhc,bnkrhc->bnrhjk', qb, kc,
                   preferred_element_type=jnp.float32) * (HD ** -0.5)
    steps = nb + jnp.arange(nb)[:, None] - jnp.arange(2 * nb)[None, :]
    valid = (steps >= 0) & (steps <= nb)
    valid = valid[None] & ((jnp.arange(n_blk) > 0)[:, None, None]
                           | (jnp.arange(2 * nb) >= nb)[None, None, :])
    bias = -slopes[:, None, None] * (steps * dil).astype(jnp.float32)[None]
    s = jnp.where(valid[None, :, None, None], s + bias[None, None, None], -jnp.inf)
    lse = jax.nn.logsumexp(s, axis=-1)
    p = jnp.exp(s - lse[..., None])
    o = jnp.einsum('bnrhjk,bnkrhc->bnjrhc', p.astype(v.dtype), vc,
                   preferred_element_type=jnp.float32)
    o = o.reshape(B, n_blk * span, G, HD)[:, :T]
    lse = lse.transpose(0, 1, 4, 2, 3).reshape(B, n_blk * span, G)[:, :T]
    return o, lse


def _dilated_decode(q, k, v, buf, window, dil, slopes):
    B, T, G, HD = q.shape
    Wb = buf.shape[1]
    nb = window // dil
    new = jnp.stack([k, v], axis=2)
    kv_all = jnp.concatenate([buf.astype(new.dtype), new], axis=1)
    steps = jnp.arange(nb + 1)
    idx = (Wb + jnp.arange(T))[:, None] - dil * steps[None, :]
    valid = idx >= 0
    kv_g = kv_all[:, jnp.clip(idx, 0)]
    s = jnp.einsum('bthc,btkhc->bthk', q, kv_g[:, :, :, 0],
                   preferred_element_type=jnp.float32) * (HD ** -0.5)
    s = s - (slopes[:, None] * (steps * dil).astype(jnp.float32)[None, :])[None, None]
    s = jnp.where(valid[None, :, None, :], s, -jnp.inf)
    lse = jax.nn.logsumexp(s, axis=-1)
    p = jnp.exp(s - lse[..., None])
    o = jnp.einsum('bthk,btkhc->bthc', p.astype(v.dtype), kv_g[:, :, :, 1],
                   preferred_element_type=jnp.float32)
    return o, lse, kv_all[:, -Wb:]


def _mixer(h, s0, bufs, w_in, gla_w_decay, gla_b_decay, gla_norm, q_norm, k_norm,
           w_gla_out, w_att_out, w_out, slopes, decode):
    B, T, _ = h.shape
    (q_g, k_g, v_g, lr, r_g, q_a, k_a, v_a, z_g, z_a) = jnp.split(h @ w_in, _split_points(), axis=-1)
    log_a = jax.nn.log_sigmoid((lr @ gla_w_decay + gla_b_decay).astype(jnp.float32)) / GLA_TAU
    shp = (B, T, GLA_HEADS, -1)
    o_g, s_new = _gla(q_g.reshape(shp) * (GLA_DK ** -0.5), k_g.reshape(shp), v_g.reshape(shp),
                      log_a.reshape(shp), s0)
    o_g = _rmsnorm(o_g.astype(h.dtype), gla_norm).reshape(B, T, GLA_V) * jax.nn.silu(r_g)
    br_g = o_g @ w_gla_out
    ashp = (B, T, ATT_HEADS, ATT_HEAD_DIM)
    q_a = _rmsnorm(q_a.reshape(ashp), q_norm)
    k_a = _rmsnorm(k_a.reshape(ashp), k_norm)
    v_a = v_a.reshape(ashp)
    outs, lses, new_bufs = [], [], []
    for gi, (win, dil) in enumerate(ATT_GROUPS):
        sl = slice(gi * ATT_GROUP_HEADS, (gi + 1) * ATT_GROUP_HEADS)
        qg, kg, vg = q_a[:, :, sl], k_a[:, :, sl], v_a[:, :, sl]
        if decode:
            o, lse, nbuf = _dilated_decode(qg, kg, vg, bufs[gi], win, dil, slopes[sl])
        else:
            o, lse = _dilated_prompt(qg, kg, vg, win, dil, slopes[sl])
            nbuf = jnp.stack([kg, vg], axis=2)[:, T - min(win, T):]
        outs.append(o)
        lses.append(lse)
        new_bufs.append(nbuf)
    w_grp = jax.nn.softmax(jnp.stack(lses), axis=0)
    o_a = jnp.sum(w_grp[..., None] * jnp.stack(outs), axis=0).astype(h.dtype).reshape(B, T, ATT_OUT)
    br_a = o_a @ w_att_out
    merged = jax.nn.sigmoid(z_g) * br_g + jax.nn.sigmoid(z_a) * br_a
    return merged @ w_out, s_new, new_bufs


def _layer(x, pe, s0, bufs, lw, slopes, decode):
    (n_f1, f1_in, f1_out, n_mix, w_in, gwd, gbd, gnorm, qn, kn, wgo, wao, wo,
     n_f2, f2_in, f2_out, n_ple, w_pg, w_pp) = lw
    x = x + 0.5 * _swiglu(_rmsnorm(x, n_f1), f1_in, f1_out)
    m, s_new, bufs_new = _mixer(_rmsnorm(x, n_mix), s0, bufs, w_in, gwd, gbd, gnorm, qn, kn,
                                wgo, wao, wo, slopes, decode)
    x = x + m
    x = x + 0.5 * _swiglu(_rmsnorm(x, n_f2), f2_in, f2_out)
    x = x + jax.nn.sigmoid(_rmsnorm(x, n_ple) @ w_pg) * (pe.astype(x.dtype) @ w_pp)
    return x, s_new, bufs_new


def setup_inputs(seed: int = 0) -> dict:
    key = jax.random.key(seed)
    ks = iter(jax.random.split(key, 40))

    def nrm(shape, scale):
        return jax.random.normal(next(ks), shape, jnp.float32) * scale

    def gain(shape):
        return 1.0 + nrm(shape, 0.02)

    L, D, G, HD = DEPTH, D_MODEL, ATT_GROUP_HEADS, ATT_HEAD_DIM
    w0, w1, w2 = ATT_GROUPS[0][0], ATT_GROUPS[1][0], ATT_GROUPS[2][0]
    return {
        'x_prompt': nrm((BATCH, SEQ, D), 1.0),
        'x_sample': nrm((DEC_BATCH, DEC_SEQ, D), 1.0),
        'state_gla': nrm((L, DEC_BATCH, GLA_HEADS, GLA_DK, GLA_DV), 1.0),
        'cache_w128': nrm((L, DEC_BATCH, min(w0, PAST_LEN), 2, G, HD), 1.0),
        'cache_w512': nrm((L, DEC_BATCH, min(w1, PAST_LEN), 2, G, HD), 1.0),
        'cache_w2048': nrm((L, DEC_BATCH, min(w2, PAST_LEN), 2, G, HD), 1.0),
        'p_prompt': nrm((L, BATCH, SEQ, PLE_DIM), 1.0),
        'p_sample': nrm((L, DEC_BATCH, DEC_SEQ, PLE_DIM), 1.0),
        'norm_ffn1': gain((L, D)),
        'ffn1_w_in': nrm((L, D, 2 * D_FF), D ** -0.5),
        'ffn1_w_out': nrm((L, D_FF, D), D_FF ** -0.5),
        'norm_mix': gain((L, D)),
        'w_in': nrm((L, D, N_IN), D ** -0.5),
        'gla_w_decay': nrm((L, GLA_RANK, GLA_QK), GLA_RANK ** -0.5),
        'gla_b_decay': nrm((L, GLA_QK), 0.1),
        'gla_norm': gain((L, GLA_DV)),
        'att_q_norm': gain((L, HD)),
        'att_k_norm': gain((L, HD)),
        'w_gla_out': nrm((L, GLA_V, D), GLA_V ** -0.5),
        'w_att_out': nrm((L, ATT_OUT, D), ATT_OUT ** -0.5),
        'w_out': nrm((L, D, D), D ** -0.5),
        'norm_ffn2': gain((L, D)),
        'ffn2_w_in': nrm((L, D, 2 * D_FF), D ** -0.5),
        'ffn2_w_out': nrm((L, D_FF, D), D_FF ** -0.5),
        'norm_ple': gain((L, D)),
        'w_ple_gate': nrm((L, D, D), D ** -0.5),
        'w_ple_proj': nrm((L, PLE_DIM, D), PLE_DIM ** -0.5),
    }


def reference(x_prompt, x_sample, state_gla, cache_w128, cache_w512, cache_w2048, p_prompt, p_sample,
              norm_ffn1, ffn1_w_in, ffn1_w_out, norm_mix, w_in, gla_w_decay, gla_b_decay, gla_norm,
              att_q_norm, att_k_norm, w_gla_out, w_att_out, w_out, norm_ffn2, ffn2_w_in, ffn2_w_out,
              norm_ple, w_ple_gate, w_ple_proj):
    slopes = _alibi_slopes()
    yp, ys = x_prompt, x_sample
    s0_prompt = jnp.zeros((x_prompt.shape[0], GLA_HEADS, GLA_DK, GLA_DV), jnp.float32)
    gla_p, gla_s, win_p, win_s = [], [], [], []
    for i in range(DEPTH):
        lw = (norm_ffn1[i], ffn1_w_in[i], ffn1_w_out[i], norm_mix[i], w_in[i], gla_w_decay[i],
              gla_b_decay[i], gla_norm[i], att_q_norm[i], att_k_norm[i], w_gla_out[i], w_att_out[i],
              w_out[i], norm_ffn2[i], ffn2_w_in[i], ffn2_w_out[i], norm_ple[i], w_ple_gate[i],
              w_ple_proj[i])
        yp, sp, bp = _layer(yp, p_prompt[i], s0_prompt, None, lw, slopes, False)
        ys, ss, bs = _layer(ys, p_sample[i], state_gla[i],
                            (cache_w128[i], cache_w512[i], cache_w2048[i]), lw, slopes, True)
        gla_p.append(sp)
        gla_s.append(ss)
        win_p.append(bp)
        win_s.append(bs)
    state_gla_prompt = jnp.stack(gla_p)
    state_gla_sample = jnp.stack(gla_s)
    cache_w128_prompt = jnp.stack([b[0] for b in win_p])
    cache_w512_prompt = jnp.stack([b[1] for b in win_p])
    cache_w2048_prompt = jnp.stack([b[2] for b in win_p])
    cache_w128_sample = jnp.stack([b[0] for b in win_s])
    cache_w512_sample = jnp.stack([b[1] for b in win_s])
    cache_w2048_sample = jnp.stack([b[2] for b in win_s])
    return (yp, ys, state_gla_prompt, cache_w128_prompt, cache_w512_prompt, cache_w2048_prompt,
            state_gla_sample, cache_w128_sample, cache_w512_sample, cache_w2048_sample)
```

```python
import functools

import numpy as np
import jax
import jax.numpy as jnp
from jax import lax
from jax.experimental import pallas as pl
from jax.experimental.pallas import tpu as pltpu

F32 = jnp.float32
MXU_DTYPE = jnp.bfloat16

EPS = 1e-6
GLA_HEADS = 4
GLA_RANK = 16
GLA_TAU = 16.0
GLA_CHUNK = 64
GLA_SUB = 16
ATT_GROUPS = ((128, 1), (512, 4), (2048, 16))
ATT_GROUP_HEADS = 8
ATT_HEAD_DIM = 128
ATT_HEADS = len(ATT_GROUPS) * ATT_GROUP_HEADS
ATT_GW = ATT_GROUP_HEADS * ATT_HEAD_DIM
ATT_W = ATT_HEADS * ATT_HEAD_DIM
ATT_BLOCK = 128
DEC_ROWS = 128
LANE = 128
NEG = -1e30

VMEM_LIMIT = 56 * 1024 * 1024

_SLOPES = np.exp2(-8.0 * np.arange(1, ATT_HEADS + 1, dtype=np.float64) / ATT_HEADS)


def _params(*sem):
    return pltpu.CompilerParams(dimension_semantics=sem, vmem_limit_bytes=VMEM_LIMIT)


def _pick_tile(n, target, mult):
    best = None
    for t in range(mult, min(n, target) + 1, mult):
        if n % t == 0:
            best = t
    assert best is not None, (n, target, mult)
    return best


def _sigmoid(x):
    return 1.0 / (1.0 + jnp.exp(-x))


def _log_sigmoid(x):
    return jnp.minimum(x, 0.0) - jnp.log(1.0 + jnp.exp(-jnp.abs(x)))


def _dot(a, b):
    return jnp.dot(a, b, preferred_element_type=F32)


def _dot_nt(a, b):
    return lax.dot_general(a, b, (((1,), (1,)), ((), ())), preferred_element_type=F32)


def _dot_tn(a, b):
    return lax.dot_general(a, b, (((0,), (0,)), ((), ())), preferred_element_type=F32)


def _rmsnorm_kernel(x_ref, g_ref, o_ref):
    x = x_ref[...]
    ms = jnp.mean(x * x, axis=-1, keepdims=True)
    o_ref[...] = (x * lax.rsqrt(ms + EPS) * g_ref[...]).astype(o_ref.dtype)


def _rmsnorm(x, gain, layer):
    m, d = x.shape
    tr = _pick_tile(m, 512, 16)
    return pl.pallas_call(
        _rmsnorm_kernel,
        grid=(m // tr,),
        in_specs=[pl.BlockSpec((tr, d), lambda i: (i, 0)),
                  pl.BlockSpec((None, 1, d), lambda i: (layer, 0, 0))],
        out_specs=pl.BlockSpec((tr, d), lambda i: (i, 0)),
        out_shape=jax.ShapeDtypeStruct((m, d), MXU_DTYPE),
        compiler_params=_params("arbitrary"),
        name="rmsnorm",
    )(x, gain)


def _headnorm_kernel(x_ref, g_ref, o_ref):
    for h in range(ATT_GROUP_HEADS):
        sl = slice(h * ATT_HEAD_DIM, (h + 1) * ATT_HEAD_DIM)
        x = x_ref[:, sl]
        ms = jnp.mean(x * x, axis=-1, keepdims=True)
        o_ref[:, sl] = x * lax.rsqrt(ms + EPS) * g_ref[:, sl]


def _qk_headnorm(y, gains, layer, q_off):
    m = y.shape[0]
    tr = _pick_tile(m, 1024, 8)
    nblk = 2 * ATT_W // ATT_GW
    off = q_off // ATT_GW
    return pl.pallas_call(
        _headnorm_kernel,
        grid=(m // tr, nblk),
        in_specs=[pl.BlockSpec((tr, ATT_GW), lambda i, j: (i, off + j)),
                  pl.BlockSpec((None, 1, ATT_GW), lambda i, j: (layer, 0, j))],
        out_specs=pl.BlockSpec((tr, ATT_GW), lambda i, j: (i, j)),
        out_shape=jax.ShapeDtypeStruct((m, 2 * ATT_W), F32),
        compiler_params=_params("arbitrary", "arbitrary"),
        name="qk_headnorm",
    )(y, gains)


def _mm_tiles(m, n):
    return _pick_tile(m, 1100, 16), _pick_tile(n, 512, LANE)


def _w_spec(k, tn, layer, col_blk0):
    return pl.BlockSpec((None, k, tn), lambda i, j: (layer, 0, col_blk0 + j))


def _mm_kernel(h_ref, w_ref, o_ref):
    o_ref[...] = _dot(h_ref[...], w_ref[...].astype(MXU_DTYPE)).astype(o_ref.dtype)


def _matmul(h, w, layer, col0, n, out_dtype, tn=None, name="matmul"):
    m, k = h.shape
    tm, tn_ = _mm_tiles(m, n)
    tn = tn or tn_
    assert col0 % tn == 0 and n % tn == 0
    return pl.pallas_call(
        _mm_kernel,
        grid=(m // tm, n // tn),
        in_specs=[pl.BlockSpec((tm, k), lambda i, j: (i, 0)), _w_spec(k, tn, layer, col0 // tn)],
        out_specs=pl.BlockSpec((tm, tn), lambda i, j: (i, j)),
        out_shape=jax.ShapeDtypeStruct((m, n), out_dtype),
        compiler_params=_params("arbitrary", "arbitrary"),
        name=name,
    )(h, w)


def _mm_res_kernel(h_ref, w_ref, r_ref, o_ref):
    o_ref[...] = r_ref[...] + _dot(h_ref[...], w_ref[...].astype(MXU_DTYPE))


def _matmul_residual(h, w, layer, res):
    m, k = h.shape
    n = w.shape[-1]
    tm, tn = _mm_tiles(m, n)
    return pl.pallas_call(
        _mm_res_kernel,
        grid=(m // tm, n // tn),
        in_specs=[pl.BlockSpec((tm, k), lambda i, j: (i, 0)), _w_spec(k, tn, layer, 0),
                  pl.BlockSpec((tm, tn), lambda i, j: (i, j))],
        out_specs=pl.BlockSpec((tm, tn), lambda i, j: (i, j)),
        out_shape=jax.ShapeDtypeStruct((m, n), F32),
        compiler_params=_params("arbitrary", "arbitrary"),
        name="matmul_residual",
    )(h, w, res)


def _swiglu_kernel(h_ref, wg_ref, wu_ref, o_ref):
    h = h_ref[...]
    g = _dot(h, wg_ref[...].astype(MXU_DTYPE))
    u = _dot(h, wu_ref[...].astype(MXU_DTYPE))
    o_ref[...] = (g * _sigmoid(g) * u).astype(o_ref.dtype)


def _swiglu_in(h, w, layer):
    m, k = h.shape
    f = w.shape[-1] // 2
    tm = _pick_tile(m, 1100, 16)
    tn = _pick_tile(f, 256, LANE)
    nf = f // tn
    return pl.pallas_call(
        _swiglu_kernel,
        grid=(m // tm, nf),
        in_specs=[pl.BlockSpec((tm, k), lambda i, j: (i, 0)),
                  _w_spec(k, tn, layer, 0), _w_spec(k, tn, layer, nf)],
        out_specs=pl.BlockSpec((tm, tn), lambda i, j: (i, j)),
        out_shape=jax.ShapeDtypeStruct((m, f), MXU_DTYPE),
        compiler_params=_params("arbitrary", "arbitrary"),
        name="swiglu_in",
    )(h, w, w)


def _ffn_out_kernel(a_ref, w_ref, r_ref, o_ref, *, nk, k_valid_last, scale):
    kk = pl.program_id(2)
    tk = a_ref.shape[1]

    @pl.when(kk == 0)
    def _():
        o_ref[...] = jnp.zeros_like(o_ref)

    def accumulate(masked):
        a = a_ref[...]
        w = w_ref[...]
        if masked:
            a = jnp.where(lax.broadcasted_iota(jnp.int32, a.shape, 1) < k_valid_last, a, 0)
            w = jnp.where(lax.broadcasted_iota(jnp.int32, w.shape, 0) < k_valid_last, w, 0)
        o_ref[...] += _dot(a, w.astype(MXU_DTYPE))

    if k_valid_last == tk:
        accumulate(False)
    else:
        pl.when(kk < nk - 1)(functools.partial(accumulate, False))
        pl.when(kk == nk - 1)(functools.partial(accumulate, True))

    @pl.when(kk == nk - 1)
    def _():
        o_ref[...] = r_ref[...] + scale * o_ref[...]


def _ffn_out(a, w, layer, res, scale):
    m, k = a.shape
    n = w.shape[-1]
    tm = _pick_tile(m, 2200, 16)
    tn = _pick_tile(n, 512, LANE)
    tk = 1024
    nk = pl.cdiv(k, tk)
    kern = functools.partial(_ffn_out_kernel, nk=nk, k_valid_last=k - (nk - 1) * tk, scale=scale)
    return pl.pallas_call(
        kern,
        grid=(m // tm, n // tn, nk),
        in_specs=[pl.BlockSpec((tm, tk), lambda i, j, kk: (i, kk)),
                  pl.BlockSpec((None, tk, tn), lambda i, j, kk: (layer, kk, j)),
                  pl.BlockSpec((tm, tn), lambda i, j, kk: (i, j))],
        out_specs=pl.BlockSpec((tm, tn), lambda i, j, kk: (i, j)),
        out_shape=jax.ShapeDtypeStruct((m, n), F32),
        compiler_params=_params("arbitrary", "arbitrary", "arbitrary"),
        name="ffn_out",
    )(a, w, res)


def _merge_kernel(og_ref, oa_ref, wg_ref, wa_ref, zg_ref, za_ref, o_ref):
    bg = _dot(og_ref[...], wg_ref[...].astype(MXU_DTYPE))
    ba = _dot(oa_ref[...], wa_ref[...].astype(MXU_DTYPE))
    o_ref[...] = (_sigmoid(zg_ref[...]) * bg + _sigmoid(za_ref[...]) * ba).astype(o_ref.dtype)


def _merge(og, oa, w_gla_out, w_att_out, layer, y, zg_off, za_off):
    m, kg = og.shape
    ka = oa.shape[1]
    n = w_gla_out.shape[-1]
    tm, _ = _mm_tiles(m, n)
    tn = _pick_tile(n, 256, LANE)
    assert zg_off % tn == 0 and za_off % tn == 0
    return pl.pallas_call(
        _merge_kernel,
        grid=(m // tm, n // tn),
        in_specs=[pl.BlockSpec((tm, kg), lambda i, j: (i, 0)), pl.BlockSpec((tm, ka), lambda i, j: (i, 0)),
                  _w_spec(kg, tn, layer, 0), _w_spec(ka, tn, layer, 0),
                  pl.BlockSpec((tm, tn), lambda i, j: (i, zg_off // tn + j)),
                  pl.BlockSpec((tm, tn), lambda i, j: (i, za_off // tn + j))],
        out_specs=pl.BlockSpec((tm, tn), lambda i, j: (i, j)),
        out_shape=jax.ShapeDtypeStruct((m, n), MXU_DTYPE),
        compiler_params=_params("arbitrary", "arbitrary"),
        name="merge",
    )(og, oa, w_gla_out, w_att_out, y, y)


def _ple_kernel(h_ref, pe_ref, wg_ref, wp_ref, r_ref, o_ref):
    gate = _dot(h_ref[...], wg_ref[...].astype(MXU_DTYPE))
    proj = _dot(pe_ref[...].astype(MXU_DTYPE), wp_ref[...].astype(MXU_DTYPE))
    o_ref[...] = r_ref[...] + _sigmoid(gate) * proj


def _ple(h, pe, w_gate, w_proj, layer, res):
    m, k = h.shape
    kp = pe.shape[1]
    n = w_gate.shape[-1]
    tm, tn = _mm_tiles(m, n)
    return pl.pallas_call(
        _ple_kernel,
        grid=(m // tm, n // tn),
        in_specs=[pl.BlockSpec((tm, k), lambda i, j: (i, 0)), pl.BlockSpec((tm, kp), lambda i, j: (i, 0)),
                  _w_spec(k, tn, layer, 0), _w_spec(kp, tn, layer, 0),
                  pl.BlockSpec((tm, tn), lambda i, j: (i, j))],
        out_specs=pl.BlockSpec((tm, tn), lambda i, j: (i, j)),
        out_shape=jax.ShapeDtypeStruct((m, n), F32),
        compiler_params=_params("arbitrary", "arbitrary"),
        name="ple",
    )(h, pe, w_gate, w_proj, res)


def _split_cumsum(tri, x):
    hi = x.astype(MXU_DTYPE)
    r1 = x - hi.astype(F32)
    mid = r1.astype(MXU_DTYPE)
    lo = (r1 - mid.astype(F32)).astype(MXU_DTYPE)
    return _dot(tri, hi) + _dot(tri, mid) + _dot(tri, lo)


def _gla_out_gate(o, gn, r):
    ms = jnp.mean(o * o, axis=-1, keepdims=True)
    return o * lax.rsqrt(ms + EPS) * gn * (r * _sigmoid(r))


def _gla_kernel(q_ref, k_ref, v_ref, r_ref, lr_ref, wd_ref, bd_ref, gn_ref, o_ref, s_ref, st_ref, *, nc, q_scale):
    c = pl.program_id(2)

    @pl.when(c == 0)
    def _():
        st_ref[...] = jnp.zeros_like(st_ref)

    cs, dk = q_ref.shape
    q = q_ref[...] * q_scale
    k = k_ref[...]
    v = v_ref[...].astype(MXU_DTYPE)
    z = _dot(lr_ref[...].astype(MXU_DTYPE), wd_ref[...].astype(MXU_DTYPE)) + bd_ref[...]
    log_a = _log_sigmoid(z) / GLA_TAU
    ri = lax.broadcasted_iota(jnp.int32, (cs, cs), 0)
    ci = lax.broadcasted_iota(jnp.int32, (cs, cs), 1)
    b = _split_cumsum(jnp.where(ri >= ci, 1.0, 0.0).astype(MXU_DTYPE), log_a)

    st = st_ref[...]
    o = _dot_nt((q * jnp.exp(b)).astype(MXU_DTYPE), st.astype(MXU_DTYPE))

    sub = GLA_SUB
    tcol = lax.broadcasted_iota(jnp.int32, (sub, 1), 0)
    acol = lax.broadcasted_iota(jnp.int32, (sub, cs), 1)
    a_rows = []
    for blk in range(cs // sub):
        r0 = blk * sub
        b_i = b[r0:r0 + sub]
        q_i = q[r0:r0 + sub]
        k_i = k[r0:r0 + sub]
        if blk > 0:
            rho = b[r0 - 1:r0]
            qs = (q_i * jnp.exp(b_i - rho)).astype(MXU_DTYPE)
            ks = (k * jnp.exp(jnp.minimum(rho - b, 0.0))).astype(MXU_DTYPE)
            a_blk = jnp.where(acol < r0, _dot_nt(qs, ks), 0.0)
        else:
            a_blk = jnp.zeros((sub, cs), F32)
        for s in range(sub):
            causal = tcol >= s
            diff = jnp.where(causal, b_i - b_i[s:s + 1], 0.0)
            term = jnp.sum(q_i * k_i[s:s + 1] * jnp.exp(diff), axis=-1, keepdims=True)
            a_blk = a_blk + jnp.where((acol == r0 + s) & causal, term, 0.0)
        a_rows.append(a_blk)
    a = jnp.concatenate(a_rows, axis=0)
    o = o + _dot(a.astype(MXU_DTYPE), v)

    b_last = b[cs - 1:cs]
    kd = (k * jnp.exp(b_last - b)).astype(MXU_DTYPE)
    st_new = st * jnp.exp(b_last) + _dot_tn(v, kd)
    st_ref[...] = st_new

    o_ref[...] = _gla_out_gate(o, gn_ref[...], r_ref[...]).astype(o_ref.dtype)

    @pl.when(c == nc - 1)
    def _():
        s_ref[...] = st_new.T


def _gla_prompt(qkv, y, lr, wd, bd, gn, layer, bsz, seq, dk, dv, m_pad):
    nc = seq // GLA_CHUNK
    cs = GLA_CHUNK
    h_ = GLA_HEADS
    kern = functools.partial(_gla_kernel, nc=nc, q_scale=dk ** -0.5)
    row = lambda b, h, c: b * nc + c
    return pl.pallas_call(
        kern,
        grid=(bsz, h_, nc),
        in_specs=[pl.BlockSpec((cs, dk), lambda b, h, c: (row(b, h, c), h)),
                  pl.BlockSpec((cs, dk), lambda b, h, c: (row(b, h, c), h_ + h)),
                  pl.BlockSpec((cs, dv), lambda b, h, c: (row(b, h, c), 2 * h_ * dk // dv + h)),
                  pl.BlockSpec((cs, dv), lambda b, h, c: (row(b, h, c), h)),
                  pl.BlockSpec((cs, LANE), lambda b, h, c: (row(b, h, c), 0)),
                  pl.BlockSpec((None, LANE, dk), lambda b, h, c: (layer, 0, h)),
                  pl.BlockSpec((None, 1, dk), lambda b, h, c: (layer, 0, h)),
                  pl.BlockSpec((None, 1, dv), lambda b, h, c: (layer, 0, 0))],
        out_specs=[pl.BlockSpec((cs, dv), lambda b, h, c: (row(b, h, c), h)),
                   pl.BlockSpec((None, None, dk, dv), lambda b, h, c: (b, h, 0, 0))],
        out_shape=[jax.ShapeDtypeStruct((m_pad, h_ * dv), MXU_DTYPE),
                   jax.ShapeDtypeStruct((bsz, h_, dk, dv), F32)],
        scratch_shapes=[pltpu.VMEM((dv, dk), F32)],
        compiler_params=_params("arbitrary", "arbitrary", "arbitrary"),
        name="gla_prompt",
    )(qkv, qkv, qkv, y, lr, wd, bd, gn)


def _gla_decode_kernel(q_ref, k_ref, v_ref, r_ref, lr_ref, wd_ref, bd_ref, gn_ref, s0_ref, og_in_ref,
                       o_ref, s_ref, acc_ref, *, nb, q_scale):
    del og_in_ref
    bi = pl.program_id(1)

    @pl.when(bi == 0)
    def _():
        acc_ref[...] = jnp.zeros_like(acc_ref)

    rows = q_ref.shape[0]
    rid = lax.broadcasted_iota(jnp.int32, (rows, 1), 0)

    def pick(x):
        return jnp.sum(jnp.where(rid == bi, x, 0.0), axis=0, keepdims=True)

    def column(x_row, n):
        eye = lax.broadcasted_iota(jnp.int32, (n, n), 0) == lax.broadcasted_iota(jnp.int32, (n, n), 1)
        return jnp.sum(jnp.where(eye, x_row, 0.0), axis=1, keepdims=True)

    dk = q_ref.shape[1]
    q = pick(q_ref[...]) * q_scale
    k = pick(k_ref[...])
    v = pick(v_ref[...])
    z = _dot(lr_ref[...].astype(MXU_DTYPE), wd_ref[...].astype(MXU_DTYPE)) + bd_ref[...]
    decay = jnp.exp(pick(_log_sigmoid(z) / GLA_TAU))
    s0 = s0_ref[...]
    qd = jnp.broadcast_to(q * decay, (rows, dk)).astype(MXU_DTYPE)
    o = _dot(qd, s0.astype(MXU_DTYPE))[0:1]
    o = o + jnp.sum(q * k, axis=-1, keepdims=True) * v
    s_ref[...] = s0 * column(decay, dk) + column(k, dk) * v

    gated = _gla_out_gate(o, gn_ref[...], pick(r_ref[...]))
    acc = jnp.where(rid == bi, gated, acc_ref[...])
    acc_ref[...] = acc

    @pl.when(bi == nb - 1)
    def _():
        pad = jnp.zeros((o_ref.shape[0] - rows, o_ref.shape[1]), F32)
        o_ref[...] = jnp.concatenate([acc, pad], axis=0).astype(o_ref.dtype)


def _gla_decode(qkv, y, lr, wd, bd, gn, state, og, layer, nb, dk, dv, m_real):
    h_ = GLA_HEADS
    r8 = m_real // 8
    rdec = m_real // DEC_ROWS
    kern = functools.partial(_gla_decode_kernel, nb=nb, q_scale=dk ** -0.5)
    og_new, s_new = pl.pallas_call(
        kern,
        grid=(h_, nb),
        in_specs=[pl.BlockSpec((8, dk), lambda h, b: (r8, h)),
                  pl.BlockSpec((8, dk), lambda h, b: (r8, h_ + h)),
                  pl.BlockSpec((8, dv), lambda h, b: (r8, 2 * h_ * dk // dv + h)),
                  pl.BlockSpec((8, dv), lambda h, b: (r8, h)),
                  pl.BlockSpec((8, LANE), lambda h, b: (r8, 0)),
                  pl.BlockSpec((None, LANE, dk), lambda h, b: (layer, 0, h)),
                  pl.BlockSpec((None, 1, dk), lambda h, b: (layer, 0, h)),
                  pl.BlockSpec((None, 1, dv), lambda h, b: (layer, 0, 0)),
                  pl.BlockSpec((None, None, None, dk, dv), lambda h, b: (layer, b, h, 0, 0)),
                  pl.BlockSpec(memory_space=pl.ANY)],
        out_specs=[pl.BlockSpec((DEC_ROWS, dv), lambda h, b: (rdec, h)),
                   pl.BlockSpec((None, None, dk, dv), lambda h, b: (b, h, 0, 0))],
        out_shape=[jax.ShapeDtypeStruct(og.shape, og.dtype),
                   jax.ShapeDtypeStruct((nb, h_, dk, dv), F32)],
        scratch_shapes=[pltpu.VMEM((8, dv), F32)],
        input_output_aliases={9: 0},
        compiler_params=_params("arbitrary", "arbitrary"),
        name="gla_decode",
    )(qkv, qkv, qkv, y, lr, wd, bd, gn, state, og)
    return og_new, s_new


def _attn_kernel(q_ref, kc_ref, kp_ref, vc_ref, vp_ref, o_ref, l_ref, *, slopes, dil):
    qi = pl.program_id(2)
    tq = q_ref.shape[0]
    hd = ATT_HEAD_DIM
    row = lax.broadcasted_iota(jnp.int32, (tq, 2 * tq), 0)
    col = lax.broadcasted_iota(jnp.int32, (tq, 2 * tq), 1)
    steps = tq + row - col
    valid = (steps >= 0) & (steps <= ATT_BLOCK) & ((col >= tq) | (qi > 0))
    dist = (steps * dil).astype(F32)
    for j in range(ATT_GROUP_HEADS):
        sl = slice(j * hd, (j + 1) * hd)
        qh = q_ref[:, sl].astype(MXU_DTYPE)
        kk = jnp.concatenate([kp_ref[:, sl], kc_ref[:, sl]], axis=0).astype(MXU_DTYPE)
        vv = jnp.concatenate([vp_ref[:, sl], vc_ref[:, sl]], axis=0).astype(MXU_DTYPE)
        s = _dot_nt(qh, kk) * (hd ** -0.5)
        s = jnp.where(valid, s - slopes[j] * dist, NEG)
        m = jnp.max(s, axis=-1, keepdims=True)
        p = jnp.exp(s - m)
        l = jnp.sum(p, axis=-1, keepdims=True)
        o_ref[:, sl] = _dot(p.astype(MXU_DTYPE), vv) / l
        l_ref[:, sl] = jnp.broadcast_to(m + jnp.log(l), (tq, hd))


def _attn_prompt(qk, y, v_off, g, bsz, seq, m_pad):
    win, dil = ATT_GROUPS[g]
    assert win // dil == ATT_BLOCK and seq % (dil * ATT_BLOCK) == 0 and m_pad % dil == 0
    ln = seq // dil
    nq = ln // ATT_BLOCK
    wq = qk.shape[1] // ATT_GW
    wy = y.shape[1] // ATT_GW
    voff = v_off // ATT_GW
    qk_v = qk.reshape(m_pad // dil, dil * qk.shape[1])
    y_v = y.reshape(m_pad // dil, dil * y.shape[1])
    slopes = tuple(float(x) for x in _SLOPES[g * ATT_GROUP_HEADS:(g + 1) * ATT_GROUP_HEADS])
    kern = functools.partial(_attn_kernel, slopes=slopes, dil=dil)
    cur = lambda b, r, i: b * nq + i
    prev = lambda b, r, i: b * nq + jnp.maximum(i - 1, 0)
    blk = (ATT_BLOCK, ATT_GW)
    kcol = ATT_HEADS // ATT_GROUP_HEADS + g
    out = pl.pallas_call(
        kern,
        grid=(bsz, dil, nq),
        in_specs=[pl.BlockSpec(blk, lambda b, r, i: (cur(b, r, i), r * wq + g)),
                  pl.BlockSpec(blk, lambda b, r, i: (cur(b, r, i), r * wq + kcol)),
                  pl.BlockSpec(blk, lambda b, r, i: (prev(b, r, i), r * wq + kcol)),
                  pl.BlockSpec(blk, lambda b, r, i: (cur(b, r, i), r * wy + voff + g)),
                  pl.BlockSpec(blk, lambda b, r, i: (prev(b, r, i), r * wy + voff + g))],
        out_specs=[pl.BlockSpec(blk, lambda b, r, i: (cur(b, r, i), r)),
                   pl.BlockSpec(blk, lambda b, r, i: (cur(b, r, i), r))],
        out_shape=[jax.ShapeDtypeStruct((m_pad // dil, dil * ATT_GW), F32)] * 2,
        compiler_params=_params("arbitrary", "arbitrary", "arbitrary"),
        name=f"attn_prompt_g{g}",
    )(qk_v, qk_v, qk_v, y_v, y_v)
    return [x.reshape(m_pad, ATT_GW) for x in out]


def _attn_decode_kernel(qk_ref, v0_ref, v1_ref, v2_ref, c0_ref, c1_ref, c2_ref, *rest, nb):
    n_g = len(ATT_GROUPS)
    outs = rest[n_g * 2:n_g * 4]
    acc_ref = rest[n_g * 4]
    v_refs = (v0_ref, v1_ref, v2_ref)
    c_refs = (c0_ref, c1_ref, c2_ref)
    bi = pl.program_id(0)
    hd = ATT_HEAD_DIM

    @pl.when(bi == 0)
    def _():
        acc_ref[...] = jnp.zeros_like(acc_ref)

    rows = qk_ref.shape[0]
    rid = lax.broadcasted_iota(jnp.int32, (rows, 1), 0)

    def pick(x):
        return jnp.sum(jnp.where(rid == bi, x, 0.0), axis=0, keepdims=True)

    nrow = lax.broadcasted_iota(jnp.int32, (ATT_BLOCK, 1), 0)
    for g, (_, dil) in enumerate(ATT_GROUPS):
        dist = ((ATT_BLOCK - nrow) * dil).astype(F32)
        for j in range(ATT_GROUP_HEADS):
            lo = g * ATT_GW + j * hd
            qh = pick(qk_ref[:, lo:lo + hd])
            kn = pick(qk_ref[:, ATT_W + lo:ATT_W + lo + hd])
            vn = pick(v_refs[g][:, j * hd:(j + 1) * hd])
            kb = c_refs[g][:, j * hd:(j + 1) * hd]
            vb = c_refs[g][:, ATT_GW + j * hd:ATT_GW + (j + 1) * hd]
            sb = jnp.sum(kb * qh, axis=-1, keepdims=True) * (hd ** -0.5) - float(_SLOPES[g * ATT_GROUP_HEADS + j]) * dist
            sn = jnp.sum(kn * qh, axis=-1, keepdims=True) * (hd ** -0.5)
            m = jnp.maximum(jnp.max(sb, axis=0, keepdims=True), sn)
            pb = jnp.exp(sb - m)
            pn = jnp.exp(sn - m)
            l = jnp.sum(pb, axis=0, keepdims=True) + pn
            o = (jnp.sum(pb * vb, axis=0, keepdims=True) + pn * vn) / l
            lse = jnp.broadcast_to(m + jnp.log(l), (1, hd))
            sl = slice(j * hd, (j + 1) * hd)
            acc_ref[2 * g, :, sl] = jnp.where(rid == bi, o, acc_ref[2 * g, :, sl])
            acc_ref[2 * g + 1, :, sl] = jnp.where(rid == bi, lse, acc_ref[2 * g + 1, :, sl])

    @pl.when(bi == nb - 1)
    def _():
        for idx, ref in enumerate(outs):
            pad = jnp.zeros((ref.shape[0] - rows, ref.shape[1]), F32)
            ref[...] = jnp.concatenate([acc_ref[idx], pad], axis=0)


def _attn_decode(qk, y, v_off, caches, ol, layer, nb, m_real):
    r8 = m_real // 8
    rdec = m_real // DEC_ROWS
    voff = v_off // ATT_GW
    views = []
    for (win, dil), cbuf in zip(ATT_GROUPS, caches):
        assert cbuf.shape[2] == win
        views.append(cbuf.reshape(cbuf.shape[0], cbuf.shape[1], win // dil, dil * 2 * ATT_GW))
    any_spec = pl.BlockSpec(memory_space=pl.ANY)
    return pl.pallas_call(
        functools.partial(_attn_decode_kernel, nb=nb),
        grid=(nb,),
        in_specs=[pl.BlockSpec((8, 2 * ATT_W), lambda b: (r8, 0))]
        + [pl.BlockSpec((8, ATT_GW), lambda b, g=g: (r8, voff + g)) for g in range(len(ATT_GROUPS))]
        + [pl.BlockSpec((None, None, ATT_BLOCK, 2 * ATT_GW), lambda b: (layer, b, 0, 0)) for _ in ATT_GROUPS]
        + [any_spec] * len(ol),
        out_specs=[pl.BlockSpec((DEC_ROWS, ATT_GW), lambda b: (rdec, 0))] * len(ol),
        out_shape=[jax.ShapeDtypeStruct(x.shape, x.dtype) for x in ol],
        scratch_shapes=[pltpu.VMEM((len(ol), 8, ATT_GW), F32)],
        input_output_aliases={7 + n: n for n in range(len(ol))},
        compiler_params=_params("arbitrary"),
        name="attn_decode",
    )(qk, y, y, y, *views, *ol)


def _combine_kernel(o0_ref, l0_ref, o1_ref, l1_ref, o2_ref, l2_ref, out_ref):
    l0, l1, l2 = l0_ref[...], l1_ref[...], l2_ref[...]
    m = jnp.maximum(jnp.maximum(l0, l1), l2)
    w0, w1, w2 = jnp.exp(l0 - m), jnp.exp(l1 - m), jnp.exp(l2 - m)
    den = w0 + w1 + w2
    out_ref[...] = ((w0 / den) * o0_ref[...] + (w1 / den) * o1_ref[...] + (w2 / den) * o2_ref[...]).astype(out_ref.dtype)


def _combine(ol):
    m = ol[0].shape[0]
    tr = _pick_tile(m, 256, 16)
    spec = pl.BlockSpec((tr, ATT_GW), lambda i: (i, 0))
    return pl.pallas_call(
        _combine_kernel,
        grid=(m // tr,),
        in_specs=[spec] * 6,
        out_specs=spec,
        out_shape=jax.ShapeDtypeStruct((m, ATT_GW), MXU_DTYPE),
        compiler_params=_params("arbitrary"),
        name="attn_combine",
    )(*ol)


def kernel(x_prompt, x_sample, state_gla, cache_w128, cache_w512, cache_w2048, p_prompt, p_sample, norm_ffn1, ffn1_w_in, ffn1_w_out, norm_mix, w_in, gla_w_decay, gla_b_decay, gla_norm, att_q_norm, att_k_norm, w_gla_out, w_att_out, w_out, norm_ffn2, ffn2_w_in, ffn2_w_out, norm_ple, w_ple_gate, w_ple_proj):
    bsz, seq, d = x_prompt.shape
    nb, dec_seq, _ = x_sample.shape
    depth = norm_ffn1.shape[0]
    assert dec_seq == 1 and nb <= 8
    m_real = bsz * seq
    assert m_real % DEC_ROWS == 0
    m_pad = m_real + DEC_ROWS
    gla_qk = gla_w_decay.shape[-1]
    dk = gla_qk // GLA_HEADS
    gla_v = d
    dv = gla_v // GLA_HEADS
    caches = (cache_w128, cache_w512, cache_w2048)

    lr_off = 2 * gla_qk + gla_v
    rest_off = lr_off + GLA_RANK
    r_off, q_off = 0, gla_v
    k_off = q_off + ATT_W
    v_off = k_off + ATT_W
    zg_off = v_off + ATT_W
    za_off = zg_off + d
    n_rest = za_off + d
    assert w_in.shape[-1] == rest_off + n_rest and lr_off % LANE == 0

    zrow = lambda n, w: jnp.zeros((n, w), F32)
    x = jnp.concatenate([x_prompt.reshape(m_real, d), x_sample.reshape(nb, d), zrow(DEC_ROWS - nb, d)], axis=0)
    pdim = p_prompt.shape[-1]
    pe = jnp.concatenate([p_prompt.reshape(depth, m_real, pdim), p_sample.reshape(depth, nb, pdim),
                          jnp.zeros((depth, DEC_ROWS - nb, pdim), F32)], axis=1)

    w_rest = w_in[:, :, rest_off:].astype(MXU_DTYPE)
    wd = jnp.zeros((depth, LANE, gla_qk), F32).at[:, :GLA_RANK].set(gla_w_decay)
    bd = gla_b_decay.reshape(depth, 1, gla_qk)
    gn = gla_norm.reshape(depth, 1, dv)
    qk_gain = jnp.concatenate([jnp.tile(att_q_norm, (1, ATT_HEADS)), jnp.tile(att_k_norm, (1, ATT_HEADS))],
                              axis=1).reshape(depth, 1, 2 * ATT_W)
    g3 = lambda a: a.reshape(depth, 1, d)
    n_f1, n_mix, n_f2, n_ple = g3(norm_ffn1), g3(norm_mix), g3(norm_ffn2), g3(norm_ple)

    gla_p, gla_s, win_p, win_s = [], [], [], []
    for i in range(depth):
        x = _ffn_out(_swiglu_in(_rmsnorm(x, n_f1, i), ffn1_w_in, i), ffn1_w_out, i, x, 0.5)

        h = _rmsnorm(x, n_mix, i)
        qkv = _matmul(h, w_in, i, 0, lr_off, F32, name="in_proj_gla")
        lr = _matmul(h, w_in, i, lr_off, LANE, F32, tn=LANE, name="in_proj_decay")
        y = _matmul(h, w_rest, i, 0, n_rest, F32, name="in_proj_rest")
        qk = _qk_headnorm(y, qk_gain, i, q_off)

        og, s_p = _gla_prompt(qkv, y, lr, wd, bd, gn, i, bsz, seq, dk, dv, m_pad)
        og, s_s = _gla_decode(qkv, y, lr, wd, bd, gn, state_gla, og, i, nb, dk, dv, m_real)

        ol = []
        for g in range(len(ATT_GROUPS)):
            ol += _attn_prompt(qk, y, v_off, g, bsz, seq, m_pad)
        ol = _attn_decode(qk, y, v_off, caches, ol, i, nb, m_real)
        oa = _combine(ol)

        merged = _merge(og, oa, w_gla_out, w_att_out, i, y, zg_off, za_off)
        x = _matmul_residual(merged, w_out, i, x)
        x = _ffn_out(_swiglu_in(_rmsnorm(x, n_f2, i), ffn2_w_in, i), ffn2_w_out, i, x, 0.5)
        x = _ple(_rmsnorm(x, n_ple, i), pe[i], w_ple_gate, w_ple_proj, i, x)

        khat = qk[:, ATT_W:]
        vraw = y[:, v_off:v_off + ATT_W]
        bp, bs = [], []
        for g, (win, _) in enumerate(ATT_GROUPS):
            cols = slice(g * ATT_GW, (g + 1) * ATT_GW)
            kv = jnp.stack([khat[:, cols], vraw[:, cols]], axis=1)
            kv = kv.reshape(m_pad, 2, ATT_GROUP_HEADS, ATT_HEAD_DIM)
            kv_p = kv[:m_real].reshape(bsz, seq, 2, ATT_GROUP_HEADS, ATT_HEAD_DIM)
            bp.append(kv_p[:, seq - min(win, seq):])
            new = kv[m_real:m_real + nb][:, None]
            bs.append(jnp.concatenate([caches[g][i][:, 1:], new], axis=1))
        gla_p.append(s_p)
        gla_s.append(s_s)
        win_p.append(bp)
        win_s.append(bs)

    y_prompt = x[:m_real].reshape(bsz, seq, d)
    y_sample = x[m_real:m_real + nb].reshape(nb, 1, d)
    stack = lambda items: jnp.stack(items)
    return (y_prompt, y_sample, stack(gla_p),
            stack([b[0] for b in win_p]), stack([b[1] for b in win_p]), stack([b[2] for b in win_p]),
            stack(gla_s),
            stack([b[0] for b in win_s]), stack([b[1] for b in win_s]), stack([b[2] for b in win_s]))
```

```python
import functools

import jax
import jax.numpy as jnp
from jax import lax
from jax.experimental import pallas as pl
from jax.experimental.pallas import tpu as pltpu

F32 = jnp.float32
MXU_DTYPE = jnp.bfloat16

EPS = 1e-6
GLA_HEADS = 4
GLA_RANK = 16
GLA_TAU = 16.0
GLA_CHUNK = 64
GLA_SUB = 16
ATT_GROUPS = ((128, 1), (512, 4), (2048, 16))
ATT_GROUP_HEADS = 8
ATT_HEAD_DIM = 128
ATT_HEADS = len(ATT_GROUPS) * ATT_GROUP_HEADS
ATT_GW = ATT_GROUP_HEADS * ATT_HEAD_DIM
ATT_W = ATT_HEADS * ATT_HEAD_DIM
ATT_BLOCK = 128
DEC_ROWS = 128
LANE = 128
NEG = -1e30

VMEM_LIMIT = 56 * 1024 * 1024


def _params(*sem):
    return pltpu.CompilerParams(dimension_semantics=sem, vmem_limit_bytes=VMEM_LIMIT)


def _pick_tile(n, target, mult):
    best = None
    for t in range(mult, min(n, target) + 1, mult):
        if n % t == 0:
            best = t
    assert best is not None, (n, target, mult)
    return best


def _sigmoid(x):
    return 1.0 / (1.0 + jnp.exp(-x))


def _log_sigmoid(x):
    return jnp.minimum(x, 0.0) - jnp.log(1.0 + jnp.exp(-jnp.abs(x)))


def _dot(a, b):
    return jnp.dot(a, b, preferred_element_type=F32)


def _dot_nt(a, b):
    return lax.dot_general(a, b, (((1,), (1,)), ((), ())), preferred_element_type=F32)


def _dot_tn(a, b):
    return lax.dot_general(a, b, (((0,), (0,)), ((), ())), preferred_element_type=F32)


def _rmsnorm_kernel(x_ref, g_ref, o_ref):
    x = x_ref[...]
    ms = jnp.mean(x * x, axis=-1, keepdims=True)
    o_ref[...] = (x * lax.rsqrt(ms + EPS) * g_ref[...]).astype(o_ref.dtype)


def _rmsnorm(x, gain, layer):
    m, d = x.shape
    tr = _pick_tile(m, 512, 16)
    return pl.pallas_call(
        _rmsnorm_kernel,
        grid=(m // tr,),
        in_specs=[pl.BlockSpec((tr, d), lambda i: (i, 0)),
                  pl.BlockSpec((None, 1, d), lambda i: (layer, 0, 0))],
        out_specs=pl.BlockSpec((tr, d), lambda i: (i, 0)),
        out_shape=jax.ShapeDtypeStruct((m, d), MXU_DTYPE),
        compiler_params=_params("arbitrary"),
        name="rmsnorm",
    )(x, gain)


def _mm_tiles(m, n):
    return _pick_tile(m, 1100, 16), _pick_tile(n, 512, LANE)


def _w_spec(k, tn, layer, col_blk0):
    return pl.BlockSpec((None, k, tn), lambda i, j: (layer, 0, col_blk0 + j))


def _mm_kernel(h_ref, w_ref, o_ref, *, w_rows):
    w = (w_ref[0] if w_rows else w_ref[...]).astype(MXU_DTYPE)
    acc = _dot_nt(h_ref[...], w) if w_rows else _dot(h_ref[...], w)
    o_ref[...] = acc.astype(o_ref.dtype)


def _w_rows_spec(k, tn, layer, row0, col_map=None):
    col_map = col_map or (lambda j: j)
    return pl.BlockSpec((pl.Element(1), pl.Element(tn), pl.Element(k)),
                        lambda i, j: (layer, pl.multiple_of(row0 + col_map(j) * tn, 8), 0))


def _matmul(h, w, layer, col0, n, out_dtype, tn=None, col_map=None, w_rows=False, name="matmul"):
    m, k = h.shape
    tm, tn_ = _mm_tiles(m, n)
    tn = tn or tn_
    assert n % tn == 0
    if w_rows:
        w_spec = _w_rows_spec(k, tn, layer, col0, col_map)
    else:
        assert col0 % tn == 0
        col_map = col_map or (lambda j: j)
        w_spec = pl.BlockSpec((None, k, tn), lambda i, j: (layer, 0, col0 // tn + col_map(j)))
    return pl.pallas_call(
        functools.partial(_mm_kernel, w_rows=w_rows),
        grid=(m // tm, n // tn),
        in_specs=[pl.BlockSpec((tm, k), lambda i, j: (i, 0)), w_spec],
        out_specs=pl.BlockSpec((tm, tn), lambda i, j: (i, j)),
        out_shape=jax.ShapeDtypeStruct((m, n), out_dtype),
        compiler_params=_params("arbitrary", "arbitrary"),
        name=name,
    )(h, w)


def _mm_res_kernel(h_ref, w_ref, r_ref, o_ref):
    o_ref[...] = r_ref[...] + _dot(h_ref[...], w_ref[...].astype(MXU_DTYPE))


def _matmul_residual(h, w, layer, res):
    m, k = h.shape
    n = w.shape[-1]
    tm, tn = _mm_tiles(m, n)
    return pl.pallas_call(
        _mm_res_kernel,
        grid=(m // tm, n // tn),
        in_specs=[pl.BlockSpec((tm, k), lambda i, j: (i, 0)), _w_spec(k, tn, layer, 0),
                  pl.BlockSpec((tm, tn), lambda i, j: (i, j))],
        out_specs=pl.BlockSpec((tm, tn), lambda i, j: (i, j)),
        out_shape=jax.ShapeDtypeStruct((m, n), F32),
        compiler_params=_params("arbitrary", "arbitrary"),
        name="matmul_residual",
    )(h, w, res)


def _swiglu_kernel(h_ref, wg_ref, wu_ref, o_ref):
    h = h_ref[...]
    g = _dot(h, wg_ref[...].astype(MXU_DTYPE))
    u = _dot(h, wu_ref[...].astype(MXU_DTYPE))
    o_ref[...] = (g * _sigmoid(g) * u).astype(o_ref.dtype)


def _swiglu_in(h, w, layer):
    m, k = h.shape
    f = w.shape[-1] // 2
    tm = _pick_tile(m, 1100, 16)
    tn = _pick_tile(f, 256, LANE)
    nf = f // tn
    return pl.pallas_call(
        _swiglu_kernel,
        grid=(m // tm, nf),
        in_specs=[pl.BlockSpec((tm, k), lambda i, j: (i, 0)),
                  _w_spec(k, tn, layer, 0), _w_spec(k, tn, layer, nf)],
        out_specs=pl.BlockSpec((tm, tn), lambda i, j: (i, j)),
        out_shape=jax.ShapeDtypeStruct((m, f), MXU_DTYPE),
        compiler_params=_params("arbitrary", "arbitrary"),
        name="swiglu_in",
    )(h, w, w)


def _ffn_out_kernel(a_ref, w_ref, r_ref, o_ref, *, nk, k_valid_last, scale):
    kk = pl.program_id(2)
    tk = a_ref.shape[1]

    @pl.when(kk == 0)
    def _():
        o_ref[...] = jnp.zeros_like(o_ref)

    def accumulate(masked):
        a = a_ref[...]
        w = w_ref[...]
        if masked:
            a = jnp.where(lax.broadcasted_iota(jnp.int32, a.shape, 1) < k_valid_last, a, 0)
            w = jnp.where(lax.broadcasted_iota(jnp.int32, w.shape, 0) < k_valid_last, w, 0)
        o_ref[...] += _dot(a, w.astype(MXU_DTYPE))

    if k_valid_last == tk:
        accumulate(False)
    else:
        pl.when(kk < nk - 1)(functools.partial(accumulate, False))
        pl.when(kk == nk - 1)(functools.partial(accumulate, True))

    @pl.when(kk == nk - 1)
    def _():
        o_ref[...] = r_ref[...] + scale * o_ref[...]


def _ffn_out(a, w, layer, res, scale):
    m, k = a.shape
    n = w.shape[-1]
    tm = _pick_tile(m, 2200, 16)
    tn = _pick_tile(n, 512, LANE)
    tk = 1024
    nk = pl.cdiv(k, tk)
    kern = functools.partial(_ffn_out_kernel, nk=nk, k_valid_last=k - (nk - 1) * tk, scale=scale)
    return pl.pallas_call(
        kern,
        grid=(m // tm, n // tn, nk),
        in_specs=[pl.BlockSpec((tm, tk), lambda i, j, kk: (i, kk)),
                  pl.BlockSpec((None, tk, tn), lambda i, j, kk: (layer, kk, j)),
                  pl.BlockSpec((tm, tn), lambda i, j, kk: (i, j))],
        out_specs=pl.BlockSpec((tm, tn), lambda i, j, kk: (i, j)),
        out_shape=jax.ShapeDtypeStruct((m, n), F32),
        compiler_params=_params("arbitrary", "arbitrary", "arbitrary"),
        name="ffn_out",
    )(a, w, res)


def _merge_kernel(og_ref, oa_ref, wg_ref, wa_ref, zg_ref, za_ref, o_ref):
    bg = _dot(og_ref[...], wg_ref[...].astype(MXU_DTYPE))
    ba = _dot(oa_ref[...], wa_ref[...].astype(MXU_DTYPE))
    o_ref[...] = (_sigmoid(zg_ref[...]) * bg + _sigmoid(za_ref[...]) * ba).astype(o_ref.dtype)


def _merge(og, oa, w_gla_out, w_att_out, layer, y, zg_off, za_off):
    m, kg = og.shape
    ka = oa.shape[1]
    n = w_gla_out.shape[-1]
    tm, _ = _mm_tiles(m, n)
    tn = _pick_tile(n, 256, LANE)
    assert zg_off % tn == 0 and za_off % tn == 0
    return pl.pallas_call(
        _merge_kernel,
        grid=(m // tm, n // tn),
        in_specs=[pl.BlockSpec((tm, kg), lambda i, j: (i, 0)), pl.BlockSpec((tm, ka), lambda i, j: (i, 0)),
                  _w_spec(kg, tn, layer, 0), _w_spec(ka, tn, layer, 0),
                  pl.BlockSpec((tm, tn), lambda i, j: (i, zg_off // tn + j)),
                  pl.BlockSpec((tm, tn), lambda i, j: (i, za_off // tn + j))],
        out_specs=pl.BlockSpec((tm, tn), lambda i, j: (i, j)),
        out_shape=jax.ShapeDtypeStruct((m, n), MXU_DTYPE),
        compiler_params=_params("arbitrary", "arbitrary"),
        name="merge",
    )(og, oa, w_gla_out, w_att_out, y, y)


def _ple_kernel(h_ref, pe_ref, wg_ref, wp_ref, r_ref, o_ref):
    gate = _dot(h_ref[...], wg_ref[...].astype(MXU_DTYPE))
    proj = _dot(pe_ref[...].astype(MXU_DTYPE), wp_ref[...].astype(MXU_DTYPE))
    o_ref[...] = r_ref[...] + _sigmoid(gate) * proj


def _ple(h, pe, w_gate, w_proj, layer, res):
    m, k = h.shape
    kp = pe.shape[1]
    n = w_gate.shape[-1]
    tm, tn = _mm_tiles(m, n)
    return pl.pallas_call(
        _ple_kernel,
        grid=(m // tm, n // tn),
        in_specs=[pl.BlockSpec((tm, k), lambda i, j: (i, 0)), pl.BlockSpec((tm, kp), lambda i, j: (i, 0)),
                  _w_spec(k, tn, layer, 0), _w_spec(kp, tn, layer, 0),
                  pl.BlockSpec((tm, tn), lambda i, j: (i, j))],
        out_specs=pl.BlockSpec((tm, tn), lambda i, j: (i, j)),
        out_shape=jax.ShapeDtypeStruct((m, n), F32),
        compiler_params=_params("arbitrary", "arbitrary"),
        name="ple",
    )(h, pe, w_gate, w_proj, res)


def _split_cumsum(tri, x):
    hi = x.astype(MXU_DTYPE)
    r1 = x - hi.astype(F32)
    mid = r1.astype(MXU_DTYPE)
    lo = (r1 - mid.astype(F32)).astype(MXU_DTYPE)
    return _dot(tri, hi) + _dot(tri, mid) + _dot(tri, lo)


def _gla_out_gate(o, gn, r):
    ms = jnp.mean(o * o, axis=-1, keepdims=True)
    return o * lax.rsqrt(ms + EPS) * gn * (r * _sigmoid(r))


def _gla_kernel(q_ref, k_ref, v_ref, r_ref, lr_ref, wd_ref, bd_ref, gn_ref, o_ref, s_ref, st_ref, *, nc, q_scale):
    c = pl.program_id(2)

    @pl.when(c == 0)
    def _():
        st_ref[...] = jnp.zeros_like(st_ref)

    cs, dk = q_ref.shape
    q = q_ref[...] * q_scale
    k = k_ref[...]
    v = v_ref[...].astype(MXU_DTYPE)
    z = _dot(lr_ref[...].astype(MXU_DTYPE), wd_ref[...].astype(MXU_DTYPE)) + bd_ref[...]
    log_a = _log_sigmoid(z) / GLA_TAU
    ri = lax.broadcasted_iota(jnp.int32, (cs, cs), 0)
    ci = lax.broadcasted_iota(jnp.int32, (cs, cs), 1)
    b = _split_cumsum(jnp.where(ri >= ci, 1.0, 0.0).astype(MXU_DTYPE), log_a)

    st = st_ref[...]
    o = _dot_nt((q * jnp.exp(b)).astype(MXU_DTYPE), st.astype(MXU_DTYPE))

    sub = GLA_SUB
    tcol = lax.broadcasted_iota(jnp.int32, (sub, 1), 0)
    acol = lax.broadcasted_iota(jnp.int32, (sub, cs), 1)
    a_rows = []
    for blk in range(cs // sub):
        r0 = blk * sub
        b_i = b[r0:r0 + sub]
        q_i = q[r0:r0 + sub]
        k_i = k[r0:r0 + sub]
        if blk > 0:
            rho = b[r0 - 1:r0]
            qs = (q_i * jnp.exp(b_i - rho)).astype(MXU_DTYPE)
            ks = (k * jnp.exp(jnp.minimum(rho - b, 0.0))).astype(MXU_DTYPE)
            a_blk = jnp.where(acol < r0, _dot_nt(qs, ks), 0.0)
        else:
            a_blk = jnp.zeros((sub, cs), F32)
        for s in range(sub):
            causal = tcol >= s
            diff = jnp.where(causal, b_i - b_i[s:s + 1], 0.0)
            term = jnp.sum(q_i * k_i[s:s + 1] * jnp.exp(diff), axis=-1, keepdims=True)
            a_blk = a_blk + jnp.where((acol == r0 + s) & causal, term, 0.0)
        a_rows.append(a_blk)
    a = jnp.concatenate(a_rows, axis=0)
    o = o + _dot(a.astype(MXU_DTYPE), v)

    b_last = b[cs - 1:cs]
    kd = (k * jnp.exp(b_last - b)).astype(MXU_DTYPE)
    st_new = st * jnp.exp(b_last) + _dot_tn(v, kd)
    st_ref[...] = st_new

    o_ref[...] = _gla_out_gate(o, gn_ref[...], r_ref[...]).astype(o_ref.dtype)

    @pl.when(c == nc - 1)
    def _():
        s_ref[...] = st_new.T


def _gla_prompt(qkv, y, lr, wd, bd, gn, layer, bsz, seq, dk, dv, m_pad):
    nc = seq // GLA_CHUNK
    cs = GLA_CHUNK
    h_ = GLA_HEADS
    kern = functools.partial(_gla_kernel, nc=nc, q_scale=dk ** -0.5)
    row = lambda b, h, c: b * nc + c
    return pl.pallas_call(
        kern,
        grid=(bsz, h_, nc),
        in_specs=[pl.BlockSpec((cs, dk), lambda b, h, c: (row(b, h, c), h)),
                  pl.BlockSpec((cs, dk), lambda b, h, c: (row(b, h, c), h_ + h)),
                  pl.BlockSpec((cs, dv), lambda b, h, c: (row(b, h, c), 2 * h_ * dk // dv + h)),
                  pl.BlockSpec((cs, dv), lambda b, h, c: (row(b, h, c), h)),
                  pl.BlockSpec((cs, LANE), lambda b, h, c: (row(b, h, c), 0)),
                  pl.BlockSpec((None, LANE, dk), lambda b, h, c: (layer, 0, h)),
                  pl.BlockSpec((None, 1, dk), lambda b, h, c: (layer, 0, h)),
                  pl.BlockSpec((None, 1, dv), lambda b, h, c: (layer, 0, 0))],
        out_specs=[pl.BlockSpec((cs, dv), lambda b, h, c: (row(b, h, c), h)),
                   pl.BlockSpec((None, None, dk, dv), lambda b, h, c: (b, h, 0, 0))],
        out_shape=[jax.ShapeDtypeStruct((m_pad, h_ * dv), MXU_DTYPE),
                   jax.ShapeDtypeStruct((bsz, h_, dk, dv), F32)],
        scratch_shapes=[pltpu.VMEM((dv, dk), F32)],
        compiler_params=_params("arbitrary", "arbitrary", "arbitrary"),
        name="gla_prompt",
    )(qkv, qkv, qkv, y, lr, wd, bd, gn)


def _gla_decode_kernel(q_ref, k_ref, v_ref, r_ref, lr_ref, wd_ref, bd_ref, gn_ref, s0_ref, og_in_ref,
                       o_ref, s_ref, acc_ref, *, nb, q_scale):
    del og_in_ref
    bi = pl.program_id(1)

    @pl.when(bi == 0)
    def _():
        acc_ref[...] = jnp.zeros_like(acc_ref)

    rows = q_ref.shape[0]
    rid = lax.broadcasted_iota(jnp.int32, (rows, 1), 0)

    def pick(x):
        return jnp.sum(jnp.where(rid == bi, x, 0.0), axis=0, keepdims=True)

    def column(x_row, n):
        eye = lax.broadcasted_iota(jnp.int32, (n, n), 0) == lax.broadcasted_iota(jnp.int32, (n, n), 1)
        return jnp.sum(jnp.where(eye, x_row, 0.0), axis=1, keepdims=True)

    dk = q_ref.shape[1]
    q = pick(q_ref[...]) * q_scale
    k = pick(k_ref[...])
    v = pick(v_ref[...])
    z = _dot(lr_ref[...].astype(MXU_DTYPE), wd_ref[...].astype(MXU_DTYPE)) + bd_ref[...]
    decay = jnp.exp(pick(_log_sigmoid(z) / GLA_TAU))
    s0 = s0_ref[...]
    qd = jnp.broadcast_to(q * decay, (rows, dk)).astype(MXU_DTYPE)
    o = _dot(qd, s0.astype(MXU_DTYPE))[0:1]
    o = o + jnp.sum(q * k, axis=-1, keepdims=True) * v
    s_ref[...] = s0 * column(decay, dk) + column(k, dk) * v

    gated = _gla_out_gate(o, gn_ref[...], pick(r_ref[...]))
    acc = jnp.where(rid == bi, gated, acc_ref[...])
    acc_ref[...] = acc

    @pl.when(bi == nb - 1)
    def _():
        pad = jnp.zeros((o_ref.shape[0] - rows, o_ref.shape[1]), F32)
        o_ref[...] = jnp.concatenate([acc, pad], axis=0).astype(o_ref.dtype)


def _gla_decode(qkv, y, lr, wd, bd, gn, state, og, layer, nb, dk, dv, m_real):
    h_ = GLA_HEADS
    r8 = m_real // 8
    rdec = m_real // DEC_ROWS
    kern = functools.partial(_gla_decode_kernel, nb=nb, q_scale=dk ** -0.5)
    og_new, s_new = pl.pallas_call(
        kern,
        grid=(h_, nb),
        in_specs=[pl.BlockSpec((8, dk), lambda h, b: (r8, h)),
                  pl.BlockSpec((8, dk), lambda h, b: (r8, h_ + h)),
                  pl.BlockSpec((8, dv), lambda h, b: (r8, 2 * h_ * dk // dv + h)),
                  pl.BlockSpec((8, dv), lambda h, b: (r8, h)),
                  pl.BlockSpec((8, LANE), lambda h, b: (r8, 0)),
                  pl.BlockSpec((None, LANE, dk), lambda h, b: (layer, 0, h)),
                  pl.BlockSpec((None, 1, dk), lambda h, b: (layer, 0, h)),
                  pl.BlockSpec((None, 1, dv), lambda h, b: (layer, 0, 0)),
                  pl.BlockSpec((None, None, None, dk, dv), lambda h, b: (layer, b, h, 0, 0)),
                  pl.BlockSpec(memory_space=pl.ANY)],
        out_specs=[pl.BlockSpec((DEC_ROWS, dv), lambda h, b: (rdec, h)),
                   pl.BlockSpec((None, None, dk, dv), lambda h, b: (b, h, 0, 0))],
        out_shape=[jax.ShapeDtypeStruct(og.shape, og.dtype),
                   jax.ShapeDtypeStruct((nb, h_, dk, dv), F32)],
        scratch_shapes=[pltpu.VMEM((8, dv), F32)],
        input_output_aliases={9: 0},
        compiler_params=_params("arbitrary", "arbitrary"),
        name="gla_decode",
    )(qkv, qkv, qkv, y, lr, wd, bd, gn, state, og)
    return og_new, s_new


def _att_proj_kernel(h_ref, w_ref, g_ref, o_ref, *, n_norm):
    j = pl.program_id(1)
    acc = _dot_nt(h_ref[...], w_ref[0].astype(MXU_DTYPE))
    nh, _, hd = o_ref.shape

    @pl.when(j < n_norm)
    def _():
        for hh in range(nh):
            x = acc[:, hh * hd:(hh + 1) * hd]
            ms = jnp.mean(x * x, axis=-1, keepdims=True)
            o_ref[hh] = x * lax.rsqrt(ms + EPS) * g_ref[:, hh * hd:(hh + 1) * hd]

    @pl.when(j >= n_norm)
    def _():
        for hh in range(nh):
            o_ref[hh] = acc[:, hh * hd:(hh + 1) * hd]


def _att_proj(h, w, layer, row0, gains):
    m, k = h.shape
    n = 3 * ATT_W
    tm, tn = _mm_tiles(m, n)
    assert (2 * ATT_W) % tn == 0 and tn % ATT_HEAD_DIM == 0
    nh = tn // ATT_HEAD_DIM
    n_norm = 2 * ATT_W // tn
    return pl.pallas_call(
        functools.partial(_att_proj_kernel, n_norm=n_norm),
        grid=(m // tm, n // tn),
        in_specs=[pl.BlockSpec((tm, k), lambda i, j: (i, 0)), _w_rows_spec(k, tn, layer, row0),
                  pl.BlockSpec((None, 1, tn), lambda i, j: (layer, 0, jnp.minimum(j, n_norm - 1)))],
        out_specs=pl.BlockSpec((nh, tm, ATT_HEAD_DIM), lambda i, j: (j, i, 0)),
        out_shape=jax.ShapeDtypeStruct((n // ATT_HEAD_DIM, m, ATT_HEAD_DIM), F32),
        compiler_params=_params("arbitrary", "arbitrary"),
        name="in_proj_att",
    )(h, w, gains)


def _softmax_mix(os, ls):
    m = functools.reduce(jnp.maximum, ls)
    ws = [jnp.exp(l - m) for l in ls]
    den = functools.reduce(lambda a, b: a + b, ws)
    return functools.reduce(lambda a, b: a + b, [(w / den) * o for w, o in zip(ws, os)])


def _attn_kernel(slope_ref, q0, q1, q2, k0, k1, k2, v0, v1, v2, oa_ref, c0, c1, c2, o_s, l_s):
    j = pl.program_id(1)
    seq, hd = q0.shape
    tq = ATT_BLOCK
    row = lax.broadcasted_iota(jnp.int32, (tq, 2 * tq), 0)
    col = lax.broadcasted_iota(jnp.int32, (tq, 2 * tq), 1)
    steps = tq + row - col
    in_window = (steps >= 0) & (steps <= ATT_BLOCK)
    for g, (_, dil) in enumerate(ATT_GROUPS):
        q_ref, k_ref, v_ref = (q0, q1, q2)[g], (k0, k1, k2)[g], (v0, v1, v2)[g]
        nq = seq // (tq * dil)
        bias = slope_ref[g, j] * (steps * dil).astype(F32)

        def block(it, carry, q_ref=q_ref, k_ref=k_ref, v_ref=v_ref, g=g, dil=dil, nq=nq, bias=bias):
            r = it // nq
            i = it - r * nq
            start = r + i * (tq * dil)
            pstart = jnp.where(i > 0, start - tq * dil, start)

            def rows(st):
                return pl.ds(st, tq, stride=dil) if dil > 1 else pl.ds(pl.multiple_of(st, tq), tq)

            qh = q_ref[rows(start), :].astype(MXU_DTYPE)
            kk = jnp.concatenate([k_ref[rows(pstart), :], k_ref[rows(start), :]], axis=0).astype(MXU_DTYPE)
            vv = jnp.concatenate([v_ref[rows(pstart), :], v_ref[rows(start), :]], axis=0).astype(MXU_DTYPE)
            s = _dot_nt(qh, kk) * (hd ** -0.5)
            valid = in_window & ((col >= tq) | (i > 0))
            s = jnp.where(valid, s - bias, NEG)
            m = jnp.max(s, axis=-1, keepdims=True)
            p = jnp.exp(s - m)
            l = jnp.sum(p, axis=-1, keepdims=True)
            o_s[g, rows(start), :] = _dot(p.astype(MXU_DTYPE), vv) / l
            l_s[g, rows(start), :] = jnp.broadcast_to(m + jnp.log(l), (tq, hd))
            return carry

        lax.fori_loop(0, seq // tq, block, 0)
        c_ref = (c0, c1, c2)[g]
        win = c_ref.shape[1]
        c_ref[0] = k_ref[seq - win:, :]
        c_ref[1] = v_ref[seq - win:, :]
    n_g = len(ATT_GROUPS)
    oa_ref[...] = _softmax_mix([o_s[g] for g in range(n_g)], [l_s[g] for g in range(n_g)]).astype(oa_ref.dtype)


def _attn_prompt(att, slopes, bsz, seq, m_pad):
    for win, dil in ATT_GROUPS:
        assert win // dil == ATT_BLOCK and seq % (dil * ATT_BLOCK) == 0 and win <= seq
    hd = ATT_HEAD_DIM
    n_g = len(ATT_GROUPS)
    specs = [pl.BlockSpec((None, seq, hd), lambda b, j, h0=kind * ATT_HEADS + g * ATT_GROUP_HEADS: (h0 + j, b, 0))
             for kind in range(3) for g in range(n_g)]
    return pl.pallas_call(
        _attn_kernel,
        grid=(bsz, ATT_GROUP_HEADS),
        in_specs=[pl.BlockSpec(memory_space=pltpu.SMEM)] + specs,
        out_specs=[pl.BlockSpec((seq, hd), lambda b, j: (b, j))]
        + [pl.BlockSpec((2, None, None, win, hd), lambda b, j: (0, j, b, 0, 0)) for win, _ in ATT_GROUPS],
        out_shape=[jax.ShapeDtypeStruct((m_pad, ATT_GW), MXU_DTYPE)]
        + [jax.ShapeDtypeStruct((2, ATT_GROUP_HEADS, bsz, win, hd), F32) for win, _ in ATT_GROUPS],
        scratch_shapes=[pltpu.VMEM((n_g, seq, hd), F32), pltpu.VMEM((n_g, seq, hd), F32)],
        compiler_params=_params("arbitrary", "arbitrary"),
        name="attn_prompt",
    )(slopes, *([att] * 9))


def _attn_decode_kernel(slope_ref, a_ref, c0_ref, c1_ref, c2_ref, o_ref):
    nh = ATT_GROUP_HEADS
    hd = ATT_HEAD_DIM
    nrow = lax.broadcasted_iota(jnp.int32, (ATT_BLOCK, 1, 1), 0)
    os, ls = [], []
    for g, (_, dil) in enumerate(ATT_GROUPS):
        c_ref = (c0_ref, c1_ref, c2_ref)[g]
        q = a_ref[g * nh:(g + 1) * nh]
        kn = a_ref[ATT_HEADS + g * nh:ATT_HEADS + (g + 1) * nh]
        vn = a_ref[2 * ATT_HEADS + g * nh:2 * ATT_HEADS + (g + 1) * nh]
        kb = c_ref[:, 0]
        vb = c_ref[:, 1]
        slope = slope_ref[g][:, 0:1]
        dist = ((ATT_BLOCK - nrow) * dil).astype(F32)
        sb = jnp.sum(kb * q, axis=-1, keepdims=True) * (hd ** -0.5) - slope * dist
        sn = jnp.sum(kn * q, axis=-1, keepdims=True) * (hd ** -0.5)
        m = jnp.maximum(jnp.max(sb, axis=0), sn)
        pb = jnp.exp(sb - m)
        pn = jnp.exp(sn - m)
        l = jnp.sum(pb, axis=0) + pn
        os.append((jnp.sum(pb * vb, axis=0) + pn * vn) / l)
        ls.append(m + jnp.log(l))
    o_ref[...] = _softmax_mix(os, ls)


def _attn_decode(a_dec, slopes_v, caches, layer):
    nb = a_dec.shape[0]
    views, specs = [], []
    for (win, dil), cbuf in zip(ATT_GROUPS, caches):
        depth, nb_, wb = cbuf.shape[:3]
        assert wb == win and nb_ == nb
        views.append(cbuf.reshape(depth, nb, win // dil, dil, 2, ATT_GROUP_HEADS, ATT_HEAD_DIM))
        specs.append(pl.BlockSpec((None, None, win // dil, None, 2, ATT_GROUP_HEADS, ATT_HEAD_DIM),
                                  lambda b: (layer, b, 0, 0, 0, 0, 0)))
    return pl.pallas_call(
        _attn_decode_kernel,
        grid=(nb,),
        in_specs=[pl.BlockSpec(slopes_v.shape, lambda b: (0, 0, 0)),
                  pl.BlockSpec((None,) + a_dec.shape[1:], lambda b: (b, 0, 0))] + specs,
        out_specs=pl.BlockSpec((None, ATT_GROUP_HEADS, ATT_HEAD_DIM), lambda b: (b, 0, 0)),
        out_shape=jax.ShapeDtypeStruct((nb, ATT_GROUP_HEADS, ATT_HEAD_DIM), F32),
        compiler_params=_params("arbitrary"),
        name="attn_decode",
    )(slopes_v, a_dec, *views)


def _place_rows_kernel(x_ref, dst_ref, o_ref):
    del dst_ref
    pad = jnp.zeros((o_ref.shape[0] - x_ref.shape[0], o_ref.shape[1]), F32)
    o_ref[...] = jnp.concatenate([x_ref[...], pad], axis=0).astype(o_ref.dtype)


def _place_sample_rows(x, dst, m_real):
    n = dst.shape[1]
    return pl.pallas_call(
        _place_rows_kernel,
        grid=(1,),
        in_specs=[pl.BlockSpec(x.shape, lambda i: (0, 0)), pl.BlockSpec(memory_space=pl.ANY)],
        out_specs=pl.BlockSpec((DEC_ROWS, n), lambda i: (m_real // DEC_ROWS, 0)),
        out_shape=jax.ShapeDtypeStruct(dst.shape, dst.dtype),
        input_output_aliases={1: 0},
        compiler_params=_params("arbitrary"),
        name="place_sample_rows",
    )(x, dst)


def kernel(x_prompt, x_sample, state_gla, cache_w128, cache_w512, cache_w2048, p_prompt, p_sample, norm_ffn1, ffn1_w_in, ffn1_w_out, norm_mix, w_in, gla_w_decay, gla_b_decay, gla_norm, att_q_norm, att_k_norm, w_gla_out, w_att_out, w_out, norm_ffn2, ffn2_w_in, ffn2_w_out, norm_ple, w_ple_gate, w_ple_proj):
    bsz, seq, d = x_prompt.shape
    nb, dec_seq, _ = x_sample.shape
    depth = norm_ffn1.shape[0]
    assert dec_seq == 1 and nb == 8
    m_real = bsz * seq
    assert m_real % DEC_ROWS == 0
    m_pad = m_real + DEC_ROWS
    gla_qk = gla_w_decay.shape[-1]
    dk = gla_qk // GLA_HEADS
    gla_v = d
    dv = gla_v // GLA_HEADS
    caches = (cache_w128, cache_w512, cache_w2048)
    n_g = len(ATT_GROUPS)
    nh, hd = ATT_GROUP_HEADS, ATT_HEAD_DIM

    lr_off = 2 * gla_qk + gla_v
    r_off = lr_off + GLA_RANK
    q_off = r_off + gla_v
    z_off = q_off + 3 * ATT_W
    assert w_in.shape[-1] == z_off + 2 * d and lr_off % LANE == 0 and r_off % 8 == 0
    w_in_t = jnp.swapaxes(w_in, 1, 2)

    zrow = lambda n, w: jnp.zeros((n, w), F32)
    x = jnp.concatenate([x_prompt.reshape(m_real, d), x_sample.reshape(nb, d), zrow(DEC_ROWS - nb, d)], axis=0)
    pdim = p_prompt.shape[-1]
    pe = jnp.concatenate([p_prompt.reshape(depth, m_real, pdim), p_sample.reshape(depth, nb, pdim),
                          jnp.zeros((depth, DEC_ROWS - nb, pdim), F32)], axis=1)

    wd = jnp.zeros((depth, LANE, gla_qk), F32).at[:, :GLA_RANK].set(gla_w_decay)
    bd = gla_b_decay.reshape(depth, 1, gla_qk)
    gn = gla_norm.reshape(depth, 1, dv)
    att_gain = jnp.concatenate([jnp.tile(att_q_norm, (1, ATT_HEADS)), jnp.tile(att_k_norm, (1, ATT_HEADS))],
                               axis=1).reshape(depth, 1, 2 * ATT_W)
    slopes = jnp.exp2(-8.0 * jnp.arange(1, ATT_HEADS + 1, dtype=F32) / ATT_HEADS)
    slopes_s = slopes.reshape(n_g, nh)
    slopes_v = jnp.broadcast_to(slopes.reshape(n_g, nh, 1), (n_g, nh, LANE))
    g3 = lambda a: a.reshape(depth, 1, d)
    n_f1, n_mix, n_f2, n_ple = g3(norm_ffn1), g3(norm_mix), g3(norm_ffn2), g3(norm_ple)

    _, tn_g = _mm_tiles(m_pad, gla_v + 2 * d)
    n_r, n_att = gla_v // tn_g, 3 * ATT_W // tn_g
    gate_cols = lambda j: jnp.where(j < n_r, j, j + n_att)

    gla_p, gla_s, win_p, new_rows = [], [], [], []
    for i in range(depth):
        x = _ffn_out(_swiglu_in(_rmsnorm(x, n_f1, i), ffn1_w_in, i), ffn1_w_out, i, x, 0.5)

        h = _rmsnorm(x, n_mix, i)
        qkv = _matmul(h, w_in_t, i, 0, lr_off, F32, w_rows=True, name="in_proj_gla")
        lr = _matmul(h, w_in_t, i, lr_off, LANE, F32, tn=LANE, w_rows=True, name="in_proj_decay")
        gates = _matmul(h, w_in_t, i, r_off, gla_v + 2 * d, F32, col_map=gate_cols, w_rows=True, name="in_proj_gates")
        att = _att_proj(h, w_in_t, i, q_off, att_gain)

        og, s_p = _gla_prompt(qkv, gates, lr, wd, bd, gn, i, bsz, seq, dk, dv, m_pad)
        og, s_s = _gla_decode(qkv, gates, lr, wd, bd, gn, state_gla, og, i, nb, dk, dv, m_real)

        oa, *bufs = _attn_prompt(att, slopes_s, bsz, seq, m_pad)
        a_dec = jnp.transpose(att[:, m_real:m_real + nb], (1, 0, 2))
        o_dec = _attn_decode(a_dec, slopes_v, caches, i)
        oa = _place_sample_rows(o_dec.reshape(nb, ATT_GW), oa, m_real)

        merged = _merge(og, oa, w_gla_out, w_att_out, i, gates, gla_v, gla_v + d)
        x = _matmul_residual(merged, w_out, i, x)
        x = _ffn_out(_swiglu_in(_rmsnorm(x, n_f2, i), ffn2_w_in, i), ffn2_w_out, i, x, 0.5)
        x = _ple(_rmsnorm(x, n_ple, i), pe[i], w_ple_gate, w_ple_proj, i, x)

        win_p.append(bufs)
        new_rows.append(a_dec.reshape(nb, 3, n_g, nh, hd)[:, 1:])
        gla_p.append(s_p)
        gla_s.append(s_s)

    y_prompt = x[:m_real].reshape(bsz, seq, d)
    y_sample = x[m_real:m_real + nb].reshape(nb, 1, d)
    new_rows = jnp.stack(new_rows)
    win_s = [jnp.concatenate([caches[g][:, :, 1:], new_rows[:, :, None, :, g]], axis=2) for g in range(n_g)]
    win_p = [jnp.transpose(jnp.stack([b[g] for b in win_p]), (0, 3, 4, 1, 2, 5)) for g in range(n_g)]
    return (y_prompt, y_sample, jnp.stack(gla_p), win_p[0], win_p[1], win_p[2],
            jnp.stack(gla_s), win_s[0], win_s[1], win_s[2])
```

```python
import functools

import jax
import jax.numpy as jnp
from jax import lax
from jax.experimental import pallas as pl
from jax.experimental.pallas import tpu as pltpu

F32 = jnp.float32
MXU_DTYPE = jnp.bfloat16

EPS = 1e-6
GLA_HEADS = 4
GLA_RANK = 16
GLA_TAU = 16.0
GLA_CHUNK = 256
GLA_DIAG = 8
ATT_GROUPS = ((128, 1), (512, 4), (2048, 16))
ATT_GROUP_HEADS = 8
ATT_HEAD_DIM = 128
ATT_HEADS = len(ATT_GROUPS) * ATT_GROUP_HEADS
ATT_GW = ATT_GROUP_HEADS * ATT_HEAD_DIM
ATT_W = ATT_HEADS * ATT_HEAD_DIM
ATT_BLOCK = 128
DEC_ROWS = 128
LANE = 128
NEG = -1e30

VMEM_LIMIT = 56 * 1024 * 1024


def _params(*sem):
    return pltpu.CompilerParams(dimension_semantics=sem, vmem_limit_bytes=VMEM_LIMIT)


def _pick_tile(n, target, mult):
    best = None
    for t in range(mult, min(n, target) + 1, mult):
        if n % t == 0:
            best = t
    assert best is not None, (n, target, mult)
    return best


def _sigmoid(x):
    return 1.0 / (1.0 + jnp.exp(-x))


def _log_sigmoid(x):
    return jnp.minimum(x, 0.0) - jnp.log(1.0 + jnp.exp(-jnp.abs(x)))


def _dot(a, b):
    return jnp.dot(a, b, preferred_element_type=F32)


def _dot_nt(a, b):
    return lax.dot_general(a, b, (((1,), (1,)), ((), ())), preferred_element_type=F32)


def _dot_tn(a, b):
    return lax.dot_general(a, b, (((0,), (0,)), ((), ())), preferred_element_type=F32)


def _rmsnorm_kernel(x_ref, g_ref, o_ref):
    x = x_ref[...]
    ms = jnp.mean(x * x, axis=-1, keepdims=True)
    o_ref[...] = (x * lax.rsqrt(ms + EPS) * g_ref[...]).astype(o_ref.dtype)


def _rmsnorm(x, gain, layer):
    m, d = x.shape
    tr = _pick_tile(m, 512, 16)
    return pl.pallas_call(
        _rmsnorm_kernel,
        grid=(m // tr,),
        in_specs=[pl.BlockSpec((tr, d), lambda i: (i, 0)),
                  pl.BlockSpec((None, 1, d), lambda i: (layer, 0, 0))],
        out_specs=pl.BlockSpec((tr, d), lambda i: (i, 0)),
        out_shape=jax.ShapeDtypeStruct((m, d), MXU_DTYPE),
        compiler_params=_params("arbitrary"),
        name="rmsnorm",
    )(x, gain)


def _mm_tiles(m, n):
    return _pick_tile(m, 1100, 16), _pick_tile(n, 512, LANE)


def _w_spec(k, tn, layer, col_blk0):
    return pl.BlockSpec((None, k, tn), lambda i, j: (layer, 0, col_blk0 + j))


def _mm_kernel(h_ref, w_ref, o_ref, *, w_rows):
    w = (w_ref[0] if w_rows else w_ref[...]).astype(MXU_DTYPE)
    acc = _dot_nt(h_ref[...], w) if w_rows else _dot(h_ref[...], w)
    o_ref[...] = acc.astype(o_ref.dtype)


def _w_rows_spec(k, tn, layer, row0, col_map=None):
    col_map = col_map or (lambda j: j)
    return pl.BlockSpec((pl.Element(1), pl.Element(tn), pl.Element(k)),
                        lambda i, j: (layer, pl.multiple_of(row0 + col_map(j) * tn, 8), 0))


def _matmul(h, w, layer, col0, n, out_dtype, tn=None, col_map=None, w_rows=False, name="matmul"):
    m, k = h.shape
    tm, tn_ = _mm_tiles(m, n)
    tn = tn or tn_
    assert n % tn == 0
    if w_rows:
        w_spec = _w_rows_spec(k, tn, layer, col0, col_map)
    else:
        assert col0 % tn == 0
        col_map = col_map or (lambda j: j)
        w_spec = pl.BlockSpec((None, k, tn), lambda i, j: (layer, 0, col0 // tn + col_map(j)))
    return pl.pallas_call(
        functools.partial(_mm_kernel, w_rows=w_rows),
        grid=(m // tm, n // tn),
        in_specs=[pl.BlockSpec((tm, k), lambda i, j: (i, 0)), w_spec],
        out_specs=pl.BlockSpec((tm, tn), lambda i, j: (i, j)),
        out_shape=jax.ShapeDtypeStruct((m, n), out_dtype),
        compiler_params=_params("arbitrary", "arbitrary"),
        name=name,
    )(h, w)


def _mm_res_kernel(h_ref, w_ref, r_ref, o_ref):
    o_ref[...] = r_ref[...] + _dot(h_ref[...], w_ref[...].astype(MXU_DTYPE))


def _matmul_residual(h, w, layer, res):
    m, k = h.shape
    n = w.shape[-1]
    tm, tn = _mm_tiles(m, n)
    return pl.pallas_call(
        _mm_res_kernel,
        grid=(m // tm, n // tn),
        in_specs=[pl.BlockSpec((tm, k), lambda i, j: (i, 0)), _w_spec(k, tn, layer, 0),
                  pl.BlockSpec((tm, tn), lambda i, j: (i, j))],
        out_specs=pl.BlockSpec((tm, tn), lambda i, j: (i, j)),
        out_shape=jax.ShapeDtypeStruct((m, n), F32),
        compiler_params=_params("arbitrary", "arbitrary"),
        name="matmul_residual",
    )(h, w, res)


def _swiglu_kernel(h_ref, wg_ref, wu_ref, o_ref):
    h = h_ref[...]
    g = _dot(h, wg_ref[...].astype(MXU_DTYPE))
    u = _dot(h, wu_ref[...].astype(MXU_DTYPE))
    o_ref[...] = (g * _sigmoid(g) * u).astype(o_ref.dtype)


def _swiglu_in(h, w, layer):
    m, k = h.shape
    f = w.shape[-1] // 2
    tm = _pick_tile(m, 1100, 16)
    tn = _pick_tile(f, 256, LANE)
    nf = f // tn
    return pl.pallas_call(
        _swiglu_kernel,
        grid=(m // tm, nf),
        in_specs=[pl.BlockSpec((tm, k), lambda i, j: (i, 0)),
                  _w_spec(k, tn, layer, 0), _w_spec(k, tn, layer, nf)],
        out_specs=pl.BlockSpec((tm, tn), lambda i, j: (i, j)),
        out_shape=jax.ShapeDtypeStruct((m, f), MXU_DTYPE),
        compiler_params=_params("arbitrary", "arbitrary"),
        name="swiglu_in",
    )(h, w, w)


def _ffn_out_kernel(a_ref, w_ref, r_ref, o_ref, *, nk, k_valid_last, scale):
    kk = pl.program_id(2)
    tk = a_ref.shape[1]

    @pl.when(kk == 0)
    def _():
        o_ref[...] = jnp.zeros_like(o_ref)

    def accumulate(masked):
        a = a_ref[...]
        w = w_ref[...]
        if masked:
            a = jnp.where(lax.broadcasted_iota(jnp.int32, a.shape, 1) < k_valid_last, a, 0)
            w = jnp.where(lax.broadcasted_iota(jnp.int32, w.shape, 0) < k_valid_last, w, 0)
        o_ref[...] += _dot(a, w.astype(MXU_DTYPE))

    if k_valid_last == tk:
        accumulate(False)
    else:
        pl.when(kk < nk - 1)(functools.partial(accumulate, False))
        pl.when(kk == nk - 1)(functools.partial(accumulate, True))

    @pl.when(kk == nk - 1)
    def _():
        o_ref[...] = r_ref[...] + scale * o_ref[...]


def _ffn_out(a, w, layer, res, scale):
    m, k = a.shape
    n = w.shape[-1]
    tm = _pick_tile(m, 2200, 16)
    tn = _pick_tile(n, 1024, LANE)
    tk = 512
    nk = pl.cdiv(k, tk)
    kern = functools.partial(_ffn_out_kernel, nk=nk, k_valid_last=k - (nk - 1) * tk, scale=scale)
    return pl.pallas_call(
        kern,
        grid=(m // tm, n // tn, nk),
        in_specs=[pl.BlockSpec((tm, tk), lambda i, j, kk: (i, kk)),
                  pl.BlockSpec((None, tk, tn), lambda i, j, kk: (layer, kk, j)),
                  pl.BlockSpec((tm, tn), lambda i, j, kk: (i, j))],
        out_specs=pl.BlockSpec((tm, tn), lambda i, j, kk: (i, j)),
        out_shape=jax.ShapeDtypeStruct((m, n), F32),
        compiler_params=_params("arbitrary", "arbitrary", "arbitrary"),
        name="ffn_out",
    )(a, w, res)


def _merge_kernel(og_ref, oa_ref, wg_ref, wa_ref, zg_ref, za_ref, o_ref):
    bg = _dot(og_ref[...], wg_ref[...].astype(MXU_DTYPE))
    ba = _dot(oa_ref[...], wa_ref[...].astype(MXU_DTYPE))
    o_ref[...] = (_sigmoid(zg_ref[...]) * bg + _sigmoid(za_ref[...]) * ba).astype(o_ref.dtype)


def _merge(og, oa, w_gla_out, w_att_out, layer, y, zg_off, za_off):
    m, kg = og.shape
    ka = oa.shape[1]
    n = w_gla_out.shape[-1]
    tm, _ = _mm_tiles(m, n)
    tn = _pick_tile(n, 256, LANE)
    assert zg_off % tn == 0 and za_off % tn == 0
    return pl.pallas_call(
        _merge_kernel,
        grid=(m // tm, n // tn),
        in_specs=[pl.BlockSpec((tm, kg), lambda i, j: (i, 0)), pl.BlockSpec((tm, ka), lambda i, j: (i, 0)),
                  _w_spec(kg, tn, layer, 0), _w_spec(ka, tn, layer, 0),
                  pl.BlockSpec((tm, tn), lambda i, j: (i, zg_off // tn + j)),
                  pl.BlockSpec((tm, tn), lambda i, j: (i, za_off // tn + j))],
        out_specs=pl.BlockSpec((tm, tn), lambda i, j: (i, j)),
        out_shape=jax.ShapeDtypeStruct((m, n), MXU_DTYPE),
        compiler_params=_params("arbitrary", "arbitrary"),
        name="merge",
    )(og, oa, w_gla_out, w_att_out, y, y)


def _ple_kernel(h_ref, pe_ref, wg_ref, wp_ref, r_ref, o_ref, *tail_ref):
    gate = _dot(h_ref[...], wg_ref[...].astype(MXU_DTYPE))
    proj = _dot(pe_ref[...].astype(MXU_DTYPE), wp_ref[...].astype(MXU_DTYPE))
    out = r_ref[...] + _sigmoid(gate) * proj
    o_ref[...] = out
    if tail_ref:
        tail_ref[0][...] = out[out.shape[0] - DEC_ROWS:]


def _ple(h, pe, w_gate, w_proj, layer, res, m_real=None):
    m, k = h.shape
    kp = pe.shape[1]
    n = w_gate.shape[-1]
    tm, tn = _mm_tiles(m, n)
    split = m_real is not None
    assert not split or (m - m_real == DEC_ROWS and tm >= DEC_ROWS)
    main = pl.BlockSpec((tm, tn), lambda i, j: (i, j))
    return pl.pallas_call(
        _ple_kernel,
        grid=(m // tm, n // tn),
        in_specs=[pl.BlockSpec((tm, k), lambda i, j: (i, 0)), pl.BlockSpec((tm, kp), lambda i, j: (i, 0)),
                  _w_spec(k, tn, layer, 0), _w_spec(kp, tn, layer, 0),
                  pl.BlockSpec((tm, tn), lambda i, j: (i, j))],
        out_specs=[main, pl.BlockSpec((DEC_ROWS, tn), lambda i, j: (i, j))] if split else main,
        out_shape=[jax.ShapeDtypeStruct((m_real, n), F32), jax.ShapeDtypeStruct((m // tm * DEC_ROWS, n), F32)] if split
        else jax.ShapeDtypeStruct((m, n), F32),
        compiler_params=_params("arbitrary", "arbitrary"),
        name="ple",
    )(h, pe, w_gate, w_proj, res)


def _split_cumsum(tri, x):
    hi = x.astype(MXU_DTYPE)
    r1 = x - hi.astype(F32)
    mid = r1.astype(MXU_DTYPE)
    lo = (r1 - mid.astype(F32)).astype(MXU_DTYPE)
    return _dot(tri, hi) + _dot(tri, mid) + _dot(tri, lo)


def _gla_out_gate(o, gn, r):
    ms = jnp.mean(o * o, axis=-1, keepdims=True)
    return o * lax.rsqrt(ms + EPS) * gn * (r * _sigmoid(r))


def _gla_kernel(q_ref, k_ref, v_ref, r_ref, lr_ref, wd_ref, bd_ref, gn_ref, *rest, nc, q_scale, aliased):
    o_ref, s_ref, st_ref, b_s, ad_s = rest[1:] if aliased else rest
    c = pl.program_id(2)

    @pl.when(c == 0)
    def _():
        st_ref[...] = jnp.zeros_like(st_ref)

    cs, dk = q_ref.shape
    q = q_ref[...] * q_scale
    k = k_ref[...]
    v = v_ref[...].astype(MXU_DTYPE)
    z = _dot(lr_ref[...].astype(MXU_DTYPE), wd_ref[...].astype(MXU_DTYPE)) + bd_ref[...]
    log_a = _log_sigmoid(z) / GLA_TAU
    row = lax.broadcasted_iota(jnp.int32, (cs, cs), 0)
    col = lax.broadcasted_iota(jnp.int32, (cs, cs), 1)
    b = _split_cumsum(jnp.where(row >= col, 1.0, 0.0).astype(MXU_DTYPE), log_a)
    b_s[...] = b

    st = st_ref[...]
    o = _dot_nt((q * jnp.exp(b)).astype(MXU_DTYPE), st.astype(MXU_DTYPE))

    rowv = lax.broadcasted_iota(jnp.int32, (cs, 1), 0)
    a = jnp.zeros((cs, cs), F32)
    half = cs // 2
    while half >= GLA_DIAG:
        blk = 2 * half
        shift = blk.bit_length() - 1
        rho = jnp.broadcast_to(b.reshape(cs // blk, blk, dk)[:, half - 1:half, :], (cs // blk, blk, dk)).reshape(cs, dk)
        bottom = (rowv & (blk - 1)) >= half
        f = jnp.exp(jnp.where(bottom, b - rho, rho - b))
        qs = jnp.where(bottom, q * f, 0.0).astype(MXU_DTYPE)
        ks = jnp.where(bottom, 0.0, k * f).astype(MXU_DTYPE)
        same_block = (row >> shift) == (col >> shift)
        a = a + jnp.where(same_block, _dot_nt(qs, ks), 0.0)
        half //= 2

    dg = GLA_DIAG
    tcol = lax.broadcasted_iota(jnp.int32, (dg, 1), 0)
    acol = lax.broadcasted_iota(jnp.int32, (dg, cs), 1)

    def diag_block(i, carry):
        r0 = pl.multiple_of(i * dg, dg)
        b_i = b_s[pl.ds(r0, dg), :]
        q_i = q_ref[pl.ds(r0, dg), :] * q_scale
        k_i = k_ref[pl.ds(r0, dg), :]
        a_blk = jnp.zeros((dg, cs), F32)
        for s in range(dg):
            term = jnp.sum(q_i * k_i[s:s + 1] * jnp.exp(b_i - b_i[s:s + 1]), axis=-1, keepdims=True)
            a_blk = a_blk + jnp.where((acol == r0 + s) & (tcol >= s), term, 0.0)
        ad_s[pl.ds(r0, dg), :] = a_blk
        return carry

    lax.fori_loop(0, cs // dg, diag_block, 0, unroll=4)
    a = a + ad_s[...]
    o = o + _dot(a.astype(MXU_DTYPE), v)

    b_last = b[cs - 1:cs]
    kd = (k * jnp.exp(b_last - b)).astype(MXU_DTYPE)
    st_new = st * jnp.exp(b_last) + _dot_tn(v, kd)
    st_ref[...] = st_new

    o_ref[...] = _gla_out_gate(o, gn_ref[...], r_ref[...]).astype(o_ref.dtype)

    @pl.when(c == nc - 1)
    def _():
        s_ref[...] = st_new.T


def _gla_prompt(qkv, y, lr, wd, bd, gn, s_all, layer, depth, bsz, seq, dk, dv, m_pad):
    cs = min(GLA_CHUNK, seq)
    assert seq % cs == 0 and cs % (2 * GLA_DIAG) == 0 and cs & (cs - 1) == 0
    nc = seq // cs
    h_ = GLA_HEADS
    aliased = s_all is not None
    kern = functools.partial(_gla_kernel, nc=nc, q_scale=dk ** -0.5, aliased=aliased)
    row = lambda b, h, c: b * nc + c
    in_specs = [pl.BlockSpec((cs, dk), lambda b, h, c: (row(b, h, c), h)),
                pl.BlockSpec((cs, dk), lambda b, h, c: (row(b, h, c), h_ + h)),
                pl.BlockSpec((cs, dv), lambda b, h, c: (row(b, h, c), 2 * h_ * dk // dv + h)),
                pl.BlockSpec((cs, dv), lambda b, h, c: (row(b, h, c), h)),
                pl.BlockSpec((cs, LANE), lambda b, h, c: (row(b, h, c), 0)),
                pl.BlockSpec((None, LANE, dk), lambda b, h, c: (layer, 0, h)),
                pl.BlockSpec((None, 1, dk), lambda b, h, c: (layer, 0, h)),
                pl.BlockSpec((None, 1, dv), lambda b, h, c: (layer, 0, 0))]
    args = [qkv, qkv, qkv, y, lr, wd, bd, gn]
    if aliased:
        in_specs.append(pl.BlockSpec(memory_space=pl.ANY))
        args.append(s_all)
    return pl.pallas_call(
        kern,
        grid=(bsz, h_, nc),
        in_specs=in_specs,
        out_specs=[pl.BlockSpec((cs, dv), lambda b, h, c: (row(b, h, c), h)),
                   pl.BlockSpec((None, None, None, dk, dv), lambda b, h, c: (layer, b, h, 0, 0))],
        out_shape=[jax.ShapeDtypeStruct((m_pad, h_ * dv), MXU_DTYPE),
                   jax.ShapeDtypeStruct((depth, bsz, h_, dk, dv), F32)],
        scratch_shapes=[pltpu.VMEM((dv, dk), F32), pltpu.VMEM((cs, dk), F32), pltpu.VMEM((cs, cs), F32)],
        input_output_aliases={8: 1} if aliased else {},
        compiler_params=_params("arbitrary", "arbitrary", "arbitrary"),
        name="gla_prompt",
    )(*args)


def _gla_decode_kernel(q_ref, k_ref, v_ref, r_ref, lr_ref, wd_ref, bd_ref, gn_ref, s0_ref, *rest, nb, q_scale):
    o_ref, s_ref, acc_ref = rest[-3:]
    bi = pl.program_id(1)

    @pl.when(bi == 0)
    def _():
        acc_ref[...] = jnp.zeros_like(acc_ref)

    rows = q_ref.shape[0]
    rid = lax.broadcasted_iota(jnp.int32, (rows, 1), 0)

    def pick(x):
        return jnp.sum(jnp.where(rid == bi, x, 0.0), axis=0, keepdims=True)

    def column(x_row, n):
        eye = lax.broadcasted_iota(jnp.int32, (n, n), 0) == lax.broadcasted_iota(jnp.int32, (n, n), 1)
        return jnp.sum(jnp.where(eye, x_row, 0.0), axis=1, keepdims=True)

    dk = q_ref.shape[1]
    q = pick(q_ref[...]) * q_scale
    k = pick(k_ref[...])
    v = pick(v_ref[...])
    z = _dot(lr_ref[...].astype(MXU_DTYPE), wd_ref[...].astype(MXU_DTYPE)) + bd_ref[...]
    decay = jnp.exp(pick(_log_sigmoid(z) / GLA_TAU))
    s0 = s0_ref[...]
    qd = jnp.broadcast_to(q * decay, (rows, dk)).astype(MXU_DTYPE)
    o = _dot(qd, s0.astype(MXU_DTYPE))[0:1]
    o = o + jnp.sum(q * k, axis=-1, keepdims=True) * v
    s_ref[...] = s0 * column(decay, dk) + column(k, dk) * v

    gated = _gla_out_gate(o, gn_ref[...], pick(r_ref[...]))
    acc = jnp.where(rid == bi, gated, acc_ref[...])
    acc_ref[...] = acc

    @pl.when(bi == nb - 1)
    def _():
        pad = jnp.zeros((o_ref.shape[0] - rows, o_ref.shape[1]), F32)
        o_ref[...] = jnp.concatenate([acc, pad], axis=0).astype(o_ref.dtype)


def _gla_decode(qkv, y, lr, wd, bd, gn, state, og, s_all, layer, nb, dk, dv, m_real):
    h_ = GLA_HEADS
    r8 = m_real // 8
    rdec = m_real // DEC_ROWS
    kern = functools.partial(_gla_decode_kernel, nb=nb, q_scale=dk ** -0.5)
    any_spec = pl.BlockSpec(memory_space=pl.ANY)
    extra = [] if s_all is None else [s_all]
    og_new, s_new = pl.pallas_call(
        kern,
        grid=(h_, nb),
        in_specs=[pl.BlockSpec((8, dk), lambda h, b: (r8, h)),
                  pl.BlockSpec((8, dk), lambda h, b: (r8, h_ + h)),
                  pl.BlockSpec((8, dv), lambda h, b: (r8, 2 * h_ * dk // dv + h)),
                  pl.BlockSpec((8, dv), lambda h, b: (r8, h)),
                  pl.BlockSpec((8, LANE), lambda h, b: (r8, 0)),
                  pl.BlockSpec((None, LANE, dk), lambda h, b: (layer, 0, h)),
                  pl.BlockSpec((None, 1, dk), lambda h, b: (layer, 0, h)),
                  pl.BlockSpec((None, 1, dv), lambda h, b: (layer, 0, 0)),
                  pl.BlockSpec((None, None, None, dk, dv), lambda h, b: (layer, b, h, 0, 0)),
                  any_spec] + [any_spec] * len(extra),
        out_specs=[pl.BlockSpec((DEC_ROWS, dv), lambda h, b: (rdec, h)),
                   pl.BlockSpec((None, None, None, dk, dv), lambda h, b: (layer, b, h, 0, 0))],
        out_shape=[jax.ShapeDtypeStruct(og.shape, og.dtype),
                   jax.ShapeDtypeStruct(state.shape, F32)],
        scratch_shapes=[pltpu.VMEM((8, dv), F32)],
        input_output_aliases={9: 0, 10: 1} if extra else {9: 0},
        compiler_params=_params("arbitrary", "arbitrary"),
        name="gla_decode",
    )(qkv, qkv, qkv, y, lr, wd, bd, gn, state, og, *extra)
    return og_new, s_new


def _att_proj_kernel(h_ref, w_ref, g_ref, o_ref, *, n_norm):
    j = pl.program_id(1)
    acc = _dot_nt(h_ref[...], w_ref[0].astype(MXU_DTYPE))
    nh, _, hd = o_ref.shape

    @pl.when(j < n_norm)
    def _():
        for hh in range(nh):
            x = acc[:, hh * hd:(hh + 1) * hd]
            ms = jnp.mean(x * x, axis=-1, keepdims=True)
            o_ref[hh] = x * lax.rsqrt(ms + EPS) * g_ref[:, hh * hd:(hh + 1) * hd]

    @pl.when(j >= n_norm)
    def _():
        for hh in range(nh):
            o_ref[hh] = acc[:, hh * hd:(hh + 1) * hd]


def _att_proj(h, w, layer, row0, gains):
    m, k = h.shape
    n = 3 * ATT_W
    tm, tn = _mm_tiles(m, n)
    assert (2 * ATT_W) % tn == 0 and tn % ATT_HEAD_DIM == 0
    nh = tn // ATT_HEAD_DIM
    n_norm = 2 * ATT_W // tn
    return pl.pallas_call(
        functools.partial(_att_proj_kernel, n_norm=n_norm),
        grid=(m // tm, n // tn),
        in_specs=[pl.BlockSpec((tm, k), lambda i, j: (i, 0)), _w_rows_spec(k, tn, layer, row0),
                  pl.BlockSpec((None, 1, tn), lambda i, j: (layer, 0, jnp.minimum(j, n_norm - 1)))],
        out_specs=pl.BlockSpec((nh, tm, ATT_HEAD_DIM), lambda i, j: (j, i, 0)),
        out_shape=jax.ShapeDtypeStruct((n // ATT_HEAD_DIM, m, ATT_HEAD_DIM), F32),
        compiler_params=_params("arbitrary", "arbitrary"),
        name="in_proj_att",
    )(h, w, gains)


def _softmax_mix(os, ls):
    m = functools.reduce(jnp.maximum, ls)
    ws = [jnp.exp(l - m) for l in ls]
    den = functools.reduce(lambda a, b: a + b, ws)
    return functools.reduce(lambda a, b: a + b, [(w / den) * o for w, o in zip(ws, os)])


def _attn_kernel(slope_ref, q0, q1, q2, k0, k1, k2, v0, v1, v2, oa_ref, c0, c1, c2, o_s, l_s):
    j = pl.program_id(1)
    seq, hd = q0.shape
    tq = ATT_BLOCK
    row = lax.broadcasted_iota(jnp.int32, (tq, 2 * tq), 0)
    col = lax.broadcasted_iota(jnp.int32, (tq, 2 * tq), 1)
    steps = tq + row - col
    in_window = (steps >= 0) & (steps <= ATT_BLOCK)
    for g, (_, dil) in enumerate(ATT_GROUPS):
        q_ref, k_ref, v_ref = (q0, q1, q2)[g], (k0, k1, k2)[g], (v0, v1, v2)[g]
        nq = seq // (tq * dil)
        bias = slope_ref[g, j] * (steps * dil).astype(F32)

        def block(it, carry, q_ref=q_ref, k_ref=k_ref, v_ref=v_ref, g=g, dil=dil, nq=nq, bias=bias):
            r = it // nq
            i = it - r * nq
            start = r + i * (tq * dil)
            pstart = jnp.where(i > 0, start - tq * dil, start)

            def rows(st):
                return pl.ds(st, tq, stride=dil) if dil > 1 else pl.ds(pl.multiple_of(st, tq), tq)

            qh = q_ref[rows(start), :].astype(MXU_DTYPE)
            kk = jnp.concatenate([k_ref[rows(pstart), :], k_ref[rows(start), :]], axis=0).astype(MXU_DTYPE)
            vv = jnp.concatenate([v_ref[rows(pstart), :], v_ref[rows(start), :]], axis=0).astype(MXU_DTYPE)
            s = _dot_nt(qh, kk) * (hd ** -0.5)
            valid = in_window & ((col >= tq) | (i > 0))
            s = jnp.where(valid, s - bias, NEG)
            m = jnp.max(s, axis=-1, keepdims=True)
            p = jnp.exp(s - m)
            l = jnp.sum(p, axis=-1, keepdims=True)
            o_s[g, rows(start), :] = _dot(p.astype(MXU_DTYPE), vv) / l
            l_s[g, rows(start), :] = jnp.broadcast_to(m + jnp.log(l), (tq, hd))
            return carry

        lax.fori_loop(0, seq // tq, block, 0, unroll=4)
        c_ref = (c0, c1, c2)[g]
        win = c_ref.shape[1]
        c_ref[0] = k_ref[seq - win:, :]
        c_ref[1] = v_ref[seq - win:, :]
    n_g = len(ATT_GROUPS)
    oa_ref[...] = _softmax_mix([o_s[g] for g in range(n_g)], [l_s[g] for g in range(n_g)]).astype(oa_ref.dtype)


def _attn_prompt(att, slopes, bsz, seq, m_pad):
    for win, dil in ATT_GROUPS:
        assert win // dil == ATT_BLOCK and seq % (dil * ATT_BLOCK) == 0 and win <= seq
    hd = ATT_HEAD_DIM
    n_g = len(ATT_GROUPS)
    specs = [pl.BlockSpec((None, seq, hd), lambda b, j, h0=kind * ATT_HEADS + g * ATT_GROUP_HEADS: (h0 + j, b, 0))
             for kind in range(3) for g in range(n_g)]
    return pl.pallas_call(
        _attn_kernel,
        grid=(bsz, ATT_GROUP_HEADS),
        in_specs=[pl.BlockSpec(memory_space=pltpu.SMEM)] + specs,
        out_specs=[pl.BlockSpec((seq, hd), lambda b, j: (b, j))]
        + [pl.BlockSpec((2, None, None, win, hd), lambda b, j: (0, j, b, 0, 0)) for win, _ in ATT_GROUPS],
        out_shape=[jax.ShapeDtypeStruct((m_pad, ATT_GW), MXU_DTYPE)]
        + [jax.ShapeDtypeStruct((2, ATT_GROUP_HEADS, bsz, win, hd), F32) for win, _ in ATT_GROUPS],
        scratch_shapes=[pltpu.VMEM((n_g, seq, hd), F32), pltpu.VMEM((n_g, seq, hd), F32)],
        compiler_params=_params("arbitrary", "arbitrary"),
        name="attn_prompt",
    )(slopes, *([att] * 9))


def _attn_decode_kernel(slope_ref, a_ref, c0_ref, c1_ref, c2_ref, o_ref):
    nh = ATT_GROUP_HEADS
    hd = ATT_HEAD_DIM
    nrow = lax.broadcasted_iota(jnp.int32, (ATT_BLOCK, 1, 1), 0)
    os, ls = [], []
    for g, (_, dil) in enumerate(ATT_GROUPS):
        c_ref = (c0_ref, c1_ref, c2_ref)[g]
        q = a_ref[g * nh:(g + 1) * nh]
        kn = a_ref[ATT_HEADS + g * nh:ATT_HEADS + (g + 1) * nh]
        vn = a_ref[2 * ATT_HEADS + g * nh:2 * ATT_HEADS + (g + 1) * nh]
        kb = c_ref[:, 0]
        vb = c_ref[:, 1]
        slope = slope_ref[g][:, 0:1]
        dist = ((ATT_BLOCK - nrow) * dil).astype(F32)
        sb = jnp.sum(kb * q, axis=-1, keepdims=True) * (hd ** -0.5) - slope * dist
        sn = jnp.sum(kn * q, axis=-1, keepdims=True) * (hd ** -0.5)
        m = jnp.maximum(jnp.max(sb, axis=0), sn)
        pb = jnp.exp(sb - m)
        pn = jnp.exp(sn - m)
        l = jnp.sum(pb, axis=0) + pn
        os.append((jnp.sum(pb * vb, axis=0) + pn * vn) / l)
        ls.append(m + jnp.log(l))
    o_ref[...] = _softmax_mix(os, ls)


def _attn_decode(a_dec, slopes_v, caches, layer):
    nb = a_dec.shape[0]
    views, specs = [], []
    for (win, dil), cbuf in zip(ATT_GROUPS, caches):
        depth, nb_, wb = cbuf.shape[:3]
        assert wb == win and nb_ == nb
        views.append(cbuf.reshape(depth, nb, win // dil, dil, 2, ATT_GROUP_HEADS, ATT_HEAD_DIM))
        specs.append(pl.BlockSpec((None, None, win // dil, None, 2, ATT_GROUP_HEADS, ATT_HEAD_DIM),
                                  lambda b: (layer, b, 0, 0, 0, 0, 0)))
    return pl.pallas_call(
        _attn_decode_kernel,
        grid=(nb,),
        in_specs=[pl.BlockSpec(slopes_v.shape, lambda b: (0, 0, 0)),
                  pl.BlockSpec((None,) + a_dec.shape[1:], lambda b: (b, 0, 0))] + specs,
        out_specs=pl.BlockSpec((None, ATT_GROUP_HEADS, ATT_HEAD_DIM), lambda b: (b, 0, 0)),
        out_shape=jax.ShapeDtypeStruct((nb, ATT_GROUP_HEADS, ATT_HEAD_DIM), F32),
        compiler_params=_params("arbitrary"),
        name="attn_decode",
    )(slopes_v, a_dec, *views)


def _place_rows_kernel(x_ref, dst_ref, o_ref):
    del dst_ref
    pad = jnp.zeros((o_ref.shape[0] - x_ref.shape[0], o_ref.shape[1]), F32)
    o_ref[...] = jnp.concatenate([x_ref[...], pad], axis=0).astype(o_ref.dtype)


def _place_sample_rows(x, dst, m_real):
    n = dst.shape[1]
    return pl.pallas_call(
        _place_rows_kernel,
        grid=(1,),
        in_specs=[pl.BlockSpec(x.shape, lambda i: (0, 0)), pl.BlockSpec(memory_space=pl.ANY)],
        out_specs=pl.BlockSpec((DEC_ROWS, n), lambda i: (m_real // DEC_ROWS, 0)),
        out_shape=jax.ShapeDtypeStruct(dst.shape, dst.dtype),
        input_output_aliases={1: 0},
        compiler_params=_params("arbitrary"),
        name="place_sample_rows",
    )(x, dst)


def kernel(x_prompt, x_sample, state_gla, cache_w128, cache_w512, cache_w2048, p_prompt, p_sample, norm_ffn1, ffn1_w_in, ffn1_w_out, norm_mix, w_in, gla_w_decay, gla_b_decay, gla_norm, att_q_norm, att_k_norm, w_gla_out, w_att_out, w_out, norm_ffn2, ffn2_w_in, ffn2_w_out, norm_ple, w_ple_gate, w_ple_proj):
    bsz, seq, d = x_prompt.shape
    nb, dec_seq, _ = x_sample.shape
    depth = norm_ffn1.shape[0]
    assert dec_seq == 1 and nb == 8
    m_real = bsz * seq
    assert m_real % DEC_ROWS == 0
    m_pad = m_real + DEC_ROWS
    gla_qk = gla_w_decay.shape[-1]
    dk = gla_qk // GLA_HEADS
    gla_v = d
    dv = gla_v // GLA_HEADS
    caches = (cache_w128, cache_w512, cache_w2048)
    n_g = len(ATT_GROUPS)
    nh, hd = ATT_GROUP_HEADS, ATT_HEAD_DIM

    lr_off = 2 * gla_qk + gla_v
    r_off = lr_off + GLA_RANK
    q_off = r_off + gla_v
    z_off = q_off + 3 * ATT_W
    assert w_in.shape[-1] == z_off + 2 * d and lr_off % LANE == 0 and r_off % 8 == 0
    w_in_t = jnp.swapaxes(w_in, 1, 2)

    zrow = lambda n, w: jnp.zeros((n, w), F32)
    x = jnp.concatenate([x_prompt.reshape(m_real, d), x_sample.reshape(nb, d), zrow(DEC_ROWS - nb, d)], axis=0)
    pdim = p_prompt.shape[-1]
    pe = jnp.concatenate([p_prompt.reshape(depth, m_real, pdim), p_sample.reshape(depth, nb, pdim),
                          jnp.zeros((depth, DEC_ROWS - nb, pdim), F32)], axis=1)

    wd = jnp.zeros((depth, LANE, gla_qk), F32).at[:, :GLA_RANK].set(gla_w_decay)
    bd = gla_b_decay.reshape(depth, 1, gla_qk)
    gn = gla_norm.reshape(depth, 1, dv)
    att_gain = jnp.concatenate([jnp.tile(att_q_norm, (1, ATT_HEADS)), jnp.tile(att_k_norm, (1, ATT_HEADS))],
                               axis=1).reshape(depth, 1, 2 * ATT_W)
    slopes = jnp.exp2(-8.0 * jnp.arange(1, ATT_HEADS + 1, dtype=F32) / ATT_HEADS)
    slopes_s = slopes.reshape(n_g, nh)
    slopes_v = jnp.broadcast_to(slopes.reshape(n_g, nh, 1), (n_g, nh, LANE))
    g3 = lambda a: a.reshape(depth, 1, d)
    n_f1, n_mix, n_f2, n_ple = g3(norm_ffn1), g3(norm_mix), g3(norm_ffn2), g3(norm_ple)

    _, tn_g = _mm_tiles(m_pad, gla_v + 2 * d)
    n_r, n_att = gla_v // tn_g, 3 * ATT_W // tn_g
    gate_cols = lambda j: jnp.where(j < n_r, j, j + n_att)

    gla_p, gla_s, win_p, new_rows = None, None, [], []
    for i in range(depth):
        x = _ffn_out(_swiglu_in(_rmsnorm(x, n_f1, i), ffn1_w_in, i), ffn1_w_out, i, x, 0.5)

        h = _rmsnorm(x, n_mix, i)
        qkv = _matmul(h, w_in_t, i, 0, lr_off, F32, w_rows=True, name="in_proj_gla")
        lr = _matmul(h, w_in_t, i, lr_off, LANE, F32, tn=LANE, w_rows=True, name="in_proj_decay")
        gates = _matmul(h, w_in_t, i, r_off, gla_v + 2 * d, F32, col_map=gate_cols, w_rows=True, name="in_proj_gates")
        att = _att_proj(h, w_in_t, i, q_off, att_gain)

        og, gla_p = _gla_prompt(qkv, gates, lr, wd, bd, gn, gla_p, i, depth, bsz, seq, dk, dv, m_pad)
        og, gla_s = _gla_decode(qkv, gates, lr, wd, bd, gn, state_gla, og, gla_s, i, nb, dk, dv, m_real)

        oa, *bufs = _attn_prompt(att, slopes_s, bsz, seq, m_pad)
        a_dec = jnp.transpose(att[:, m_real:m_real + nb], (1, 0, 2))
        o_dec = _attn_decode(a_dec, slopes_v, caches, i)
        oa = _place_sample_rows(o_dec.reshape(nb, ATT_GW), oa, m_real)

        merged = _merge(og, oa, w_gla_out, w_att_out, i, gates, gla_v, gla_v + d)
        x = _matmul_residual(merged, w_out, i, x)
        x = _ffn_out(_swiglu_in(_rmsnorm(x, n_f2, i), ffn2_w_in, i), ffn2_w_out, i, x, 0.5)
        x = _ple(_rmsnorm(x, n_ple, i), pe[i], w_ple_gate, w_ple_proj, i, x, m_real if i == depth - 1 else None)

        win_p.append(bufs)
        new_rows.append(a_dec.reshape(nb, 3, n_g, nh, hd)[:, 1:])

    y_prompt = x[0].reshape(bsz, seq, d)
    y_sample = x[1][x[1].shape[0] - DEC_ROWS:][:nb].reshape(nb, 1, d)
    new_rows = jnp.stack(new_rows)
    win_s = [jnp.concatenate([caches[g][:, :, 1:], new_rows[:, :, None, :, g]], axis=2) for g in range(n_g)]
    win_p = [jnp.transpose(jnp.stack([b[g] for b in win_p]), (0, 3, 4, 1, 2, 5)) for g in range(n_g)]
    return (y_prompt, y_sample, gla_p, win_p[0], win_p[1], win_p[2], gla_s, win_s[0], win_s[1], win_s[2])
```

```python
import functools

import jax
import jax.numpy as jnp
from jax import lax
from jax.experimental import pallas as pl
from jax.experimental.pallas import tpu as pltpu

F32 = jnp.float32
MXU_DTYPE = jnp.bfloat16

EPS = 1e-6
GLA_HEADS = 4
GLA_RANK = 16
GLA_TAU = 16.0
GLA_CHUNK = 256
GLA_SUPER = 128
ATT_GROUPS = ((128, 1), (512, 4), (2048, 16))
ATT_GROUP_HEADS = 8
ATT_HEAD_DIM = 128
ATT_HEADS = len(ATT_GROUPS) * ATT_GROUP_HEADS
ATT_GW = ATT_GROUP_HEADS * ATT_HEAD_DIM
ATT_W = ATT_HEADS * ATT_HEAD_DIM
ATT_BLOCK = 128
DEC_ROWS = 128
LANE = 128
NEG = -1e30

VMEM_LIMIT = 56 * 1024 * 1024


def _params(*sem):
    return pltpu.CompilerParams(dimension_semantics=sem, vmem_limit_bytes=VMEM_LIMIT)


def _pick_tile(n, target, mult):
    best = None
    for t in range(mult, min(n, target) + 1, mult):
        if n % t == 0:
            best = t
    assert best is not None, (n, target, mult)
    return best


def _sigmoid(x):
    return 1.0 / (1.0 + jnp.exp(-x))


def _log_sigmoid(x):
    return jnp.minimum(x, 0.0) - jnp.log(1.0 + jnp.exp(-jnp.abs(x)))


def _dot(a, b):
    return jnp.dot(a, b, preferred_element_type=F32)


def _dot_nt(a, b):
    return lax.dot_general(a, b, (((1,), (1,)), ((), ())), preferred_element_type=F32)


def _dot_tn(a, b):
    return lax.dot_general(a, b, (((0,), (0,)), ((), ())), preferred_element_type=F32)


def _rmsnorm_kernel(x_ref, g_ref, o_ref):
    x = x_ref[...]
    ms = jnp.mean(x * x, axis=-1, keepdims=True)
    o_ref[...] = (x * lax.rsqrt(ms + EPS) * g_ref[...]).astype(o_ref.dtype)


def _rmsnorm(x, gain, layer):
    m, d = x.shape
    tr = _pick_tile(m, 512, 16)
    return pl.pallas_call(
        _rmsnorm_kernel,
        grid=(m // tr,),
        in_specs=[pl.BlockSpec((tr, d), lambda i: (i, 0)),
                  pl.BlockSpec((None, 1, d), lambda i: (layer, 0, 0))],
        out_specs=pl.BlockSpec((tr, d), lambda i: (i, 0)),
        out_shape=jax.ShapeDtypeStruct((m, d), MXU_DTYPE),
        compiler_params=_params("arbitrary"),
        name="rmsnorm",
    )(x, gain)


def _mm_tiles(m, n):
    return _pick_tile(m, 1100, 16), _pick_tile(n, 512, LANE)


def _lhs_spec(m, k, big):
    if big:
        tm = _pick_tile(m, 2200, 16)
        return tm, pl.BlockSpec((tm, k), lambda i, j: (i, 0), pipeline_mode=pl.Buffered(1))
    tm = _pick_tile(m, 1100, 16)
    return tm, pl.BlockSpec((tm, k), lambda i, j: (i, 0))


def _w_spec(k, tn, layer, col_blk0):
    return pl.BlockSpec((None, k, tn), lambda i, j: (layer, 0, col_blk0 + j))


def _mm_kernel(h_ref, w_ref, o_ref, *, w_rows):
    w = (w_ref[0] if w_rows else w_ref[...]).astype(MXU_DTYPE)
    acc = _dot_nt(h_ref[...], w) if w_rows else _dot(h_ref[...], w)
    o_ref[...] = acc.astype(o_ref.dtype)


def _w_rows_spec(k, tn, layer, row0, col_map=None):
    col_map = col_map or (lambda j: j)
    return pl.BlockSpec((pl.Element(1), pl.Element(tn), pl.Element(k)),
                        lambda i, j: (layer, pl.multiple_of(row0 + col_map(j) * tn, 8), 0))


def _matmul(h, w, layer, col0, n, out_dtype, tn=None, col_map=None, w_rows=False, big=False, name="matmul"):
    m, k = h.shape
    tm, h_spec = _lhs_spec(m, k, big)
    tn = tn or _mm_tiles(m, n)[1]
    assert n % tn == 0
    if w_rows:
        w_spec = _w_rows_spec(k, tn, layer, col0, col_map)
    else:
        assert col0 % tn == 0
        col_map = col_map or (lambda j: j)
        w_spec = pl.BlockSpec((None, k, tn), lambda i, j: (layer, 0, col0 // tn + col_map(j)))
    return pl.pallas_call(
        functools.partial(_mm_kernel, w_rows=w_rows),
        grid=(m // tm, n // tn),
        in_specs=[h_spec, w_spec],
        out_specs=pl.BlockSpec((tm, tn), lambda i, j: (i, j)),
        out_shape=jax.ShapeDtypeStruct((m, n), out_dtype),
        compiler_params=_params("arbitrary", "arbitrary"),
        name=name,
    )(h, w)


def _mm_res_kernel(h_ref, w_ref, r_ref, o_ref):
    o_ref[...] = r_ref[...] + _dot(h_ref[...], w_ref[...].astype(MXU_DTYPE))


def _matmul_residual(h, w, layer, res):
    m, k = h.shape
    n = w.shape[-1]
    tm, tn = _mm_tiles(m, n)
    return pl.pallas_call(
        _mm_res_kernel,
        grid=(m // tm, n // tn),
        in_specs=[pl.BlockSpec((tm, k), lambda i, j: (i, 0)), _w_spec(k, tn, layer, 0),
                  pl.BlockSpec((tm, tn), lambda i, j: (i, j))],
        out_specs=pl.BlockSpec((tm, tn), lambda i, j: (i, j)),
        out_shape=jax.ShapeDtypeStruct((m, n), F32),
        compiler_params=_params("arbitrary", "arbitrary"),
        name="matmul_residual",
    )(h, w, res)


def _swiglu_kernel(h_ref, wg_ref, wu_ref, o_ref):
    h = h_ref[...]
    g = _dot(h, wg_ref[...].astype(MXU_DTYPE))
    u = _dot(h, wu_ref[...].astype(MXU_DTYPE))
    o_ref[...] = (g * _sigmoid(g) * u).astype(o_ref.dtype)


def _swiglu_in(h, w, layer):
    m, k = h.shape
    f = w.shape[-1] // 2
    tm, h_spec = _lhs_spec(m, k, True)
    tn = _pick_tile(f, 256, LANE)
    nf = f // tn
    return pl.pallas_call(
        _swiglu_kernel,
        grid=(m // tm, nf),
        in_specs=[h_spec, _w_spec(k, tn, layer, 0), _w_spec(k, tn, layer, nf)],
        out_specs=pl.BlockSpec((tm, tn), lambda i, j: (i, j)),
        out_shape=jax.ShapeDtypeStruct((m, f), MXU_DTYPE),
        compiler_params=_params("arbitrary", "arbitrary"),
        name="swiglu_in",
    )(h, w, w)


def _ffn_out_kernel(a_ref, w_ref, r_ref, o_ref, *, nk, k_valid_last, scale):
    kk = pl.program_id(2)
    tk = a_ref.shape[1]

    @pl.when(kk == 0)
    def _():
        o_ref[...] = jnp.zeros_like(o_ref)

    def accumulate(masked):
        a = a_ref[...]
        w = w_ref[...]
        if masked:
            a = jnp.where(lax.broadcasted_iota(jnp.int32, a.shape, 1) < k_valid_last, a, 0)
            w = jnp.where(lax.broadcasted_iota(jnp.int32, w.shape, 0) < k_valid_last, w, 0)
        o_ref[...] += _dot(a, w.astype(MXU_DTYPE))

    if k_valid_last == tk:
        accumulate(False)
    else:
        pl.when(kk < nk - 1)(functools.partial(accumulate, False))
        pl.when(kk == nk - 1)(functools.partial(accumulate, True))

    @pl.when(kk == nk - 1)
    def _():
        o_ref[...] = r_ref[...] + scale * o_ref[...]


def _ffn_out(a, w, layer, res, scale):
    m, k = a.shape
    n = w.shape[-1]
    tm = _pick_tile(m, 2200, 16)
    tn = _pick_tile(n, 1024, LANE)
    tk = 512
    nk = pl.cdiv(k, tk)
    kern = functools.partial(_ffn_out_kernel, nk=nk, k_valid_last=k - (nk - 1) * tk, scale=scale)
    return pl.pallas_call(
        kern,
        grid=(m // tm, n // tn, nk),
        in_specs=[pl.BlockSpec((tm, tk), lambda i, j, kk: (i, kk)),
                  pl.BlockSpec((None, tk, tn), lambda i, j, kk: (layer, kk, j)),
                  pl.BlockSpec((tm, tn), lambda i, j, kk: (i, j))],
        out_specs=pl.BlockSpec((tm, tn), lambda i, j, kk: (i, j)),
        out_shape=jax.ShapeDtypeStruct((m, n), F32),
        compiler_params=_params("arbitrary", "arbitrary", "arbitrary"),
        name="ffn_out",
    )(a, w, res)


def _merge_kernel(og_ref, oa_ref, wg_ref, wa_ref, zg_ref, za_ref, o_ref):
    bg = _dot(og_ref[...], wg_ref[...].astype(MXU_DTYPE))
    ba = _dot(oa_ref[...], wa_ref[...].astype(MXU_DTYPE))
    o_ref[...] = (_sigmoid(zg_ref[...]) * bg + _sigmoid(za_ref[...]) * ba).astype(o_ref.dtype)


def _merge(og, oa, w_gla_out, w_att_out, layer, y, zg_off, za_off):
    m, kg = og.shape
    ka = oa.shape[1]
    n = w_gla_out.shape[-1]
    tm, _ = _mm_tiles(m, n)
    tn = _pick_tile(n, 256, LANE)
    assert zg_off % tn == 0 and za_off % tn == 0
    return pl.pallas_call(
        _merge_kernel,
        grid=(m // tm, n // tn),
        in_specs=[pl.BlockSpec((tm, kg), lambda i, j: (i, 0)), pl.BlockSpec((tm, ka), lambda i, j: (i, 0)),
                  _w_spec(kg, tn, layer, 0), _w_spec(ka, tn, layer, 0),
                  pl.BlockSpec((tm, tn), lambda i, j: (i, zg_off // tn + j)),
                  pl.BlockSpec((tm, tn), lambda i, j: (i, za_off // tn + j))],
        out_specs=pl.BlockSpec((tm, tn), lambda i, j: (i, j)),
        out_shape=jax.ShapeDtypeStruct((m, n), MXU_DTYPE),
        compiler_params=_params("arbitrary", "arbitrary"),
        name="merge",
    )(og, oa, w_gla_out, w_att_out, y, y)


def _ple_kernel(h_ref, pe_ref, wg_ref, wp_ref, r_ref, o_ref, *tail_ref):
    gate = _dot(h_ref[...], wg_ref[...].astype(MXU_DTYPE))
    proj = _dot(pe_ref[...].astype(MXU_DTYPE), wp_ref[...].astype(MXU_DTYPE))
    out = r_ref[...] + _sigmoid(gate) * proj
    o_ref[...] = out
    if tail_ref:
        tail_ref[0][...] = out[out.shape[0] - DEC_ROWS:]


def _ple(h, pe, w_gate, w_proj, layer, res, m_real=None):
    m, k = h.shape
    kp = pe.shape[1]
    n = w_gate.shape[-1]
    tm, tn = _mm_tiles(m, n)
    split = m_real is not None
    assert not split or (m - m_real == DEC_ROWS and tm >= DEC_ROWS)
    main = pl.BlockSpec((tm, tn), lambda i, j: (i, j))
    return pl.pallas_call(
        _ple_kernel,
        grid=(m // tm, n // tn),
        in_specs=[pl.BlockSpec((tm, k), lambda i, j: (i, 0)), pl.BlockSpec((tm, kp), lambda i, j: (i, 0)),
                  _w_spec(k, tn, layer, 0), _w_spec(kp, tn, layer, 0),
                  pl.BlockSpec((tm, tn), lambda i, j: (i, j))],
        out_specs=[main, pl.BlockSpec((DEC_ROWS, tn), lambda i, j: (i, j))] if split else main,
        out_shape=[jax.ShapeDtypeStruct((m_real, n), F32), jax.ShapeDtypeStruct((m // tm * DEC_ROWS, n), F32)] if split
        else jax.ShapeDtypeStruct((m, n), F32),
        compiler_params=_params("arbitrary", "arbitrary"),
        name="ple",
    )(h, pe, w_gate, w_proj, res)


def _split_cumsum(tri, x):
    hi = x.astype(MXU_DTYPE)
    r1 = x - hi.astype(F32)
    mid = r1.astype(MXU_DTYPE)
    lo = (r1 - mid.astype(F32)).astype(MXU_DTYPE)
    return _dot(tri, hi) + _dot(tri, mid) + _dot(tri, lo)


def _gla_out_gate(o, gn, r):
    ms = jnp.mean(o * o, axis=-1, keepdims=True)
    return o * lax.rsqrt(ms + EPS) * gn * (r * _sigmoid(r))


def _gla_kernel(q_ref, k_ref, v_ref, r_ref, lr_ref, wd_ref, bd_ref, gn_ref, *rest, nc, q_scale, aliased):
    o_ref, s_ref, st_ref = rest[1:] if aliased else rest
    c = pl.program_id(2)

    @pl.when(c == 0)
    def _():
        st_ref[...] = jnp.zeros_like(st_ref)

    cs, dk = q_ref.shape
    q = q_ref[...] * q_scale
    k = k_ref[...]
    v = v_ref[...].astype(MXU_DTYPE)
    z = _dot(lr_ref[...].astype(MXU_DTYPE), wd_ref[...].astype(MXU_DTYPE)) + bd_ref[...]
    log_a = _log_sigmoid(z) / GLA_TAU
    row = lax.broadcasted_iota(jnp.int32, (cs, cs), 0)
    col = lax.broadcasted_iota(jnp.int32, (cs, cs), 1)
    b = _split_cumsum(jnp.where(row >= col, 1.0, 0.0).astype(MXU_DTYPE), log_a)

    st = st_ref[...]
    o = _dot_nt((q * jnp.exp(b)).astype(MXU_DTYPE), st.astype(MXU_DTYPE))

    sup = min(GLA_SUPER, cs)
    n_sup = cs // sup
    rowv = lax.broadcasted_iota(jnp.int32, (cs, 1), 0)
    srow = lax.broadcasted_iota(jnp.int32, (sup, sup), 0)
    scol = lax.broadcasted_iota(jnp.int32, (sup, sup), 1)
    diag = jnp.sum(q * k, axis=-1, keepdims=True)
    a_sup = [jnp.where(srow == scol, diag[i * sup:(i + 1) * sup], 0.0) for i in range(n_sup)]
    a = jnp.zeros((cs, cs), F32)
    last = b
    half = 1
    while half < cs:
        blk = 2 * half
        shift = blk.bit_length() - 1
        bottom = (rowv & (blk - 1)) >= half
        rho = jnp.where(bottom, pltpu.roll(last, half, 0), last)
        f = jnp.exp(jnp.where(bottom, b - rho, rho - b))
        qs = jnp.where(bottom, q * f, 0.0).astype(MXU_DTYPE)
        ks = jnp.where(bottom, 0.0, k * f).astype(MXU_DTYPE)
        if blk <= sup:
            same_block = (srow >> shift) == (scol >> shift)
            a_sup = [a_i + jnp.where(same_block, _dot_nt(qs[i * sup:(i + 1) * sup], ks[i * sup:(i + 1) * sup]), 0.0)
                     for i, a_i in enumerate(a_sup)]
        else:
            a = a + jnp.where((row >> shift) == (col >> shift), _dot_nt(qs, ks), 0.0)
        last = jnp.where(bottom, last, pltpu.roll(last, cs - half, 0))
        half = blk
    zero = jnp.zeros((sup, sup), F32)
    a = a + jnp.concatenate([jnp.concatenate([a_sup[i] if j == i else zero for j in range(n_sup)], axis=1)
                             for i in range(n_sup)], axis=0)
    o = o + _dot(a.astype(MXU_DTYPE), v)

    b_last = b[cs - 1:cs]
    kd = (k * jnp.exp(b_last - b)).astype(MXU_DTYPE)
    st_new = st * jnp.exp(b_last) + _dot_tn(v, kd)
    st_ref[...] = st_new

    o_ref[...] = _gla_out_gate(o, gn_ref[...], r_ref[...]).astype(o_ref.dtype)

    @pl.when(c == nc - 1)
    def _():
        s_ref[...] = st_new.T


def _gla_prompt(qkv, y, lr, wd, bd, gn, s_all, layer, depth, bsz, seq, dk, dv, m_pad):
    cs = min(GLA_CHUNK, seq)
    assert seq % cs == 0 and cs & (cs - 1) == 0 and cs % 8 == 0
    nc = seq // cs
    h_ = GLA_HEADS
    aliased = s_all is not None
    kern = functools.partial(_gla_kernel, nc=nc, q_scale=dk ** -0.5, aliased=aliased)
    row = lambda b, h, c: b * nc + c
    in_specs = [pl.BlockSpec((cs, dk), lambda b, h, c: (row(b, h, c), h)),
                pl.BlockSpec((cs, dk), lambda b, h, c: (row(b, h, c), h_ + h)),
                pl.BlockSpec((cs, dv), lambda b, h, c: (row(b, h, c), 2 * h_ * dk // dv + h)),
                pl.BlockSpec((cs, dv), lambda b, h, c: (row(b, h, c), h)),
                pl.BlockSpec((cs, LANE), lambda b, h, c: (row(b, h, c), 0)),
                pl.BlockSpec((None, LANE, dk), lambda b, h, c: (layer, 0, h)),
                pl.BlockSpec((None, 1, dk), lambda b, h, c: (layer, 0, h)),
                pl.BlockSpec((None, 1, dv), lambda b, h, c: (layer, 0, 0))]
    args = [qkv, qkv, qkv, y, lr, wd, bd, gn]
    if aliased:
        in_specs.append(pl.BlockSpec(memory_space=pl.ANY))
        args.append(s_all)
    return pl.pallas_call(
        kern,
        grid=(bsz, h_, nc),
        in_specs=in_specs,
        out_specs=[pl.BlockSpec((cs, dv), lambda b, h, c: (row(b, h, c), h)),
                   pl.BlockSpec((None, None, None, dk, dv), lambda b, h, c: (layer, b, h, 0, 0))],
        out_shape=[jax.ShapeDtypeStruct((m_pad, h_ * dv), MXU_DTYPE),
                   jax.ShapeDtypeStruct((depth, bsz, h_, dk, dv), F32)],
        scratch_shapes=[pltpu.VMEM((dv, dk), F32)],
        input_output_aliases={8: 1} if aliased else {},
        compiler_params=_params("arbitrary", "arbitrary", "arbitrary"),
        name="gla_prompt",
    )(*args)


def _gla_decode_kernel(q_ref, k_ref, v_ref, r_ref, lr_ref, wd_ref, bd_ref, gn_ref, s0_ref, *rest, nb, q_scale):
    o_ref, s_ref, acc_ref = rest[-3:]
    bi = pl.program_id(1)

    @pl.when(bi == 0)
    def _():
        acc_ref[...] = jnp.zeros_like(acc_ref)

    rows = q_ref.shape[0]
    rid = lax.broadcasted_iota(jnp.int32, (rows, 1), 0)

    def pick(x):
        return jnp.sum(jnp.where(rid == bi, x, 0.0), axis=0, keepdims=True)

    def column(x_row, n):
        eye = lax.broadcasted_iota(jnp.int32, (n, n), 0) == lax.broadcasted_iota(jnp.int32, (n, n), 1)
        return jnp.sum(jnp.where(eye, x_row, 0.0), axis=1, keepdims=True)

    dk = q_ref.shape[1]
    q = pick(q_ref[...]) * q_scale
    k = pick(k_ref[...])
    v = pick(v_ref[...])
    z = _dot(lr_ref[...].astype(MXU_DTYPE), wd_ref[...].astype(MXU_DTYPE)) + bd_ref[...]
    decay = jnp.exp(pick(_log_sigmoid(z) / GLA_TAU))
    s0 = s0_ref[...]
    qd = jnp.broadcast_to(q * decay, (rows, dk)).astype(MXU_DTYPE)
    o = _dot(qd, s0.astype(MXU_DTYPE))[0:1]
    o = o + jnp.sum(q * k, axis=-1, keepdims=True) * v
    s_ref[...] = s0 * column(decay, dk) + column(k, dk) * v

    gated = _gla_out_gate(o, gn_ref[...], pick(r_ref[...]))
    acc = jnp.where(rid == bi, gated, acc_ref[...])
    acc_ref[...] = acc

    @pl.when(bi == nb - 1)
    def _():
        pad = jnp.zeros((o_ref.shape[0] - rows, o_ref.shape[1]), F32)
        o_ref[...] = jnp.concatenate([acc, pad], axis=0).astype(o_ref.dtype)


def _gla_decode(qkv, y, lr, wd, bd, gn, state, og, s_all, layer, nb, dk, dv, m_real):
    h_ = GLA_HEADS
    r8 = m_real // 8
    rdec = m_real // DEC_ROWS
    kern = functools.partial(_gla_decode_kernel, nb=nb, q_scale=dk ** -0.5)
    any_spec = pl.BlockSpec(memory_space=pl.ANY)
    extra = [] if s_all is None else [s_all]
    og_new, s_new = pl.pallas_call(
        kern,
        grid=(h_, nb),
        in_specs=[pl.BlockSpec((8, dk), lambda h, b: (r8, h)),
                  pl.BlockSpec((8, dk), lambda h, b: (r8, h_ + h)),
                  pl.BlockSpec((8, dv), lambda h, b: (r8, 2 * h_ * dk // dv + h)),
                  pl.BlockSpec((8, dv), lambda h, b: (r8, h)),
                  pl.BlockSpec((8, LANE), lambda h, b: (r8, 0)),
                  pl.BlockSpec((None, LANE, dk), lambda h, b: (layer, 0, h)),
                  pl.BlockSpec((None, 1, dk), lambda h, b: (layer, 0, h)),
                  pl.BlockSpec((None, 1, dv), lambda h, b: (layer, 0, 0)),
                  pl.BlockSpec((None, None, None, dk, dv), lambda h, b: (layer, b, h, 0, 0)),
                  any_spec] + [any_spec] * len(extra),
        out_specs=[pl.BlockSpec((DEC_ROWS, dv), lambda h, b: (rdec, h)),
                   pl.BlockSpec((None, None, None, dk, dv), lambda h, b: (layer, b, h, 0, 0))],
        out_shape=[jax.ShapeDtypeStruct(og.shape, og.dtype),
                   jax.ShapeDtypeStruct(state.shape, F32)],
        scratch_shapes=[pltpu.VMEM((8, dv), F32)],
        input_output_aliases={9: 0, 10: 1} if extra else {9: 0},
        compiler_params=_params("arbitrary", "arbitrary"),
        name="gla_decode",
    )(qkv, qkv, qkv, y, lr, wd, bd, gn, state, og, *extra)
    return og_new, s_new


def _att_proj_kernel(h_ref, w_ref, g_ref, o_ref, *, n_norm):
    j = pl.program_id(1)
    acc = _dot_nt(h_ref[...], w_ref[0].astype(MXU_DTYPE))
    nh, _, hd = o_ref.shape

    @pl.when(j < n_norm)
    def _():
        for hh in range(nh):
            x = acc[:, hh * hd:(hh + 1) * hd]
            ms = jnp.mean(x * x, axis=-1, keepdims=True)
            o_ref[hh] = x * lax.rsqrt(ms + EPS) * g_ref[:, hh * hd:(hh + 1) * hd]

    @pl.when(j >= n_norm)
    def _():
        for hh in range(nh):
            o_ref[hh] = acc[:, hh * hd:(hh + 1) * hd]


def _att_proj(h, w, layer, row0, gains):
    m, k = h.shape
    n = 3 * ATT_W
    tm, h_spec = _lhs_spec(m, k, True)
    tn = _mm_tiles(m, n)[1]
    assert (2 * ATT_W) % tn == 0 and tn % ATT_HEAD_DIM == 0
    nh = tn // ATT_HEAD_DIM
    n_norm = 2 * ATT_W // tn
    return pl.pallas_call(
        functools.partial(_att_proj_kernel, n_norm=n_norm),
        grid=(m // tm, n // tn),
        in_specs=[h_spec, _w_rows_spec(k, tn, layer, row0),
                  pl.BlockSpec((None, 1, tn), lambda i, j: (layer, 0, jnp.minimum(j, n_norm - 1)))],
        out_specs=pl.BlockSpec((nh, tm, ATT_HEAD_DIM), lambda i, j: (j, i, 0)),
        out_shape=jax.ShapeDtypeStruct((n // ATT_HEAD_DIM, m, ATT_HEAD_DIM), F32),
        compiler_params=_params("arbitrary", "arbitrary"),
        name="in_proj_att",
    )(h, w, gains)


def _softmax_mix(os, ls):
    m = functools.reduce(jnp.maximum, ls)
    ws = [jnp.exp(l - m) for l in ls]
    den = functools.reduce(lambda a, b: a + b, ws)
    return functools.reduce(lambda a, b: a + b, [(w / den) * o for w, o in zip(ws, os)])


def _attn_kernel(slope_ref, q0, q1, q2, k0, k1, k2, v0, v1, v2, oa_ref, c0, c1, c2, o_s, l_s):
    j = pl.program_id(1)
    seq, hd = q0.shape
    tq = ATT_BLOCK
    row = lax.broadcasted_iota(jnp.int32, (tq, 2 * tq), 0)
    col = lax.broadcasted_iota(jnp.int32, (tq, 2 * tq), 1)
    steps = tq + row - col
    in_window = (steps >= 0) & (steps <= ATT_BLOCK)
    for g, (_, dil) in enumerate(ATT_GROUPS):
        q_ref, k_ref, v_ref = (q0, q1, q2)[g], (k0, k1, k2)[g], (v0, v1, v2)[g]
        nq = seq // (tq * dil)
        bias = slope_ref[g, j] * (steps * dil).astype(F32)

        def block(it, carry, q_ref=q_ref, k_ref=k_ref, v_ref=v_ref, g=g, dil=dil, nq=nq, bias=bias):
            r = it // nq
            i = it - r * nq
            start = r + i * (tq * dil)
            pstart = jnp.where(i > 0, start - tq * dil, start)

            def rows(st):
                return pl.ds(st, tq, stride=dil) if dil > 1 else pl.ds(pl.multiple_of(st, tq), tq)

            qh = q_ref[rows(start), :].astype(MXU_DTYPE)
            kk = jnp.concatenate([k_ref[rows(pstart), :], k_ref[rows(start), :]], axis=0).astype(MXU_DTYPE)
            vv = jnp.concatenate([v_ref[rows(pstart), :], v_ref[rows(start), :]], axis=0).astype(MXU_DTYPE)
            s = _dot_nt(qh, kk) * (hd ** -0.5)
            valid = in_window & ((col >= tq) | (i > 0))
            s = jnp.where(valid, s - bias, NEG)
            m = jnp.max(s, axis=-1, keepdims=True)
            p = jnp.exp(s - m)
            l = jnp.sum(p, axis=-1, keepdims=True)
            o_s[g, rows(start), :] = _dot(p.astype(MXU_DTYPE), vv) / l
            l_s[g, rows(start), :] = jnp.broadcast_to(m + jnp.log(l), (tq, hd))
            return carry

        lax.fori_loop(0, seq // tq, block, 0, unroll=4)
        c_ref = (c0, c1, c2)[g]
        win = c_ref.shape[1]
        c_ref[0] = k_ref[seq - win:, :]
        c_ref[1] = v_ref[seq - win:, :]
    n_g = len(ATT_GROUPS)
    oa_ref[...] = _softmax_mix([o_s[g] for g in range(n_g)], [l_s[g] for g in range(n_g)]).astype(oa_ref.dtype)


def _attn_prompt(att, slopes, bsz, seq, m_pad):
    for win, dil in ATT_GROUPS:
        assert win // dil == ATT_BLOCK and seq % (dil * ATT_BLOCK) == 0 and win <= seq
    hd = ATT_HEAD_DIM
    n_g = len(ATT_GROUPS)
    specs = [pl.BlockSpec((None, seq, hd), lambda b, j, h0=kind * ATT_HEADS + g * ATT_GROUP_HEADS: (h0 + j, b, 0))
             for kind in range(3) for g in range(n_g)]
    return pl.pallas_call(
        _attn_kernel,
        grid=(bsz, ATT_GROUP_HEADS),
        in_specs=[pl.BlockSpec(memory_space=pltpu.SMEM)] + specs,
        out_specs=[pl.BlockSpec((seq, hd), lambda b, j: (b, j))]
        + [pl.BlockSpec((2, None, None, win, hd), lambda b, j: (0, j, b, 0, 0)) for win, _ in ATT_GROUPS],
        out_shape=[jax.ShapeDtypeStruct((m_pad, ATT_GW), MXU_DTYPE)]
        + [jax.ShapeDtypeStruct((2, ATT_GROUP_HEADS, bsz, win, hd), F32) for win, _ in ATT_GROUPS],
        scratch_shapes=[pltpu.VMEM((n_g, seq, hd), F32), pltpu.VMEM((n_g, seq, hd), F32)],
        compiler_params=_params("arbitrary", "arbitrary"),
        name="attn_prompt",
    )(slopes, *([att] * 9))


def _attn_decode_kernel(slope_ref, a_ref, c0_ref, c1_ref, c2_ref, o_ref):
    nh = ATT_GROUP_HEADS
    hd = ATT_HEAD_DIM
    nrow = lax.broadcasted_iota(jnp.int32, (ATT_BLOCK, 1, 1), 0)
    os, ls = [], []
    for g, (_, dil) in enumerate(ATT_GROUPS):
        c_ref = (c0_ref, c1_ref, c2_ref)[g]
        q = a_ref[g * nh:(g + 1) * nh]
        kn = a_ref[ATT_HEADS + g * nh:ATT_HEADS + (g + 1) * nh]
        vn = a_ref[2 * ATT_HEADS + g * nh:2 * ATT_HEADS + (g + 1) * nh]
        kb = c_ref[:, 0]
        vb = c_ref[:, 1]
        slope = slope_ref[g][:, 0:1]
        dist = ((ATT_BLOCK - nrow) * dil).astype(F32)
        sb = jnp.sum(kb * q, axis=-1, keepdims=True) * (hd ** -0.5) - slope * dist
        sn = jnp.sum(kn * q, axis=-1, keepdims=True) * (hd ** -0.5)
        m = jnp.maximum(jnp.max(sb, axis=0), sn)
        pb = jnp.exp(sb - m)
        pn = jnp.exp(sn - m)
        l = jnp.sum(pb, axis=0) + pn
        os.append((jnp.sum(pb * vb, axis=0) + pn * vn) / l)
        ls.append(m + jnp.log(l))
    o_ref[...] = _softmax_mix(os, ls)


def _attn_decode(a_dec, slopes_v, caches, layer):
    nb = a_dec.shape[0]
    views, specs = [], []
    for (win, dil), cbuf in zip(ATT_GROUPS, caches):
        depth, nb_, wb = cbuf.shape[:3]
        assert wb == win and nb_ == nb
        views.append(cbuf.reshape(depth, nb, win // dil, dil, 2, ATT_GROUP_HEADS, ATT_HEAD_DIM))
        specs.append(pl.BlockSpec((None, None, win // dil, None, 2, ATT_GROUP_HEADS, ATT_HEAD_DIM),
                                  lambda b: (layer, b, 0, 0, 0, 0, 0)))
    return pl.pallas_call(
        _attn_decode_kernel,
        grid=(nb,),
        in_specs=[pl.BlockSpec(slopes_v.shape, lambda b: (0, 0, 0)),
                  pl.BlockSpec((None,) + a_dec.shape[1:], lambda b: (b, 0, 0))] + specs,
        out_specs=pl.BlockSpec((None, ATT_GROUP_HEADS, ATT_HEAD_DIM), lambda b: (b, 0, 0)),
        out_shape=jax.ShapeDtypeStruct((nb, ATT_GROUP_HEADS, ATT_HEAD_DIM), F32),
        compiler_params=_params("arbitrary"),
        name="attn_decode",
    )(slopes_v, a_dec, *views)


def _place_rows_kernel(x_ref, dst_ref, o_ref):
    del dst_ref
    pad = jnp.zeros((o_ref.shape[0] - x_ref.shape[0], o_ref.shape[1]), F32)
    o_ref[...] = jnp.concatenate([x_ref[...], pad], axis=0).astype(o_ref.dtype)


def _place_sample_rows(x, dst, m_real):
    n = dst.shape[1]
    return pl.pallas_call(
        _place_rows_kernel,
        grid=(1,),
        in_specs=[pl.BlockSpec(x.shape, lambda i: (0, 0)), pl.BlockSpec(memory_space=pl.ANY)],
        out_specs=pl.BlockSpec((DEC_ROWS, n), lambda i: (m_real // DEC_ROWS, 0)),
        out_shape=jax.ShapeDtypeStruct(dst.shape, dst.dtype),
        input_output_aliases={1: 0},
        compiler_params=_params("arbitrary"),
        name="place_sample_rows",
    )(x, dst)


def kernel(x_prompt, x_sample, state_gla, cache_w128, cache_w512, cache_w2048, p_prompt, p_sample, norm_ffn1, ffn1_w_in, ffn1_w_out, norm_mix, w_in, gla_w_decay, gla_b_decay, gla_norm, att_q_norm, att_k_norm, w_gla_out, w_att_out, w_out, norm_ffn2, ffn2_w_in, ffn2_w_out, norm_ple, w_ple_gate, w_ple_proj):
    bsz, seq, d = x_prompt.shape
    nb, dec_seq, _ = x_sample.shape
    depth = norm_ffn1.shape[0]
    assert dec_seq == 1 and nb == 8
    m_real = bsz * seq
    assert m_real % DEC_ROWS == 0
    m_pad = m_real + DEC_ROWS
    gla_qk = gla_w_decay.shape[-1]
    dk = gla_qk // GLA_HEADS
    gla_v = d
    dv = gla_v // GLA_HEADS
    caches = (cache_w128, cache_w512, cache_w2048)
    n_g = len(ATT_GROUPS)
    nh, hd = ATT_GROUP_HEADS, ATT_HEAD_DIM

    lr_off = 2 * gla_qk + gla_v
    r_off = lr_off + GLA_RANK
    q_off = r_off + gla_v
    z_off = q_off + 3 * ATT_W
    assert w_in.shape[-1] == z_off + 2 * d and lr_off % LANE == 0 and r_off % 8 == 0
    w_in_t = jnp.swapaxes(w_in, 1, 2)

    zrow = lambda n, w: jnp.zeros((n, w), F32)
    x = jnp.concatenate([x_prompt.reshape(m_real, d), x_sample.reshape(nb, d), zrow(DEC_ROWS - nb, d)], axis=0)
    pdim = p_prompt.shape[-1]
    pe = jnp.concatenate([p_prompt.reshape(depth, m_real, pdim), p_sample.reshape(depth, nb, pdim),
                          jnp.zeros((depth, DEC_ROWS - nb, pdim), F32)], axis=1)

    wd = jnp.zeros((depth, LANE, gla_qk), F32).at[:, :GLA_RANK].set(gla_w_decay)
    bd = gla_b_decay.reshape(depth, 1, gla_qk)
    gn = gla_norm.reshape(depth, 1, dv)
    att_gain = jnp.concatenate([jnp.tile(att_q_norm, (1, ATT_HEADS)), jnp.tile(att_k_norm, (1, ATT_HEADS))],
                               axis=1).reshape(depth, 1, 2 * ATT_W)
    slopes = jnp.exp2(-8.0 * jnp.arange(1, ATT_HEADS + 1, dtype=F32) / ATT_HEADS)
    slopes_s = slopes.reshape(n_g, nh)
    slopes_v = jnp.broadcast_to(slopes.reshape(n_g, nh, 1), (n_g, nh, LANE))
    g3 = lambda a: a.reshape(depth, 1, d)
    n_f1, n_mix, n_f2, n_ple = g3(norm_ffn1), g3(norm_mix), g3(norm_ffn2), g3(norm_ple)

    _, tn_g = _mm_tiles(m_pad, gla_v + 2 * d)
    n_r, n_att = gla_v // tn_g, 3 * ATT_W // tn_g
    gate_cols = lambda j: jnp.where(j < n_r, j, j + n_att)

    gla_p, gla_s, win_p, new_rows = None, None, [], []
    for i in range(depth):
        x = _ffn_out(_swiglu_in(_rmsnorm(x, n_f1, i), ffn1_w_in, i), ffn1_w_out, i, x, 0.5)

        h = _rmsnorm(x, n_mix, i)
        qkv = _matmul(h, w_in_t, i, 0, lr_off, F32, w_rows=True, big=True, name="in_proj_gla")
        lr = _matmul(h, w_in_t, i, lr_off, LANE, F32, tn=LANE, w_rows=True, name="in_proj_decay")
        gates = _matmul(h, w_in_t, i, r_off, gla_v + 2 * d, F32, col_map=gate_cols, w_rows=True, big=True,
                        name="in_proj_gates")
        att = _att_proj(h, w_in_t, i, q_off, att_gain)

        og, gla_p = _gla_prompt(qkv, gates, lr, wd, bd, gn, gla_p, i, depth, bsz, seq, dk, dv, m_pad)
        og, gla_s = _gla_decode(qkv, gates, lr, wd, bd, gn, state_gla, og, gla_s, i, nb, dk, dv, m_real)

        oa, *bufs = _attn_prompt(att, slopes_s, bsz, seq, m_pad)
        a_dec = jnp.transpose(att[:, m_real:m_real + nb], (1, 0, 2))
        o_dec = _attn_decode(a_dec, slopes_v, caches, i)
        oa = _place_sample_rows(o_dec.reshape(nb, ATT_GW), oa, m_real)

        merged = _merge(og, oa, w_gla_out, w_att_out, i, gates, gla_v, gla_v + d)
        x = _matmul_residual(merged, w_out, i, x)
        x = _ffn_out(_swiglu_in(_rmsnorm(x, n_f2, i), ffn2_w_in, i), ffn2_w_out, i, x, 0.5)
        x = _ple(_rmsnorm(x, n_ple, i), pe[i], w_ple_gate, w_ple_proj, i, x, m_real if i == depth - 1 else None)

        win_p.append(bufs)
        new_rows.append(a_dec.reshape(nb, 3, n_g, nh, hd)[:, 1:])

    y_prompt = x[0].reshape(bsz, seq, d)
    y_sample = x[1][x[1].shape[0] - DEC_ROWS:][:nb].reshape(nb, 1, d)
    new_rows = jnp.stack(new_rows)
    win_s = [jnp.concatenate([caches[g][:, :, 1:], new_rows[:, :, None, :, g]], axis=2) for g in range(n_g)]
    win_p = [jnp.transpose(jnp.stack([b[g] for b in win_p]), (0, 3, 4, 1, 2, 5)) for g in range(n_g)]
    return (y_prompt, y_sample, gla_p, win_p[0], win_p[1], win_p[2], gla_s, win_s[0], win_s[1], win_s[2])
```

```python
import functools

import jax
import jax.numpy as jnp
from jax import lax
from jax.experimental import pallas as pl
from jax.experimental.pallas import tpu as pltpu

F32 = jnp.float32
MXU_DTYPE = jnp.bfloat16

EPS = 1e-6
GLA_HEADS = 4
GLA_RANK = 16
GLA_TAU = 16.0
GLA_CHUNK = 256
GLA_SUPER = 128
ATT_GROUPS = ((128, 1), (512, 4), (2048, 16))
ATT_GROUP_HEADS = 8
ATT_HEAD_DIM = 128
ATT_HEADS = len(ATT_GROUPS) * ATT_GROUP_HEADS
ATT_GW = ATT_GROUP_HEADS * ATT_HEAD_DIM
ATT_W = ATT_HEADS * ATT_HEAD_DIM
ATT_BLOCK = 128
DEC_ROWS = 128
LANE = 128
NEG = -1e30

VMEM_LIMIT = 56 * 1024 * 1024


VMEM_LIMIT_MAX = 61 * 1024 * 1024


def _params(*sem, vmem=VMEM_LIMIT):
    return pltpu.CompilerParams(dimension_semantics=sem, vmem_limit_bytes=vmem)


def _pick_tile(n, target, mult):
    best = None
    for t in range(mult, min(n, target) + 1, mult):
        if n % t == 0:
            best = t
    assert best is not None, (n, target, mult)
    return best


def _sigmoid(x):
    return 1.0 / (1.0 + jnp.exp(-x))


def _log_sigmoid(x):
    return jnp.minimum(x, 0.0) - jnp.log(1.0 + jnp.exp(-jnp.abs(x)))


def _dot(a, b):
    return jnp.dot(a, b, preferred_element_type=F32)


def _dot_nt(a, b):
    return lax.dot_general(a, b, (((1,), (1,)), ((), ())), preferred_element_type=F32)


def _dot_tn(a, b):
    return lax.dot_general(a, b, (((0,), (0,)), ((), ())), preferred_element_type=F32)


def _row_scale(ss_ref, d):
    return lax.rsqrt(jnp.sum(ss_ref[...], axis=-1, keepdims=True) / d + EPS)


def _emit_norm_operand(out, g_ref, xg_ref, ss_ref, j):
    xg_ref[...] = (out * g_ref[...]).astype(xg_ref.dtype)

    @pl.when(j == 0)
    def _():
        ss_ref[...] = jnp.zeros_like(ss_ref)

    lane = lax.broadcasted_iota(jnp.int32, ss_ref.shape, 1)
    ss_ref[...] = jnp.where(lane == j, jnp.sum(out * out, axis=-1, keepdims=True), ss_ref[...])


def _norm_out(m, n, tm, tn):
    assert n // tn <= LANE
    specs = [pl.BlockSpec((tm, tn), lambda i, j, *_: (i, j)), pl.BlockSpec((tm, LANE), lambda i, j, *_: (i, 0))]
    shapes = [jax.ShapeDtypeStruct((m, n), MXU_DTYPE), jax.ShapeDtypeStruct((m, LANE), F32)]
    return specs, shapes


def _gain_spec(tn, layer):
    return pl.BlockSpec((None, 1, tn), lambda i, j, *_: (layer, 0, j))


def _norm_operand_kernel(x_ref, g_ref, xg_ref, ss_ref):
    x = x_ref[...]
    xg_ref[...] = (x * g_ref[...]).astype(xg_ref.dtype)
    lane = lax.broadcasted_iota(jnp.int32, ss_ref.shape, 1)
    ss_ref[...] = jnp.where(lane == 0, jnp.sum(x * x, axis=-1, keepdims=True), 0.0)


def _norm_operand(x, gain, layer):
    m, d = x.shape
    tr = _pick_tile(m, 512, 16)
    return pl.pallas_call(
        _norm_operand_kernel,
        grid=(m // tr,),
        in_specs=[pl.BlockSpec((tr, d), lambda i: (i, 0)),
                  pl.BlockSpec((None, 1, d), lambda i: (layer, 0, 0))],
        out_specs=[pl.BlockSpec((tr, d), lambda i: (i, 0)), pl.BlockSpec((tr, LANE), lambda i: (i, 0))],
        out_shape=[jax.ShapeDtypeStruct((m, d), MXU_DTYPE), jax.ShapeDtypeStruct((m, LANE), F32)],
        compiler_params=_params("arbitrary"),
        name="norm_operand",
    )(x, gain)


def _mm_tiles(m, n):
    return _pick_tile(m, 1100, 16), _pick_tile(n, 512, LANE)


def _lhs_spec(m, k, big):
    if big:
        tm = _pick_tile(m, 2200, 16)
        return tm, pl.BlockSpec((tm, k), lambda i, j: (i, 0), pipeline_mode=pl.Buffered(1))
    tm = _pick_tile(m, 1100, 16)
    return tm, pl.BlockSpec((tm, k), lambda i, j: (i, 0))


def _w_spec(k, tn, layer, col_blk0):
    return pl.BlockSpec((None, k, tn), lambda i, j: (layer, 0, col_blk0 + j))


def _mm_kernel(h_ref, ss_ref, w_ref, o_ref, *, w_rows):
    w = (w_ref[0] if w_rows else w_ref[...]).astype(MXU_DTYPE)
    acc = _dot_nt(h_ref[...], w) if w_rows else _dot(h_ref[...], w)
    o_ref[...] = (acc * _row_scale(ss_ref, h_ref.shape[1])).astype(o_ref.dtype)


def _w_rows_spec(k, tn, layer, row0, col_map=None):
    col_map = col_map or (lambda j: j)
    return pl.BlockSpec((pl.Element(1), pl.Element(tn), pl.Element(k)),
                        lambda i, j: (layer, pl.multiple_of(row0 + col_map(j) * tn, 8), 0))


def _matmul(hn, w, layer, col0, n, out_dtype, tn=None, col_map=None, w_rows=False, big=False, name="matmul"):
    h, ss = hn
    m, k = h.shape
    tm, h_spec = _lhs_spec(m, k, big)
    tn = tn or _mm_tiles(m, n)[1]
    assert n % tn == 0
    if w_rows:
        w_spec = _w_rows_spec(k, tn, layer, col0, col_map)
    else:
        assert col0 % tn == 0
        col_map = col_map or (lambda j: j)
        w_spec = pl.BlockSpec((None, k, tn), lambda i, j: (layer, 0, col0 // tn + col_map(j)))
    return pl.pallas_call(
        functools.partial(_mm_kernel, w_rows=w_rows),
        grid=(m // tm, n // tn),
        in_specs=[h_spec, pl.BlockSpec((tm, LANE), lambda i, j: (i, 0)), w_spec],
        out_specs=pl.BlockSpec((tm, tn), lambda i, j: (i, j)),
        out_shape=jax.ShapeDtypeStruct((m, n), out_dtype),
        compiler_params=_params("arbitrary", "arbitrary"),
        name=name,
    )(h, ss, w)


def _mm_res_kernel(h_ref, w_ref, r_ref, g_ref, o_ref, xg_ref, ss_ref):
    out = r_ref[...] + _dot(h_ref[...], w_ref[...].astype(MXU_DTYPE))
    o_ref[...] = out
    _emit_norm_operand(out, g_ref, xg_ref, ss_ref, pl.program_id(1))


def _matmul_residual(h, w, layer, res, gain, glayer):
    m, k = h.shape
    n = w.shape[-1]
    tm, tn = _mm_tiles(m, n)
    nspecs, nshapes = _norm_out(m, n, tm, tn)
    x, xg, ss = pl.pallas_call(
        _mm_res_kernel,
        grid=(m // tm, n // tn),
        in_specs=[pl.BlockSpec((tm, k), lambda i, j: (i, 0)), _w_spec(k, tn, layer, 0),
                  pl.BlockSpec((tm, tn), lambda i, j: (i, j)), _gain_spec(tn, glayer)],
        out_specs=[pl.BlockSpec((tm, tn), lambda i, j: (i, j))] + nspecs,
        out_shape=[jax.ShapeDtypeStruct((m, n), F32)] + nshapes,
        compiler_params=_params("arbitrary", "arbitrary"),
        name="matmul_residual",
    )(h, w, res, gain)
    return x, (xg, ss)


def _swiglu_kernel(h_ref, ss_ref, wg_ref, wu_ref, o_ref):
    h = h_ref[...]
    r = _row_scale(ss_ref, h.shape[1])
    g = _dot(h, wg_ref[...].astype(MXU_DTYPE)) * r
    u = _dot(h, wu_ref[...].astype(MXU_DTYPE)) * r
    o_ref[...] = (g * _sigmoid(g) * u).astype(o_ref.dtype)


def _swiglu_in(hn, w, layer):
    h, ss = hn
    m, k = h.shape
    f = w.shape[-1] // 2
    tm, h_spec = _lhs_spec(m, k, True)
    tn = _pick_tile(f, 256, LANE)
    nf = f // tn
    return pl.pallas_call(
        _swiglu_kernel,
        grid=(m // tm, nf),
        in_specs=[h_spec, pl.BlockSpec((tm, LANE), lambda i, j: (i, 0)),
                  _w_spec(k, tn, layer, 0), _w_spec(k, tn, layer, nf)],
        out_specs=pl.BlockSpec((tm, tn), lambda i, j: (i, j)),
        out_shape=jax.ShapeDtypeStruct((m, f), MXU_DTYPE),
        compiler_params=_params("arbitrary", "arbitrary"),
        name="swiglu_in",
    )(h, ss, w, w)


def _ffn_out_kernel(a_ref, w_ref, r_ref, g_ref, o_ref, xg_ref, ss_ref, *, nk, k_valid_last, scale):
    kk = pl.program_id(2)
    tk = a_ref.shape[1]

    @pl.when(kk == 0)
    def _():
        o_ref[...] = jnp.zeros_like(o_ref)

    def accumulate(masked):
        a = a_ref[...]
        w = w_ref[...]
        if masked:
            a = jnp.where(lax.broadcasted_iota(jnp.int32, a.shape, 1) < k_valid_last, a, 0)
            w = jnp.where(lax.broadcasted_iota(jnp.int32, w.shape, 0) < k_valid_last, w, 0)
        o_ref[...] += _dot(a, w.astype(MXU_DTYPE))

    if k_valid_last == tk:
        accumulate(False)
    else:
        pl.when(kk < nk - 1)(functools.partial(accumulate, False))
        pl.when(kk == nk - 1)(functools.partial(accumulate, True))

    @pl.when(kk == nk - 1)
    def _():
        out = r_ref[...] + scale * o_ref[...]
        o_ref[...] = out
        _emit_norm_operand(out, g_ref, xg_ref, ss_ref, pl.program_id(1))


def _ffn_out(a, w, layer, res, scale, gain, glayer):
    m, k = a.shape
    n = w.shape[-1]
    tm = _pick_tile(m, 2200, 16)
    tn = _pick_tile(n, 1024, LANE)
    tk = 512
    nk = pl.cdiv(k, tk)
    kern = functools.partial(_ffn_out_kernel, nk=nk, k_valid_last=k - (nk - 1) * tk, scale=scale)
    nspecs, nshapes = _norm_out(m, n, tm, tn)
    x, xg, ss = pl.pallas_call(
        kern,
        grid=(m // tm, n // tn, nk),
        in_specs=[pl.BlockSpec((tm, tk), lambda i, j, kk: (i, kk)),
                  pl.BlockSpec((None, tk, tn), lambda i, j, kk: (layer, kk, j)),
                  pl.BlockSpec((tm, tn), lambda i, j, kk: (i, j)), _gain_spec(tn, glayer)],
        out_specs=[pl.BlockSpec((tm, tn), lambda i, j, kk: (i, j))] + nspecs,
        out_shape=[jax.ShapeDtypeStruct((m, n), F32)] + nshapes,
        compiler_params=_params("arbitrary", "arbitrary", "arbitrary", vmem=VMEM_LIMIT_MAX),
        name="ffn_out",
    )(a, w, res, gain)
    return x, (xg, ss)


def _merge_kernel(og_ref, oa_ref, wg_ref, wa_ref, zg_ref, za_ref, o_ref):
    bg = _dot(og_ref[...], wg_ref[...].astype(MXU_DTYPE))
    ba = _dot(oa_ref[...], wa_ref[...].astype(MXU_DTYPE))
    o_ref[...] = (_sigmoid(zg_ref[...]) * bg + _sigmoid(za_ref[...]) * ba).astype(o_ref.dtype)


def _merge(og, oa, w_gla_out, w_att_out, layer, y, zg_off, za_off):
    m, kg = og.shape
    ka = oa.shape[1]
    n = w_gla_out.shape[-1]
    tm, _ = _mm_tiles(m, n)
    tn = _pick_tile(n, 256, LANE)
    assert zg_off % tn == 0 and za_off % tn == 0
    return pl.pallas_call(
        _merge_kernel,
        grid=(m // tm, n // tn),
        in_specs=[pl.BlockSpec((tm, kg), lambda i, j: (i, 0)), pl.BlockSpec((tm, ka), lambda i, j: (i, 0)),
                  _w_spec(kg, tn, layer, 0), _w_spec(ka, tn, layer, 0),
                  pl.BlockSpec((tm, tn), lambda i, j: (i, zg_off // tn + j)),
                  pl.BlockSpec((tm, tn), lambda i, j: (i, za_off // tn + j))],
        out_specs=pl.BlockSpec((tm, tn), lambda i, j: (i, j)),
        out_shape=jax.ShapeDtypeStruct((m, n), MXU_DTYPE),
        compiler_params=_params("arbitrary", "arbitrary"),
        name="merge",
    )(og, oa, w_gla_out, w_att_out, y, y)


def _ple_kernel(h_ref, ss_ref, pe_ref, wg_ref, wp_ref, r_ref, *rest, final):
    gate = _dot(h_ref[...], wg_ref[...].astype(MXU_DTYPE)) * _row_scale(ss_ref, h_ref.shape[1])
    proj = _dot(pe_ref[...].astype(MXU_DTYPE), wp_ref[...].astype(MXU_DTYPE))
    out = r_ref[...] + _sigmoid(gate) * proj
    if final:
        o_ref, tail_ref = rest
        o_ref[...] = out
        tail_ref[...] = out[out.shape[0] - DEC_ROWS:]
    else:
        g_ref, o_ref, xg_ref, ss_out_ref = rest
        o_ref[...] = out
        _emit_norm_operand(out, g_ref, xg_ref, ss_out_ref, pl.program_id(1))


def _ple(hn, pe, w_gate, w_proj, layer, res, gain=None, glayer=None, m_real=None):
    h, ss = hn
    m, k = h.shape
    kp = pe.shape[1]
    n = w_gate.shape[-1]
    tm, tn = _mm_tiles(m, n)
    final = m_real is not None
    assert not final or (m - m_real == DEC_ROWS and tm >= DEC_ROWS)
    main = pl.BlockSpec((tm, tn), lambda i, j: (i, j))
    in_specs = [pl.BlockSpec((tm, k), lambda i, j: (i, 0)), pl.BlockSpec((tm, LANE), lambda i, j: (i, 0)),
                pl.BlockSpec((tm, kp), lambda i, j: (i, 0)),
                _w_spec(k, tn, layer, 0), _w_spec(kp, tn, layer, 0),
                pl.BlockSpec((tm, tn), lambda i, j: (i, j))]
    args = [h, ss, pe, w_gate, w_proj, res]
    if final:
        out_specs = [main, pl.BlockSpec((DEC_ROWS, tn), lambda i, j: (i, j))]
        out_shape = [jax.ShapeDtypeStruct((m_real, n), F32), jax.ShapeDtypeStruct((m // tm * DEC_ROWS, n), F32)]
    else:
        nspecs, nshapes = _norm_out(m, n, tm, tn)
        in_specs.append(_gain_spec(tn, glayer))
        args.append(gain)
        out_specs = [main] + nspecs
        out_shape = [jax.ShapeDtypeStruct((m, n), F32)] + nshapes
    outs = pl.pallas_call(
        functools.partial(_ple_kernel, final=final),
        grid=(m // tm, n // tn),
        in_specs=in_specs,
        out_specs=out_specs,
        out_shape=out_shape,
        compiler_params=_params("arbitrary", "arbitrary"),
        name="ple",
    )(*args)
    return outs if final else (outs[0], (outs[1], outs[2]))


def _split_cumsum(tri, x):
    hi = x.astype(MXU_DTYPE)
    r1 = x - hi.astype(F32)
    mid = r1.astype(MXU_DTYPE)
    lo = (r1 - mid.astype(F32)).astype(MXU_DTYPE)
    return _dot(tri, hi) + _dot(tri, mid) + _dot(tri, lo)


def _gla_out_gate(o, gn, r):
    ms = jnp.mean(o * o, axis=-1, keepdims=True)
    return o * lax.rsqrt(ms + EPS) * gn * (r * _sigmoid(r))


def _gla_kernel(q_ref, k_ref, v_ref, r_ref, lr_ref, wd_ref, bd_ref, gn_ref, *rest, nc, q_scale, aliased):
    o_ref, s_ref, st_ref = rest[1:] if aliased else rest
    c = pl.program_id(2)

    @pl.when(c == 0)
    def _():
        st_ref[...] = jnp.zeros_like(st_ref)

    cs, dk = q_ref.shape
    q = q_ref[...] * q_scale
    k = k_ref[...]
    v = v_ref[...].astype(MXU_DTYPE)
    z = _dot(lr_ref[...].astype(MXU_DTYPE), wd_ref[...].astype(MXU_DTYPE)) + bd_ref[...]
    log_a = _log_sigmoid(z) / GLA_TAU
    row = lax.broadcasted_iota(jnp.int32, (cs, cs), 0)
    col = lax.broadcasted_iota(jnp.int32, (cs, cs), 1)
    b = _split_cumsum(jnp.where(row >= col, 1.0, 0.0).astype(MXU_DTYPE), log_a)

    st = st_ref[...]
    o = _dot_nt((q * jnp.exp(b)).astype(MXU_DTYPE), st.astype(MXU_DTYPE))

    sup = min(GLA_SUPER, cs)
    n_sup = cs // sup
    rowv = lax.broadcasted_iota(jnp.int32, (cs, 1), 0)
    srow = lax.broadcasted_iota(jnp.int32, (sup, sup), 0)
    scol = lax.broadcasted_iota(jnp.int32, (sup, sup), 1)
    diag = jnp.sum(q * k, axis=-1, keepdims=True)
    a_sup = [jnp.where(srow == scol, diag[i * sup:(i + 1) * sup], 0.0) for i in range(n_sup)]
    a = jnp.zeros((cs, cs), F32)
    last = b
    half = 1
    while half < cs:
        blk = 2 * half
        shift = blk.bit_length() - 1
        bottom = (rowv & (blk - 1)) >= half
        rho = jnp.where(bottom, pltpu.roll(last, half, 0), last)
        f = jnp.exp(jnp.where(bottom, b - rho, rho - b))
        qs = jnp.where(bottom, q * f, 0.0).astype(MXU_DTYPE)
        ks = jnp.where(bottom, 0.0, k * f).astype(MXU_DTYPE)
        if blk <= sup:
            same_block = (srow >> shift) == (scol >> shift)
            a_sup = [a_i + jnp.where(same_block, _dot_nt(qs[i * sup:(i + 1) * sup], ks[i * sup:(i + 1) * sup]), 0.0)
                     for i, a_i in enumerate(a_sup)]
        else:
            a = a + jnp.where((row >> shift) == (col >> shift), _dot_nt(qs, ks), 0.0)
        last = jnp.where(bottom, last, pltpu.roll(last, cs - half, 0))
        half = blk
    zero = jnp.zeros((sup, sup), F32)
    a = a + jnp.concatenate([jnp.concatenate([a_sup[i] if j == i else zero for j in range(n_sup)], axis=1)
                             for i in range(n_sup)], axis=0)
    o = o + _dot(a.astype(MXU_DTYPE), v)

    b_last = b[cs - 1:cs]
    kd = (k * jnp.exp(b_last - b)).astype(MXU_DTYPE)
    st_new = st * jnp.exp(b_last) + _dot_tn(v, kd)
    st_ref[...] = st_new

    o_ref[...] = _gla_out_gate(o, gn_ref[...], r_ref[...]).astype(o_ref.dtype)

    @pl.when(c == nc - 1)
    def _():
        s_ref[...] = st_new.T


def _gla_prompt(qkv, y, lr, wd, bd, gn, s_all, layer, depth, bsz, seq, dk, dv, m_pad):
    cs = min(GLA_CHUNK, seq)
    assert seq % cs == 0 and cs & (cs - 1) == 0 and cs % 8 == 0
    nc = seq // cs
    h_ = GLA_HEADS
    aliased = s_all is not None
    kern = functools.partial(_gla_kernel, nc=nc, q_scale=dk ** -0.5, aliased=aliased)
    row = lambda b, h, c: b * nc + c
    in_specs = [pl.BlockSpec((cs, dk), lambda b, h, c: (row(b, h, c), h)),
                pl.BlockSpec((cs, dk), lambda b, h, c: (row(b, h, c), h_ + h)),
                pl.BlockSpec((cs, dv), lambda b, h, c: (row(b, h, c), 2 * h_ * dk // dv + h)),
                pl.BlockSpec((cs, dv), lambda b, h, c: (row(b, h, c), h)),
                pl.BlockSpec((cs, LANE), lambda b, h, c: (row(b, h, c), 0)),
                pl.BlockSpec((None, LANE, dk), lambda b, h, c: (layer, 0, h)),
                pl.BlockSpec((None, 1, dk), lambda b, h, c: (layer, 0, h)),
                pl.BlockSpec((None, 1, dv), lambda b, h, c: (layer, 0, 0))]
    args = [qkv, qkv, qkv, y, lr, wd, bd, gn]
    if aliased:
        in_specs.append(pl.BlockSpec(memory_space=pl.ANY))
        args.append(s_all)
    return pl.pallas_call(
        kern,
        grid=(bsz, h_, nc),
        in_specs=in_specs,
        out_specs=[pl.BlockSpec((cs, dv), lambda b, h, c: (row(b, h, c), h)),
                   pl.BlockSpec((None, None, None, dk, dv), lambda b, h, c: (layer, b, h, 0, 0))],
        out_shape=[jax.ShapeDtypeStruct((m_pad, h_ * dv), MXU_DTYPE),
                   jax.ShapeDtypeStruct((depth, bsz, h_, dk, dv), F32)],
        scratch_shapes=[pltpu.VMEM((dv, dk), F32)],
        input_output_aliases={8: 1} if aliased else {},
        compiler_params=_params("arbitrary", "arbitrary", "arbitrary"),
        name="gla_prompt",
    )(*args)


def _gla_decode_kernel(q_ref, k_ref, v_ref, r_ref, lr_ref, wd_ref, bd_ref, gn_ref, s0_ref, *rest, nb, q_scale):
    o_ref, s_ref, acc_ref = rest[-3:]
    bi = pl.program_id(1)

    @pl.when(bi == 0)
    def _():
        acc_ref[...] = jnp.zeros_like(acc_ref)

    rows = q_ref.shape[0]
    rid = lax.broadcasted_iota(jnp.int32, (rows, 1), 0)

    def pick(x):
        return jnp.sum(jnp.where(rid == bi, x, 0.0), axis=0, keepdims=True)

    def column(x_row, n):
        eye = lax.broadcasted_iota(jnp.int32, (n, n), 0) == lax.broadcasted_iota(jnp.int32, (n, n), 1)
        return jnp.sum(jnp.where(eye, x_row, 0.0), axis=1, keepdims=True)

    dk = q_ref.shape[1]
    q = pick(q_ref[...]) * q_scale
    k = pick(k_ref[...])
    v = pick(v_ref[...])
    z = _dot(lr_ref[...].astype(MXU_DTYPE), wd_ref[...].astype(MXU_DTYPE)) + bd_ref[...]
    decay = jnp.exp(pick(_log_sigmoid(z) / GLA_TAU))
    s0 = s0_ref[...]
    qd = jnp.broadcast_to(q * decay, (rows, dk)).astype(MXU_DTYPE)
    o = _dot(qd, s0.astype(MXU_DTYPE))[0:1]
    o = o + jnp.sum(q * k, axis=-1, keepdims=True) * v
    s_ref[...] = s0 * column(decay, dk) + column(k, dk) * v

    gated = _gla_out_gate(o, gn_ref[...], pick(r_ref[...]))
    acc = jnp.where(rid == bi, gated, acc_ref[...])
    acc_ref[...] = acc

    @pl.when(bi == nb - 1)
    def _():
        pad = jnp.zeros((o_ref.shape[0] - rows, o_ref.shape[1]), F32)
        o_ref[...] = jnp.concatenate([acc, pad], axis=0).astype(o_ref.dtype)


def _gla_decode(qkv, y, lr, wd, bd, gn, state, og, s_all, layer, nb, dk, dv, m_real):
    h_ = GLA_HEADS
    r8 = m_real // 8
    rdec = m_real // DEC_ROWS
    kern = functools.partial(_gla_decode_kernel, nb=nb, q_scale=dk ** -0.5)
    any_spec = pl.BlockSpec(memory_space=pl.ANY)
    extra = [] if s_all is None else [s_all]
    og_new, s_new = pl.pallas_call(
        kern,
        grid=(h_, nb),
        in_specs=[pl.BlockSpec((8, dk), lambda h, b: (r8, h)),
                  pl.BlockSpec((8, dk), lambda h, b: (r8, h_ + h)),
                  pl.BlockSpec((8, dv), lambda h, b: (r8, 2 * h_ * dk // dv + h)),
                  pl.BlockSpec((8, dv), lambda h, b: (r8, h)),
                  pl.BlockSpec((8, LANE), lambda h, b: (r8, 0)),
                  pl.BlockSpec((None, LANE, dk), lambda h, b: (layer, 0, h)),
                  pl.BlockSpec((None, 1, dk), lambda h, b: (layer, 0, h)),
                  pl.BlockSpec((None, 1, dv), lambda h, b: (layer, 0, 0)),
                  pl.BlockSpec((None, None, None, dk, dv), lambda h, b: (layer, b, h, 0, 0)),
                  any_spec] + [any_spec] * len(extra),
        out_specs=[pl.BlockSpec((DEC_ROWS, dv), lambda h, b: (rdec, h)),
                   pl.BlockSpec((None, None, None, dk, dv), lambda h, b: (layer, b, h, 0, 0))],
        out_shape=[jax.ShapeDtypeStruct(og.shape, og.dtype),
                   jax.ShapeDtypeStruct(state.shape, F32)],
        scratch_shapes=[pltpu.VMEM((8, dv), F32)],
        input_output_aliases={9: 0, 10: 1} if extra else {9: 0},
        compiler_params=_params("arbitrary", "arbitrary"),
        name="gla_decode",
    )(qkv, qkv, qkv, y, lr, wd, bd, gn, state, og, *extra)
    return og_new, s_new


def _att_proj_kernel(h_ref, ss_ref, w_ref, g_ref, o_ref, *, n_norm):
    j = pl.program_id(1)
    acc = _dot_nt(h_ref[...], w_ref[0].astype(MXU_DTYPE)) * _row_scale(ss_ref, h_ref.shape[1])
    nh, _, hd = o_ref.shape

    @pl.when(j < n_norm)
    def _():
        for hh in range(nh):
            x = acc[:, hh * hd:(hh + 1) * hd]
            ms = jnp.mean(x * x, axis=-1, keepdims=True)
            o_ref[hh] = x * lax.rsqrt(ms + EPS) * g_ref[:, hh * hd:(hh + 1) * hd]

    @pl.when(j >= n_norm)
    def _():
        for hh in range(nh):
            o_ref[hh] = acc[:, hh * hd:(hh + 1) * hd]


def _att_proj(hn, w, layer, row0, gains):
    h, ss = hn
    m, k = h.shape
    n = 3 * ATT_W
    tm, h_spec = _lhs_spec(m, k, True)
    tn = _mm_tiles(m, n)[1]
    assert (2 * ATT_W) % tn == 0 and tn % ATT_HEAD_DIM == 0
    nh = tn // ATT_HEAD_DIM
    n_norm = 2 * ATT_W // tn
    return pl.pallas_call(
        functools.partial(_att_proj_kernel, n_norm=n_norm),
        grid=(m // tm, n // tn),
        in_specs=[h_spec, pl.BlockSpec((tm, LANE), lambda i, j: (i, 0)), _w_rows_spec(k, tn, layer, row0),
                  pl.BlockSpec((None, 1, tn), lambda i, j: (layer, 0, jnp.minimum(j, n_norm - 1)))],
        out_specs=pl.BlockSpec((nh, tm, ATT_HEAD_DIM), lambda i, j: (j, i, 0)),
        out_shape=jax.ShapeDtypeStruct((n // ATT_HEAD_DIM, m, ATT_HEAD_DIM), F32),
        compiler_params=_params("arbitrary", "arbitrary"),
        name="in_proj_att",
    )(h, ss, w, gains)


def _softmax_mix(os, ls):
    m = functools.reduce(jnp.maximum, ls)
    ws = [jnp.exp(l - m) for l in ls]
    den = functools.reduce(lambda a, b: a + b, ws)
    return functools.reduce(lambda a, b: a + b, [(w / den) * o for w, o in zip(ws, os)])


def _attn_kernel(slope_ref, q0, q1, q2, k0, k1, k2, v0, v1, v2, oa_ref, c0, c1, c2, o_s, l_s):
    j = pl.program_id(1)
    seq, hd = q0.shape
    tq = ATT_BLOCK
    row = lax.broadcasted_iota(jnp.int32, (tq, 2 * tq), 0)
    col = lax.broadcasted_iota(jnp.int32, (tq, 2 * tq), 1)
    steps = tq + row - col
    in_window = (steps >= 0) & (steps <= ATT_BLOCK)
    for g, (_, dil) in enumerate(ATT_GROUPS):
        q_ref, k_ref, v_ref = (q0, q1, q2)[g], (k0, k1, k2)[g], (v0, v1, v2)[g]
        nq = seq // (tq * dil)
        bias = slope_ref[g, j] * (steps * dil).astype(F32)

        def block(it, carry, q_ref=q_ref, k_ref=k_ref, v_ref=v_ref, g=g, dil=dil, nq=nq, bias=bias):
            r = it // nq
            i = it - r * nq
            start = r + i * (tq * dil)
            pstart = jnp.where(i > 0, start - tq * dil, start)

            def rows(st):
                return pl.ds(st, tq, stride=dil) if dil > 1 else pl.ds(pl.multiple_of(st, tq), tq)

            qh = q_ref[rows(start), :].astype(MXU_DTYPE)
            kk = jnp.concatenate([k_ref[rows(pstart), :], k_ref[rows(start), :]], axis=0).astype(MXU_DTYPE)
            vv = jnp.concatenate([v_ref[rows(pstart), :], v_ref[rows(start), :]], axis=0).astype(MXU_DTYPE)
            s = _dot_nt(qh, kk) * (hd ** -0.5)
            valid = in_window & ((col >= tq) | (i > 0))
            s = jnp.where(valid, s - bias, NEG)
            m = jnp.max(s, axis=-1, keepdims=True)
            p = jnp.exp(s - m)
            l = jnp.sum(p, axis=-1, keepdims=True)
            o_s[g, rows(start), :] = _dot(p.astype(MXU_DTYPE), vv) / l
            l_s[g, rows(start), :] = jnp.broadcast_to(m + jnp.log(l), (tq, hd))
            return carry

        lax.fori_loop(0, seq // tq, block, 0, unroll=8)
        c_ref = (c0, c1, c2)[g]
        win = c_ref.shape[1]
        c_ref[0] = k_ref[seq - win:, :]
        c_ref[1] = v_ref[seq - win:, :]
    n_g = len(ATT_GROUPS)
    oa_ref[...] = _softmax_mix([o_s[g] for g in range(n_g)], [l_s[g] for g in range(n_g)]).astype(oa_ref.dtype)


def _attn_prompt(att, slopes, bsz, seq, m_pad):
    for win, dil in ATT_GROUPS:
        assert win // dil == ATT_BLOCK and seq % (dil * ATT_BLOCK) == 0 and win <= seq
    hd = ATT_HEAD_DIM
    n_g = len(ATT_GROUPS)
    specs = [pl.BlockSpec((None, seq, hd), lambda b, j, h0=kind * ATT_HEADS + g * ATT_GROUP_HEADS: (h0 + j, b, 0))
             for kind in range(3) for g in range(n_g)]
    return pl.pallas_call(
        _attn_kernel,
        grid=(bsz, ATT_GROUP_HEADS),
        in_specs=[pl.BlockSpec(memory_space=pltpu.SMEM)] + specs,
        out_specs=[pl.BlockSpec((seq, hd), lambda b, j: (b, j))]
        + [pl.BlockSpec((2, None, None, win, hd), lambda b, j: (0, j, b, 0, 0)) for win, _ in ATT_GROUPS],
        out_shape=[jax.ShapeDtypeStruct((m_pad, ATT_GW), MXU_DTYPE)]
        + [jax.ShapeDtypeStruct((2, ATT_GROUP_HEADS, bsz, win, hd), F32) for win, _ in ATT_GROUPS],
        scratch_shapes=[pltpu.VMEM((n_g, seq, hd), F32), pltpu.VMEM((n_g, seq, hd), F32)],
        compiler_params=_params("arbitrary", "arbitrary"),
        name="attn_prompt",
    )(slopes, *([att] * 9))


def _attn_decode_kernel(slope_ref, a_ref, c0_ref, c1_ref, c2_ref, o_ref):
    nh = ATT_GROUP_HEADS
    hd = ATT_HEAD_DIM
    nrow = lax.broadcasted_iota(jnp.int32, (ATT_BLOCK, 1, 1), 0)
    os, ls = [], []
    for g, (_, dil) in enumerate(ATT_GROUPS):
        c_ref = (c0_ref, c1_ref, c2_ref)[g]
        q = a_ref[g * nh:(g + 1) * nh]
        kn = a_ref[ATT_HEADS + g * nh:ATT_HEADS + (g + 1) * nh]
        vn = a_ref[2 * ATT_HEADS + g * nh:2 * ATT_HEADS + (g + 1) * nh]
        kb = c_ref[:, 0]
        vb = c_ref[:, 1]
        slope = slope_ref[g][:, 0:1]
        dist = ((ATT_BLOCK - nrow) * dil).astype(F32)
        sb = jnp.sum(kb * q, axis=-1, keepdims=True) * (hd ** -0.5) - slope * dist
        sn = jnp.sum(kn * q, axis=-1, keepdims=True) * (hd ** -0.5)
        m = jnp.maximum(jnp.max(sb, axis=0), sn)
        pb = jnp.exp(sb - m)
        pn = jnp.exp(sn - m)
        l = jnp.sum(pb, axis=0) + pn
        os.append((jnp.sum(pb * vb, axis=0) + pn * vn) / l)
        ls.append(m + jnp.log(l))
    o_ref[...] = _softmax_mix(os, ls)


def _attn_decode(a_dec, slopes_v, caches, layer):
    nb = a_dec.shape[0]
    views, specs = [], []
    for (win, dil), cbuf in zip(ATT_GROUPS, caches):
        depth, nb_, wb = cbuf.shape[:3]
        assert wb == win and nb_ == nb
        views.append(cbuf.reshape(depth, nb, win // dil, dil, 2, ATT_GROUP_HEADS, ATT_HEAD_DIM))
        specs.append(pl.BlockSpec((None, None, win // dil, None, 2, ATT_GROUP_HEADS, ATT_HEAD_DIM),
                                  lambda b: (layer, b, 0, 0, 0, 0, 0)))
    return pl.pallas_call(
        _attn_decode_kernel,
        grid=(nb,),
        in_specs=[pl.BlockSpec(slopes_v.shape, lambda b: (0, 0, 0)),
                  pl.BlockSpec((None,) + a_dec.shape[1:], lambda b: (b, 0, 0))] + specs,
        out_specs=pl.BlockSpec((None, ATT_GROUP_HEADS, ATT_HEAD_DIM), lambda b: (b, 0, 0)),
        out_shape=jax.ShapeDtypeStruct((nb, ATT_GROUP_HEADS, ATT_HEAD_DIM), F32),
        compiler_params=_params("arbitrary"),
        name="attn_decode",
    )(slopes_v, a_dec, *views)


def _place_rows_kernel(x_ref, dst_ref, o_ref):
    del dst_ref
    pad = jnp.zeros((o_ref.shape[0] - x_ref.shape[0], o_ref.shape[1]), F32)
    o_ref[...] = jnp.concatenate([x_ref[...], pad], axis=0).astype(o_ref.dtype)


def _place_sample_rows(x, dst, m_real):
    n = dst.shape[1]
    return pl.pallas_call(
        _place_rows_kernel,
        grid=(1,),
        in_specs=[pl.BlockSpec(x.shape, lambda i: (0, 0)), pl.BlockSpec(memory_space=pl.ANY)],
        out_specs=pl.BlockSpec((DEC_ROWS, n), lambda i: (m_real // DEC_ROWS, 0)),
        out_shape=jax.ShapeDtypeStruct(dst.shape, dst.dtype),
        input_output_aliases={1: 0},
        compiler_params=_params("arbitrary"),
        name="place_sample_rows",
    )(x, dst)


def kernel(x_prompt, x_sample, state_gla, cache_w128, cache_w512, cache_w2048, p_prompt, p_sample, norm_ffn1, ffn1_w_in, ffn1_w_out, norm_mix, w_in, gla_w_decay, gla_b_decay, gla_norm, att_q_norm, att_k_norm, w_gla_out, w_att_out, w_out, norm_ffn2, ffn2_w_in, ffn2_w_out, norm_ple, w_ple_gate, w_ple_proj):
    bsz, seq, d = x_prompt.shape
    nb, dec_seq, _ = x_sample.shape
    depth = norm_ffn1.shape[0]
    assert dec_seq == 1 and nb == 8
    m_real = bsz * seq
    assert m_real % DEC_ROWS == 0
    m_pad = m_real + DEC_ROWS
    gla_qk = gla_w_decay.shape[-1]
    dk = gla_qk // GLA_HEADS
    gla_v = d
    dv = gla_v // GLA_HEADS
    caches = (cache_w128, cache_w512, cache_w2048)
    n_g = len(ATT_GROUPS)
    nh, hd = ATT_GROUP_HEADS, ATT_HEAD_DIM

    lr_off = 2 * gla_qk + gla_v
    r_off = lr_off + GLA_RANK
    q_off = r_off + gla_v
    z_off = q_off + 3 * ATT_W
    assert w_in.shape[-1] == z_off + 2 * d and lr_off % LANE == 0 and r_off % 8 == 0
    w_in_t = jnp.swapaxes(w_in, 1, 2)

    zrow = lambda n, w: jnp.zeros((n, w), F32)
    x = jnp.concatenate([x_prompt.reshape(m_real, d), x_sample.reshape(nb, d), zrow(DEC_ROWS - nb, d)], axis=0)
    pdim = p_prompt.shape[-1]
    pe = jnp.concatenate([p_prompt.reshape(depth, m_real, pdim), p_sample.reshape(depth, nb, pdim),
                          jnp.zeros((depth, DEC_ROWS - nb, pdim), F32)], axis=1)

    wd = jnp.zeros((depth, LANE, gla_qk), F32).at[:, :GLA_RANK].set(gla_w_decay)
    bd = gla_b_decay.reshape(depth, 1, gla_qk)
    gn = gla_norm.reshape(depth, 1, dv)
    att_gain = jnp.concatenate([jnp.tile(att_q_norm, (1, ATT_HEADS)), jnp.tile(att_k_norm, (1, ATT_HEADS))],
                               axis=1).reshape(depth, 1, 2 * ATT_W)
    slopes = jnp.exp2(-8.0 * jnp.arange(1, ATT_HEADS + 1, dtype=F32) / ATT_HEADS)
    slopes_s = slopes.reshape(n_g, nh)
    slopes_v = jnp.broadcast_to(slopes.reshape(n_g, nh, 1), (n_g, nh, LANE))
    g3 = lambda a: a.reshape(depth, 1, d)
    n_f1, n_mix, n_f2, n_ple = g3(norm_ffn1), g3(norm_mix), g3(norm_ffn2), g3(norm_ple)

    _, tn_g = _mm_tiles(m_pad, gla_v + 2 * d)
    n_r, n_att = gla_v // tn_g, 3 * ATT_W // tn_g
    gate_cols = lambda j: jnp.where(j < n_r, j, j + n_att)

    gla_p, gla_s, win_p, new_rows = None, None, [], []
    hn = _norm_operand(x, n_f1, 0)
    for i in range(depth):
        x, h = _ffn_out(_swiglu_in(hn, ffn1_w_in, i), ffn1_w_out, i, x, 0.5, n_mix, i)
        qkv = _matmul(h, w_in_t, i, 0, lr_off, F32, w_rows=True, big=True, name="in_proj_gla")
        lr = _matmul(h, w_in_t, i, lr_off, LANE, F32, tn=LANE, w_rows=True, name="in_proj_decay")
        gates = _matmul(h, w_in_t, i, r_off, gla_v + 2 * d, F32, col_map=gate_cols, w_rows=True, big=True,
                        name="in_proj_gates")
        att = _att_proj(h, w_in_t, i, q_off, att_gain)

        og, gla_p = _gla_prompt(qkv, gates, lr, wd, bd, gn, gla_p, i, depth, bsz, seq, dk, dv, m_pad)
        og, gla_s = _gla_decode(qkv, gates, lr, wd, bd, gn, state_gla, og, gla_s, i, nb, dk, dv, m_real)

        oa, *bufs = _attn_prompt(att, slopes_s, bsz, seq, m_pad)
        a_dec = jnp.transpose(att[:, m_real:m_real + nb], (1, 0, 2))
        o_dec = _attn_decode(a_dec, slopes_v, caches, i)
        oa = _place_sample_rows(o_dec.reshape(nb, ATT_GW), oa, m_real)

        merged = _merge(og, oa, w_gla_out, w_att_out, i, gates, gla_v, gla_v + d)
        x, hn = _matmul_residual(merged, w_out, i, x, n_f2, i)
        x, hn = _ffn_out(_swiglu_in(hn, ffn2_w_in, i), ffn2_w_out, i, x, 0.5, n_ple, i)
        if i == depth - 1:
            x = _ple(hn, pe[i], w_ple_gate, w_ple_proj, i, x, m_real=m_real)
        else:
            x, hn = _ple(hn, pe[i], w_ple_gate, w_ple_proj, i, x, n_f1, i + 1)

        win_p.append(bufs)
        new_rows.append(a_dec.reshape(nb, 3, n_g, nh, hd)[:, 1:])

    y_prompt = x[0].reshape(bsz, seq, d)
    y_sample = x[1][x[1].shape[0] - DEC_ROWS:][:nb].reshape(nb, 1, d)
    new_rows = jnp.stack(new_rows)
    shift_cfg = [(0, 0, 0), (0, 0, 0), (-1, 1, 0), (0, 0, 0), (0, 0, 0), (0, 0, 0)]
    win_s = [lax.dynamic_update_slice(lax.pad(caches[g], jnp.zeros((), F32), shift_cfg), new_rows[:, :, None, :, g],
                                      (0, 0, caches[g].shape[2] - 1, 0, 0, 0)) for g in range(n_g)]
    win_p = [jnp.transpose(jnp.stack([b[g] for b in win_p]), (0, 3, 4, 1, 2, 5)) for g in range(n_g)]
    return (y_prompt, y_sample, gla_p, win_p[0], win_p[1], win_p[2], gla_s, win_s[0], win_s[1], win_s[2])
```

```python
import functools

import jax
import jax.numpy as jnp
from jax import lax
from jax.experimental import pallas as pl
from jax.experimental.pallas import tpu as pltpu

F32 = jnp.float32
MXU_DTYPE = jnp.bfloat16

EPS = 1e-6
GLA_HEADS = 4
GLA_RANK = 16
GLA_TAU = 16.0
GLA_CHUNK = 256
GLA_SUPER = 128
ATT_GROUPS = ((128, 1), (512, 4), (2048, 16))
ATT_GROUP_HEADS = 8
ATT_HEAD_DIM = 128
ATT_HEADS = len(ATT_GROUPS) * ATT_GROUP_HEADS
ATT_GW = ATT_GROUP_HEADS * ATT_HEAD_DIM
ATT_W = ATT_HEADS * ATT_HEAD_DIM
ATT_BLOCK = 128
DEC_ROWS = 128
LANE = 128
NEG = -1e30

VMEM_LIMIT = 56 * 1024 * 1024


VMEM_LIMIT_MAX = 61 * 1024 * 1024


def _params(*sem, vmem=VMEM_LIMIT):
    return pltpu.CompilerParams(dimension_semantics=sem, vmem_limit_bytes=vmem)


def _pick_tile(n, target, mult):
    best = None
    for t in range(mult, min(n, target) + 1, mult):
        if n % t == 0:
            best = t
    assert best is not None, (n, target, mult)
    return best


_sigmoid = jax.nn.sigmoid


def _log_sigmoid(x):
    return jnp.minimum(x, 0.0) - jnp.log(1.0 + jnp.exp(-jnp.abs(x)))


def _dot(a, b):
    return jnp.dot(a, b, preferred_element_type=F32)


def _dot_nt(a, b):
    return lax.dot_general(a, b, (((1,), (1,)), ((), ())), preferred_element_type=F32)


def _dot_tn(a, b):
    return lax.dot_general(a, b, (((0,), (0,)), ((), ())), preferred_element_type=F32)


def _row_scale(ss_ref, d):
    return lax.rsqrt(jnp.sum(ss_ref[...], axis=-1, keepdims=True) / d + EPS)


def _emit_norm_operand(out, g_ref, xg_ref, ss_ref, j):
    xg_ref[...] = (out * g_ref[...]).astype(xg_ref.dtype)

    @pl.when(j == 0)
    def _():
        ss_ref[...] = jnp.zeros_like(ss_ref)

    lane = lax.broadcasted_iota(jnp.int32, ss_ref.shape, 1)
    ss_ref[...] = jnp.where(lane == j, jnp.sum(out * out, axis=-1, keepdims=True), ss_ref[...])


def _norm_out(m, n, tm, tn):
    assert n // tn <= LANE
    specs = [pl.BlockSpec((tm, tn), lambda i, j, *_: (i, j)), pl.BlockSpec((tm, LANE), lambda i, j, *_: (i, 0))]
    shapes = [jax.ShapeDtypeStruct((m, n), MXU_DTYPE), jax.ShapeDtypeStruct((m, LANE), F32)]
    return specs, shapes


def _gain_spec(tn, layer):
    return pl.BlockSpec((None, 1, tn), lambda i, j, *_: (layer, 0, j))


def _norm_operand_kernel(x_ref, g_ref, xg_ref, ss_ref):
    x = x_ref[...]
    xg_ref[...] = (x * g_ref[...]).astype(xg_ref.dtype)
    lane = lax.broadcasted_iota(jnp.int32, ss_ref.shape, 1)
    ss_ref[...] = jnp.where(lane == 0, jnp.sum(x * x, axis=-1, keepdims=True), 0.0)


def _norm_operand(x, gain, layer):
    m, d = x.shape
    tr = _pick_tile(m, 512, 16)
    return pl.pallas_call(
        _norm_operand_kernel,
        grid=(m // tr,),
        in_specs=[pl.BlockSpec((tr, d), lambda i: (i, 0)),
                  pl.BlockSpec((None, 1, d), lambda i: (layer, 0, 0))],
        out_specs=[pl.BlockSpec((tr, d), lambda i: (i, 0)), pl.BlockSpec((tr, LANE), lambda i: (i, 0))],
        out_shape=[jax.ShapeDtypeStruct((m, d), MXU_DTYPE), jax.ShapeDtypeStruct((m, LANE), F32)],
        compiler_params=_params("arbitrary"),
        name="norm_operand",
    )(x, gain)


def _mm_tiles(m, n):
    return _pick_tile(m, 1100, 16), _pick_tile(n, 512, LANE)


def _lhs_spec(m, k, big):
    if big:
        tm = _pick_tile(m, 2200, 16)
        return tm, pl.BlockSpec((tm, k), lambda i, j: (i, 0), pipeline_mode=pl.Buffered(1))
    tm = _pick_tile(m, 1100, 16)
    return tm, pl.BlockSpec((tm, k), lambda i, j: (i, 0))


def _w_spec(k, tn, layer, col_blk0):
    return pl.BlockSpec((None, k, tn), lambda i, j: (layer, 0, col_blk0 + j))


def _mm_kernel(h_ref, ss_ref, w_ref, o_ref, *, w_rows):
    w = (w_ref[0] if w_rows else w_ref[...]).astype(MXU_DTYPE)
    acc = _dot_nt(h_ref[...], w) if w_rows else _dot(h_ref[...], w)
    o_ref[...] = (acc * _row_scale(ss_ref, h_ref.shape[1])).astype(o_ref.dtype)


def _w_rows_spec(k, tn, layer, row0, col_map=None):
    col_map = col_map or (lambda j: j)
    return pl.BlockSpec((pl.Element(1), pl.Element(tn), pl.Element(k)),
                        lambda i, j: (layer, pl.multiple_of(row0 + col_map(j) * tn, 8), 0))


def _matmul(hn, w, layer, col0, n, out_dtype, tn=None, col_map=None, w_rows=False, big=False, name="matmul"):
    h, ss = hn
    m, k = h.shape
    tm, h_spec = _lhs_spec(m, k, big)
    tn = tn or _mm_tiles(m, n)[1]
    assert n % tn == 0
    if w_rows:
        w_spec = _w_rows_spec(k, tn, layer, col0, col_map)
    else:
        assert col0 % tn == 0
        col_map = col_map or (lambda j: j)
        w_spec = pl.BlockSpec((None, k, tn), lambda i, j: (layer, 0, col0 // tn + col_map(j)))
    return pl.pallas_call(
        functools.partial(_mm_kernel, w_rows=w_rows),
        grid=(m // tm, n // tn),
        in_specs=[h_spec, pl.BlockSpec((tm, LANE), lambda i, j: (i, 0)), w_spec],
        out_specs=pl.BlockSpec((tm, tn), lambda i, j: (i, j)),
        out_shape=jax.ShapeDtypeStruct((m, n), out_dtype),
        compiler_params=_params("arbitrary", "arbitrary"),
        name=name,
    )(h, ss, w)


def _mm_res_kernel(h_ref, w_ref, r_ref, g_ref, o_ref, xg_ref, ss_ref):
    out = r_ref[...] + _dot(h_ref[...], w_ref[...].astype(MXU_DTYPE))
    o_ref[...] = out
    _emit_norm_operand(out, g_ref, xg_ref, ss_ref, pl.program_id(1))


def _matmul_residual(h, w, layer, res, gain, glayer):
    m, k = h.shape
    n = w.shape[-1]
    tm, tn = _mm_tiles(m, n)
    nspecs, nshapes = _norm_out(m, n, tm, tn)
    x, xg, ss = pl.pallas_call(
        _mm_res_kernel,
        grid=(m // tm, n // tn),
        in_specs=[pl.BlockSpec((tm, k), lambda i, j: (i, 0)), _w_spec(k, tn, layer, 0),
                  pl.BlockSpec((tm, tn), lambda i, j: (i, j)), _gain_spec(tn, glayer)],
        out_specs=[pl.BlockSpec((tm, tn), lambda i, j: (i, j))] + nspecs,
        out_shape=[jax.ShapeDtypeStruct((m, n), F32)] + nshapes,
        compiler_params=_params("arbitrary", "arbitrary"),
        name="matmul_residual",
    )(h, w, res, gain)
    return x, (xg, ss)


def _swiglu_kernel(h_ref, ss_ref, wg_ref, wu_ref, o_ref):
    h = h_ref[...]
    r = _row_scale(ss_ref, h.shape[1])
    g = _dot(h, wg_ref[...].astype(MXU_DTYPE)) * r
    u = _dot(h, wu_ref[...].astype(MXU_DTYPE)) * r
    o_ref[...] = (g * _sigmoid(g) * u).astype(o_ref.dtype)


def _swiglu_in(hn, w, layer):
    h, ss = hn
    m, k = h.shape
    f = w.shape[-1] // 2
    tm, h_spec = _lhs_spec(m, k, True)
    tn = _pick_tile(f, 256, LANE)
    nf = f // tn
    return pl.pallas_call(
        _swiglu_kernel,
        grid=(m // tm, nf),
        in_specs=[h_spec, pl.BlockSpec((tm, LANE), lambda i, j: (i, 0)),
                  _w_spec(k, tn, layer, 0), _w_spec(k, tn, layer, nf)],
        out_specs=pl.BlockSpec((tm, tn), lambda i, j: (i, j)),
        out_shape=jax.ShapeDtypeStruct((m, f), MXU_DTYPE),
        compiler_params=_params("arbitrary", "arbitrary"),
        name="swiglu_in",
    )(h, ss, w, w)


def _ffn_out_kernel(a_ref, w_ref, r_ref, g_ref, o_ref, xg_ref, ss_ref, *, nk, k_valid_last, scale):
    kk = pl.program_id(2)
    tk = a_ref.shape[1]

    @pl.when(kk == 0)
    def _():
        o_ref[...] = jnp.zeros_like(o_ref)

    def accumulate(masked):
        a = a_ref[...]
        w = w_ref[...]
        if masked:
            a = jnp.where(lax.broadcasted_iota(jnp.int32, a.shape, 1) < k_valid_last, a, 0)
            w = jnp.where(lax.broadcasted_iota(jnp.int32, w.shape, 0) < k_valid_last, w, 0)
        o_ref[...] += _dot(a, w.astype(MXU_DTYPE))

    if k_valid_last == tk:
        accumulate(False)
    else:
        pl.when(kk < nk - 1)(functools.partial(accumulate, False))
        pl.when(kk == nk - 1)(functools.partial(accumulate, True))

    @pl.when(kk == nk - 1)
    def _():
        out = r_ref[...] + scale * o_ref[...]
        o_ref[...] = out
        _emit_norm_operand(out, g_ref, xg_ref, ss_ref, pl.program_id(1))


def _ffn_out(a, w, layer, res, scale, gain, glayer):
    m, k = a.shape
    n = w.shape[-1]
    tm = _pick_tile(m, 2200, 16)
    tn = _pick_tile(n, 1024, LANE)
    tk = 512
    nk = pl.cdiv(k, tk)
    kern = functools.partial(_ffn_out_kernel, nk=nk, k_valid_last=k - (nk - 1) * tk, scale=scale)
    nspecs, nshapes = _norm_out(m, n, tm, tn)
    x, xg, ss = pl.pallas_call(
        kern,
        grid=(m // tm, n // tn, nk),
        in_specs=[pl.BlockSpec((tm, tk), lambda i, j, kk: (i, kk)),
                  pl.BlockSpec((None, tk, tn), lambda i, j, kk: (layer, kk, j)),
                  pl.BlockSpec((tm, tn), lambda i, j, kk: (i, j)), _gain_spec(tn, glayer)],
        out_specs=[pl.BlockSpec((tm, tn), lambda i, j, kk: (i, j))] + nspecs,
        out_shape=[jax.ShapeDtypeStruct((m, n), F32)] + nshapes,
        compiler_params=_params("arbitrary", "arbitrary", "arbitrary", vmem=VMEM_LIMIT_MAX),
        name="ffn_out",
    )(a, w, res, gain)
    return x, (xg, ss)


def _merge_kernel(og_ref, oa_ref, wg_ref, wa_ref, zg_ref, za_ref, o_ref):
    bg = _dot(og_ref[...], wg_ref[...].astype(MXU_DTYPE))
    ba = _dot(oa_ref[...], wa_ref[...].astype(MXU_DTYPE))
    o_ref[...] = (_sigmoid(zg_ref[...]) * bg + _sigmoid(za_ref[...]) * ba).astype(o_ref.dtype)


def _merge(og, oa, w_gla_out, w_att_out, layer, y, zg_off, za_off):
    m, kg = og.shape
    ka = oa.shape[1]
    n = w_gla_out.shape[-1]
    tm, tn = _mm_tiles(m, n)
    assert zg_off % tn == 0 and za_off % tn == 0
    once = pl.Buffered(1)
    return pl.pallas_call(
        _merge_kernel,
        grid=(m // tm, n // tn),
        in_specs=[pl.BlockSpec((tm, kg), lambda i, j: (i, 0), pipeline_mode=once),
                  pl.BlockSpec((tm, ka), lambda i, j: (i, 0), pipeline_mode=once),
                  _w_spec(kg, tn, layer, 0), _w_spec(ka, tn, layer, 0),
                  pl.BlockSpec((tm, tn), lambda i, j: (i, zg_off // tn + j)),
                  pl.BlockSpec((tm, tn), lambda i, j: (i, za_off // tn + j))],
        out_specs=pl.BlockSpec((tm, tn), lambda i, j: (i, j)),
        out_shape=jax.ShapeDtypeStruct((m, n), MXU_DTYPE),
        compiler_params=_params("arbitrary", "arbitrary"),
        name="merge",
    )(og, oa, w_gla_out, w_att_out, y, y)


def _ple_kernel(h_ref, ss_ref, pe_ref, wg_ref, wp_ref, r_ref, *rest, final):
    gate = _dot(h_ref[...], wg_ref[...].astype(MXU_DTYPE)) * _row_scale(ss_ref, h_ref.shape[1])
    proj = _dot(pe_ref[...].astype(MXU_DTYPE), wp_ref[...].astype(MXU_DTYPE))
    out = r_ref[...] + _sigmoid(gate) * proj
    if final:
        o_ref, tail_ref = rest
        o_ref[...] = out
        tail_ref[...] = out[out.shape[0] - DEC_ROWS:]
    else:
        g_ref, o_ref, xg_ref, ss_out_ref = rest
        o_ref[...] = out
        _emit_norm_operand(out, g_ref, xg_ref, ss_out_ref, pl.program_id(1))


def _ple(hn, pe, w_gate, w_proj, layer, res, gain=None, glayer=None, m_real=None):
    h, ss = hn
    m, k = h.shape
    kp = pe.shape[1]
    n = w_gate.shape[-1]
    tm, tn = _mm_tiles(m, n)
    final = m_real is not None
    assert not final or (m - m_real == DEC_ROWS and tm >= DEC_ROWS)
    main = pl.BlockSpec((tm, tn), lambda i, j: (i, j))
    in_specs = [pl.BlockSpec((tm, k), lambda i, j: (i, 0)), pl.BlockSpec((tm, LANE), lambda i, j: (i, 0)),
                pl.BlockSpec((tm, kp), lambda i, j: (i, 0)),
                _w_spec(k, tn, layer, 0), _w_spec(kp, tn, layer, 0),
                pl.BlockSpec((tm, tn), lambda i, j: (i, j))]
    args = [h, ss, pe, w_gate, w_proj, res]
    if final:
        out_specs = [main, pl.BlockSpec((DEC_ROWS, tn), lambda i, j: (i, j))]
        out_shape = [jax.ShapeDtypeStruct((m_real, n), F32), jax.ShapeDtypeStruct((m // tm * DEC_ROWS, n), F32)]
    else:
        nspecs, nshapes = _norm_out(m, n, tm, tn)
        in_specs.append(_gain_spec(tn, glayer))
        args.append(gain)
        out_specs = [main] + nspecs
        out_shape = [jax.ShapeDtypeStruct((m, n), F32)] + nshapes
    outs = pl.pallas_call(
        functools.partial(_ple_kernel, final=final),
        grid=(m // tm, n // tn),
        in_specs=in_specs,
        out_specs=out_specs,
        out_shape=out_shape,
        compiler_params=_params("arbitrary", "arbitrary"),
        name="ple",
    )(*args)
    return outs if final else (outs[0], (outs[1], outs[2]))


def _split_cumsum(tri, x):
    hi = x.astype(MXU_DTYPE)
    r1 = x - hi.astype(F32)
    mid = r1.astype(MXU_DTYPE)
    lo = (r1 - mid.astype(F32)).astype(MXU_DTYPE)
    return _dot(tri, hi) + _dot(tri, mid) + _dot(tri, lo)


def _gla_out_gate(o, gn, r):
    ms = jnp.mean(o * o, axis=-1, keepdims=True)
    return o * lax.rsqrt(ms + EPS) * gn * (r * _sigmoid(r))


def _gla_kernel(q_ref, k_ref, v_ref, r_ref, lr_ref, wd_ref, bd_ref, gn_ref, *rest, nc, q_scale, aliased):
    o_ref, s_ref, st_ref = rest[1:] if aliased else rest
    c = pl.program_id(2)

    @pl.when(c == 0)
    def _():
        st_ref[...] = jnp.zeros_like(st_ref)

    cs, dk = q_ref.shape
    q = q_ref[...] * q_scale
    k = k_ref[...]
    v = v_ref[...].astype(MXU_DTYPE)
    z = _dot(lr_ref[...].astype(MXU_DTYPE), wd_ref[...].astype(MXU_DTYPE)) + bd_ref[...]
    log_a = _log_sigmoid(z) / GLA_TAU
    row = lax.broadcasted_iota(jnp.int32, (cs, cs), 0)
    col = lax.broadcasted_iota(jnp.int32, (cs, cs), 1)
    b = _split_cumsum(jnp.where(row >= col, 1.0, 0.0).astype(MXU_DTYPE), log_a)

    st = st_ref[...]
    o = _dot_nt((q * jnp.exp(b)).astype(MXU_DTYPE), st.astype(MXU_DTYPE))

    sup = min(GLA_SUPER, cs)
    n_sup = cs // sup
    rowv = lax.broadcasted_iota(jnp.int32, (cs, 1), 0)
    srow = lax.broadcasted_iota(jnp.int32, (sup, sup), 0)
    scol = lax.broadcasted_iota(jnp.int32, (sup, sup), 1)
    diag = jnp.sum(q * k, axis=-1, keepdims=True)
    a_sup = [jnp.where(srow == scol, diag[i * sup:(i + 1) * sup], 0.0) for i in range(n_sup)]
    a = jnp.zeros((cs, cs), F32)
    last = b
    half = 1
    while half < cs:
        blk = 2 * half
        shift = blk.bit_length() - 1
        if half % 8 == 0:
            split = lambda x: x.reshape(cs // blk, 2, half, dk)
            b4, q4, k4 = split(b), split(q), split(k)
            rho = b4[:, 0, half - 1:half]
            zeros = jnp.zeros((cs // blk, half, dk), F32)
            qs = jnp.stack([zeros, q4[:, 1] * jnp.exp(b4[:, 1] - rho)], axis=1).reshape(cs, dk).astype(MXU_DTYPE)
            ks = jnp.stack([k4[:, 0] * jnp.exp(rho - b4[:, 0]), zeros], axis=1).reshape(cs, dk).astype(MXU_DTYPE)
        else:
            bottom = (rowv & (blk - 1)) >= half
            rho = jnp.where(bottom, pltpu.roll(last, half, 0), last)
            f = jnp.exp(jnp.where(bottom, b - rho, rho - b))
            qs = jnp.where(bottom, q * f, 0.0).astype(MXU_DTYPE)
            ks = jnp.where(bottom, 0.0, k * f).astype(MXU_DTYPE)
            last = jnp.where(bottom, last, pltpu.roll(last, cs - half, 0))
        if blk <= sup:
            same_block = (srow >> shift) == (scol >> shift)
            a_sup = [a_i + jnp.where(same_block, _dot_nt(qs[i * sup:(i + 1) * sup], ks[i * sup:(i + 1) * sup]), 0.0)
                     for i, a_i in enumerate(a_sup)]
        else:
            a = a + jnp.where((row >> shift) == (col >> shift), _dot_nt(qs, ks), 0.0)
        half = blk
    zero = jnp.zeros((sup, sup), F32)
    a = a + jnp.concatenate([jnp.concatenate([a_sup[i] if j == i else zero for j in range(n_sup)], axis=1)
                             for i in range(n_sup)], axis=0)
    o = o + _dot(a.astype(MXU_DTYPE), v)

    b_last = b[cs - 1:cs]
    kd = (k * jnp.exp(b_last - b)).astype(MXU_DTYPE)
    st_new = st * jnp.exp(b_last) + _dot_tn(v, kd)
    st_ref[...] = st_new

    o_ref[...] = _gla_out_gate(o, gn_ref[...], r_ref[...]).astype(o_ref.dtype)

    @pl.when(c == nc - 1)
    def _():
        s_ref[...] = st_new.T


def _gla_prompt(qkv, y, lr, wd, bd, gn, s_all, layer, depth, bsz, seq, dk, dv, m_pad):
    cs = min(GLA_CHUNK, seq)
    assert seq % cs == 0 and cs & (cs - 1) == 0 and cs % 8 == 0
    nc = seq // cs
    h_ = GLA_HEADS
    aliased = s_all is not None
    kern = functools.partial(_gla_kernel, nc=nc, q_scale=dk ** -0.5, aliased=aliased)
    row = lambda b, h, c: b * nc + c
    in_specs = [pl.BlockSpec((cs, dk), lambda b, h, c: (row(b, h, c), h)),
                pl.BlockSpec((cs, dk), lambda b, h, c: (row(b, h, c), h_ + h)),
                pl.BlockSpec((cs, dv), lambda b, h, c: (row(b, h, c), 2 * h_ * dk // dv + h)),
                pl.BlockSpec((cs, dv), lambda b, h, c: (row(b, h, c), h)),
                pl.BlockSpec((cs, LANE), lambda b, h, c: (row(b, h, c), 0)),
                pl.BlockSpec((None, LANE, dk), lambda b, h, c: (layer, 0, h)),
                pl.BlockSpec((None, 1, dk), lambda b, h, c: (layer, 0, h)),
                pl.BlockSpec((None, 1, dv), lambda b, h, c: (layer, 0, 0))]
    args = [qkv, qkv, qkv, y, lr, wd, bd, gn]
    if aliased:
        in_specs.append(pl.BlockSpec(memory_space=pl.ANY))
        args.append(s_all)
    return pl.pallas_call(
        kern,
        grid=(bsz, h_, nc),
        in_specs=in_specs,
        out_specs=[pl.BlockSpec((cs, dv), lambda b, h, c: (row(b, h, c), h)),
                   pl.BlockSpec((None, None, None, dk, dv), lambda b, h, c: (layer, b, h, 0, 0))],
        out_shape=[jax.ShapeDtypeStruct((m_pad, h_ * dv), MXU_DTYPE),
                   jax.ShapeDtypeStruct((depth, bsz, h_, dk, dv), F32)],
        scratch_shapes=[pltpu.VMEM((dv, dk), F32)],
        input_output_aliases={8: 1} if aliased else {},
        compiler_params=_params("arbitrary", "arbitrary", "arbitrary"),
        name="gla_prompt",
    )(*args)


def _gla_decode_kernel(q_ref, k_ref, v_ref, r_ref, lr_ref, wd_ref, bd_ref, gn_ref, s0_ref, *rest, nb, q_scale):
    o_ref, s_ref, acc_ref = rest[-3:]
    bi = pl.program_id(1)

    @pl.when(bi == 0)
    def _():
        acc_ref[...] = jnp.zeros_like(acc_ref)

    rows = q_ref.shape[0]
    rid = lax.broadcasted_iota(jnp.int32, (rows, 1), 0)

    def pick(x):
        return jnp.sum(jnp.where(rid == bi, x, 0.0), axis=0, keepdims=True)

    def column(x_row, n):
        eye = lax.broadcasted_iota(jnp.int32, (n, n), 0) == lax.broadcasted_iota(jnp.int32, (n, n), 1)
        return jnp.sum(jnp.where(eye, x_row, 0.0), axis=1, keepdims=True)

    dk = q_ref.shape[1]
    q = pick(q_ref[...]) * q_scale
    k = pick(k_ref[...])
    v = pick(v_ref[...])
    z = _dot(lr_ref[...].astype(MXU_DTYPE), wd_ref[...].astype(MXU_DTYPE)) + bd_ref[...]
    decay = jnp.exp(pick(_log_sigmoid(z) / GLA_TAU))
    s0 = s0_ref[...]
    qd = jnp.broadcast_to(q * decay, (rows, dk)).astype(MXU_DTYPE)
    o = _dot(qd, s0.astype(MXU_DTYPE))[0:1]
    o = o + jnp.sum(q * k, axis=-1, keepdims=True) * v
    s_ref[...] = s0 * column(decay, dk) + column(k, dk) * v

    gated = _gla_out_gate(o, gn_ref[...], pick(r_ref[...]))
    acc = jnp.where(rid == bi, gated, acc_ref[...])
    acc_ref[...] = acc

    @pl.when(bi == nb - 1)
    def _():
        pad = jnp.zeros((o_ref.shape[0] - rows, o_ref.shape[1]), F32)
        o_ref[...] = jnp.concatenate([acc, pad], axis=0).astype(o_ref.dtype)


def _gla_decode(qkv, y, lr, wd, bd, gn, state, og, s_all, layer, nb, dk, dv, m_real):
    h_ = GLA_HEADS
    r8 = m_real // 8
    rdec = m_real // DEC_ROWS
    kern = functools.partial(_gla_decode_kernel, nb=nb, q_scale=dk ** -0.5)
    any_spec = pl.BlockSpec(memory_space=pl.ANY)
    extra = [] if s_all is None else [s_all]
    og_new, s_new = pl.pallas_call(
        kern,
        grid=(h_, nb),
        in_specs=[pl.BlockSpec((8, dk), lambda h, b: (r8, h)),
                  pl.BlockSpec((8, dk), lambda h, b: (r8, h_ + h)),
                  pl.BlockSpec((8, dv), lambda h, b: (r8, 2 * h_ * dk // dv + h)),
                  pl.BlockSpec((8, dv), lambda h, b: (r8, h)),
                  pl.BlockSpec((8, LANE), lambda h, b: (r8, 0)),
                  pl.BlockSpec((None, LANE, dk), lambda h, b: (layer, 0, h)),
                  pl.BlockSpec((None, 1, dk), lambda h, b: (layer, 0, h)),
                  pl.BlockSpec((None, 1, dv), lambda h, b: (layer, 0, 0)),
                  pl.BlockSpec((None, None, None, dk, dv), lambda h, b: (layer, b, h, 0, 0)),
                  any_spec] + [any_spec] * len(extra),
        out_specs=[pl.BlockSpec((DEC_ROWS, dv), lambda h, b: (rdec, h)),
                   pl.BlockSpec((None, None, None, dk, dv), lambda h, b: (layer, b, h, 0, 0))],
        out_shape=[jax.ShapeDtypeStruct(og.shape, og.dtype),
                   jax.ShapeDtypeStruct(state.shape, F32)],
        scratch_shapes=[pltpu.VMEM((8, dv), F32)],
        input_output_aliases={9: 0, 10: 1} if extra else {9: 0},
        compiler_params=_params("arbitrary", "arbitrary"),
        name="gla_decode",
    )(qkv, qkv, qkv, y, lr, wd, bd, gn, state, og, *extra)
    return og_new, s_new


def _att_proj_kernel(h_ref, ss_ref, w_ref, g_ref, o_ref, *, n_norm):
    acc = _dot_nt(h_ref[...], w_ref[0].astype(MXU_DTYPE)) * _row_scale(ss_ref, h_ref.shape[1])
    nh, _, hd = o_ref.shape
    normed = pl.program_id(1) < n_norm
    for hh in range(nh):
        x = acc[:, hh * hd:(hh + 1) * hd]
        ms = jnp.mean(x * x, axis=-1, keepdims=True)
        o_ref[hh] = x * jnp.where(normed, lax.rsqrt(ms + EPS) * g_ref[:, hh * hd:(hh + 1) * hd], 1.0)


def _att_proj(hn, w, layer, row0, gains):
    h, ss = hn
    m, k = h.shape
    n = 3 * ATT_W
    tm, h_spec = _lhs_spec(m, k, True)
    tn = _mm_tiles(m, n)[1]
    assert (2 * ATT_W) % tn == 0 and tn % ATT_HEAD_DIM == 0
    nh = tn // ATT_HEAD_DIM
    n_norm = 2 * ATT_W // tn
    return pl.pallas_call(
        functools.partial(_att_proj_kernel, n_norm=n_norm),
        grid=(m // tm, n // tn),
        in_specs=[h_spec, pl.BlockSpec((tm, LANE), lambda i, j: (i, 0)), _w_rows_spec(k, tn, layer, row0),
                  pl.BlockSpec((None, 1, tn), lambda i, j: (layer, 0, jnp.minimum(j, n_norm - 1)))],
        out_specs=pl.BlockSpec((nh, tm, ATT_HEAD_DIM), lambda i, j: (j, i, 0)),
        out_shape=jax.ShapeDtypeStruct((n // ATT_HEAD_DIM, m, ATT_HEAD_DIM), F32),
        compiler_params=_params("arbitrary", "arbitrary"),
        name="in_proj_att",
    )(h, ss, w, gains)


def _softmax_mix(os, ls):
    m = functools.reduce(jnp.maximum, ls)
    ws = [jnp.exp(l - m) for l in ls]
    den = functools.reduce(lambda a, b: a + b, ws)
    return functools.reduce(lambda a, b: a + b, [(w / den) * o for w, o in zip(ws, os)])


def _attn_kernel(slope_ref, q0, q1, q2, k0, k1, k2, v0, v1, v2, *rest):
    oa_ref, c0, c1, c2, o_s, l_s = rest[-6:]
    j = pl.program_id(1)
    seq, hd = q0.shape
    tq = ATT_BLOCK
    row = lax.broadcasted_iota(jnp.int32, (tq, 2 * tq), 0)
    col = lax.broadcasted_iota(jnp.int32, (tq, 2 * tq), 1)
    steps = tq + row - col
    in_window = (steps >= 0) & (steps <= ATT_BLOCK)
    for g, (_, dil) in enumerate(ATT_GROUPS):
        q_ref, k_ref, v_ref = (q0, q1, q2)[g], (k0, k1, k2)[g], (v0, v1, v2)[g]
        nq = seq // (tq * dil)
        bias = slope_ref[g, j] * (steps * dil).astype(F32)

        def block(it, carry, q_ref=q_ref, k_ref=k_ref, v_ref=v_ref, g=g, dil=dil, nq=nq, bias=bias):
            r = it // nq
            i = it - r * nq
            start = r + i * (tq * dil)
            pstart = jnp.where(i > 0, start - tq * dil, start)

            def rows(st):
                return pl.ds(st, tq, stride=dil) if dil > 1 else pl.ds(pl.multiple_of(st, tq), tq)

            qh = q_ref[rows(start), :].astype(MXU_DTYPE)
            kk = jnp.concatenate([k_ref[rows(pstart), :], k_ref[rows(start), :]], axis=0).astype(MXU_DTYPE)
            vv = jnp.concatenate([v_ref[rows(pstart), :], v_ref[rows(start), :]], axis=0).astype(MXU_DTYPE)
            s = _dot_nt(qh, kk) * (hd ** -0.5)
            valid = in_window & ((col >= tq) | (i > 0))
            s = jnp.where(valid, s - bias, NEG)
            m = jnp.max(s, axis=-1, keepdims=True)
            p = jnp.exp(s - m)
            l = jnp.sum(p, axis=-1, keepdims=True)
            o_s[g, rows(start), :] = _dot(p.astype(MXU_DTYPE), vv) / l
            l_s[g, rows(start), :] = jnp.broadcast_to(m + jnp.log(l), (tq, hd))
            return carry

        lax.fori_loop(0, seq // tq, block, 0, unroll=8)
        c_ref = (c0, c1, c2)[g]
        win = c_ref.shape[1]
        c_ref[0] = k_ref[seq - win:, :]
        c_ref[1] = v_ref[seq - win:, :]
    n_g = len(ATT_GROUPS)
    oa_ref[...] = _softmax_mix([o_s[g] for g in range(n_g)], [l_s[g] for g in range(n_g)]).astype(oa_ref.dtype)


def _attn_prompt(att, slopes, bufs, layer, depth, bsz, seq, m_pad):
    for win, dil in ATT_GROUPS:
        assert win // dil == ATT_BLOCK and seq % (dil * ATT_BLOCK) == 0 and win <= seq
    hd = ATT_HEAD_DIM
    n_g = len(ATT_GROUPS)
    specs = [pl.BlockSpec((None, seq, hd), lambda b, j, h0=kind * ATT_HEADS + g * ATT_GROUP_HEADS: (h0 + j, b, 0))
             for kind in range(3) for g in range(n_g)]
    bufs = list(bufs or [])
    outs = pl.pallas_call(
        _attn_kernel,
        grid=(bsz, ATT_GROUP_HEADS),
        in_specs=[pl.BlockSpec(memory_space=pltpu.SMEM)] + specs + [pl.BlockSpec(memory_space=pl.ANY)] * len(bufs),
        out_specs=[pl.BlockSpec((seq, hd), lambda b, j: (b, j))]
        + [pl.BlockSpec((None, 2, None, None, win, hd), lambda b, j: (layer, 0, j, b, 0, 0)) for win, _ in ATT_GROUPS],
        out_shape=[jax.ShapeDtypeStruct((m_pad, ATT_GW), MXU_DTYPE)]
        + [jax.ShapeDtypeStruct((depth, 2, ATT_GROUP_HEADS, bsz, win, hd), F32) for win, _ in ATT_GROUPS],
        scratch_shapes=[pltpu.VMEM((n_g, seq, hd), F32), pltpu.VMEM((n_g, seq, hd), F32)],
        input_output_aliases={10 + g: 1 + g for g in range(len(bufs))},
        compiler_params=_params("arbitrary", "arbitrary"),
        name="attn_prompt",
    )(slopes, *([att] * 9), *bufs)
    return outs[0], outs[1:]


def _attn_decode_kernel(slope_ref, a_ref, c0_ref, c1_ref, c2_ref, o_ref):
    nh = ATT_GROUP_HEADS
    hd = ATT_HEAD_DIM
    nrow = lax.broadcasted_iota(jnp.int32, (ATT_BLOCK, 1, 1), 0)
    os, ls = [], []
    for g, (_, dil) in enumerate(ATT_GROUPS):
        c_ref = (c0_ref, c1_ref, c2_ref)[g]
        q = a_ref[g * nh:(g + 1) * nh]
        kn = a_ref[ATT_HEADS + g * nh:ATT_HEADS + (g + 1) * nh]
        vn = a_ref[2 * ATT_HEADS + g * nh:2 * ATT_HEADS + (g + 1) * nh]
        kb = c_ref[:, 0]
        vb = c_ref[:, 1]
        slope = slope_ref[g][:, 0:1]
        dist = ((ATT_BLOCK - nrow) * dil).astype(F32)
        sb = jnp.sum(kb * q, axis=-1, keepdims=True) * (hd ** -0.5) - slope * dist
        sn = jnp.sum(kn * q, axis=-1, keepdims=True) * (hd ** -0.5)
        m = jnp.maximum(jnp.max(sb, axis=0), sn)
        pb = jnp.exp(sb - m)
        pn = jnp.exp(sn - m)
        l = jnp.sum(pb, axis=0) + pn
        os.append((jnp.sum(pb * vb, axis=0) + pn * vn) / l)
        ls.append(m + jnp.log(l))
    o_ref[...] = _softmax_mix(os, ls)


def _attn_decode(a_dec, slopes_v, caches, layer):
    nb = a_dec.shape[0]
    views, specs = [], []
    for (win, dil), cbuf in zip(ATT_GROUPS, caches):
        depth, nb_, wb = cbuf.shape[:3]
        assert wb == win and nb_ == nb
        views.append(cbuf.reshape(depth, nb, win // dil, dil, 2, ATT_GROUP_HEADS, ATT_HEAD_DIM))
        specs.append(pl.BlockSpec((None, None, win // dil, None, 2, ATT_GROUP_HEADS, ATT_HEAD_DIM),
                                  lambda b: (layer, b, 0, 0, 0, 0, 0)))
    return pl.pallas_call(
        _attn_decode_kernel,
        grid=(nb,),
        in_specs=[pl.BlockSpec(slopes_v.shape, lambda b: (0, 0, 0)),
                  pl.BlockSpec((None,) + a_dec.shape[1:], lambda b: (b, 0, 0))] + specs,
        out_specs=pl.BlockSpec((None, ATT_GROUP_HEADS, ATT_HEAD_DIM), lambda b: (b, 0, 0)),
        out_shape=jax.ShapeDtypeStruct((nb, ATT_GROUP_HEADS, ATT_HEAD_DIM), F32),
        compiler_params=_params("arbitrary"),
        name="attn_decode",
    )(slopes_v, a_dec, *views)


def _place_rows_kernel(x_ref, dst_ref, o_ref):
    del dst_ref
    pad = jnp.zeros((o_ref.shape[0] - x_ref.shape[0], o_ref.shape[1]), F32)
    o_ref[...] = jnp.concatenate([x_ref[...], pad], axis=0).astype(o_ref.dtype)


def _place_sample_rows(x, dst, m_real):
    n = dst.shape[1]
    return pl.pallas_call(
        _place_rows_kernel,
        grid=(1,),
        in_specs=[pl.BlockSpec(x.shape, lambda i: (0, 0)), pl.BlockSpec(memory_space=pl.ANY)],
        out_specs=pl.BlockSpec((DEC_ROWS, n), lambda i: (m_real // DEC_ROWS, 0)),
        out_shape=jax.ShapeDtypeStruct(dst.shape, dst.dtype),
        input_output_aliases={1: 0},
        compiler_params=_params("arbitrary"),
        name="place_sample_rows",
    )(x, dst)


def kernel(x_prompt, x_sample, state_gla, cache_w128, cache_w512, cache_w2048, p_prompt, p_sample, norm_ffn1, ffn1_w_in, ffn1_w_out, norm_mix, w_in, gla_w_decay, gla_b_decay, gla_norm, att_q_norm, att_k_norm, w_gla_out, w_att_out, w_out, norm_ffn2, ffn2_w_in, ffn2_w_out, norm_ple, w_ple_gate, w_ple_proj):
    bsz, seq, d = x_prompt.shape
    nb, dec_seq, _ = x_sample.shape
    depth = norm_ffn1.shape[0]
    assert dec_seq == 1 and nb == 8
    m_real = bsz * seq
    assert m_real % DEC_ROWS == 0
    m_pad = m_real + DEC_ROWS
    gla_qk = gla_w_decay.shape[-1]
    dk = gla_qk // GLA_HEADS
    gla_v = d
    dv = gla_v // GLA_HEADS
    caches = (cache_w128, cache_w512, cache_w2048)
    n_g = len(ATT_GROUPS)
    nh, hd = ATT_GROUP_HEADS, ATT_HEAD_DIM

    lr_off = 2 * gla_qk + gla_v
    r_off = lr_off + GLA_RANK
    q_off = r_off + gla_v
    z_off = q_off + 3 * ATT_W
    assert w_in.shape[-1] == z_off + 2 * d and lr_off % LANE == 0 and r_off % 8 == 0
    w_in_t = jnp.swapaxes(w_in, 1, 2)

    zrow = lambda n, w: jnp.zeros((n, w), F32)
    x = jnp.concatenate([x_prompt.reshape(m_real, d), x_sample.reshape(nb, d), zrow(DEC_ROWS - nb, d)], axis=0)
    pdim = p_prompt.shape[-1]
    pe = jnp.concatenate([p_prompt.reshape(depth, m_real, pdim), p_sample.reshape(depth, nb, pdim),
                          jnp.zeros((depth, DEC_ROWS - nb, pdim), F32)], axis=1)

    wd = jnp.zeros((depth, LANE, gla_qk), F32).at[:, :GLA_RANK].set(gla_w_decay)
    bd = gla_b_decay.reshape(depth, 1, gla_qk)
    gn = gla_norm.reshape(depth, 1, dv)
    att_gain = jnp.concatenate([jnp.tile(att_q_norm, (1, ATT_HEADS)), jnp.tile(att_k_norm, (1, ATT_HEADS))],
                               axis=1).reshape(depth, 1, 2 * ATT_W)
    slopes = jnp.exp2(-8.0 * jnp.arange(1, ATT_HEADS + 1, dtype=F32) / ATT_HEADS)
    slopes_s = slopes.reshape(n_g, nh)
    slopes_v = jnp.broadcast_to(slopes.reshape(n_g, nh, 1), (n_g, nh, LANE))
    g3 = lambda a: a.reshape(depth, 1, d)
    n_f1, n_mix, n_f2, n_ple = g3(norm_ffn1), g3(norm_mix), g3(norm_ffn2), g3(norm_ple)

    _, tn_g = _mm_tiles(m_pad, gla_v + 2 * d)
    n_r, n_att = gla_v // tn_g, 3 * ATT_W // tn_g
    gate_cols = lambda j: jnp.where(j < n_r, j, j + n_att)

    gla_p, gla_s, win_p, new_rows = None, None, None, []
    hn = _norm_operand(x, n_f1, 0)
    for i in range(depth):
        x, h = _ffn_out(_swiglu_in(hn, ffn1_w_in, i), ffn1_w_out, i, x, 0.5, n_mix, i)
        qkv = _matmul(h, w_in_t, i, 0, lr_off, F32, w_rows=True, big=True, name="in_proj_gla")
        lr = _matmul(h, w_in_t, i, lr_off, LANE, F32, tn=LANE, w_rows=True, name="in_proj_decay")
        gates = _matmul(h, w_in_t, i, r_off, gla_v + 2 * d, F32, col_map=gate_cols, w_rows=True, big=True,
                        name="in_proj_gates")
        att = _att_proj(h, w_in_t, i, q_off, att_gain)

        og, gla_p = _gla_prompt(qkv, gates, lr, wd, bd, gn, gla_p, i, depth, bsz, seq, dk, dv, m_pad)
        og, gla_s = _gla_decode(qkv, gates, lr, wd, bd, gn, state_gla, og, gla_s, i, nb, dk, dv, m_real)

        oa, win_p = _attn_prompt(att, slopes_s, win_p, i, depth, bsz, seq, m_pad)
        a_dec = jnp.transpose(att[:, m_real:m_real + nb], (1, 0, 2))
        o_dec = _attn_decode(a_dec, slopes_v, caches, i)
        oa = _place_sample_rows(o_dec.reshape(nb, ATT_GW), oa, m_real)

        merged = _merge(og, oa, w_gla_out, w_att_out, i, gates, gla_v, gla_v + d)
        x, hn = _matmul_residual(merged, w_out, i, x, n_f2, i)
        x, hn = _ffn_out(_swiglu_in(hn, ffn2_w_in, i), ffn2_w_out, i, x, 0.5, n_ple, i)
        if i == depth - 1:
            x = _ple(hn, pe[i], w_ple_gate, w_ple_proj, i, x, m_real=m_real)
        else:
            x, hn = _ple(hn, pe[i], w_ple_gate, w_ple_proj, i, x, n_f1, i + 1)

        new_rows.append(a_dec.reshape(nb, 3, n_g, nh, hd)[:, 1:])

    y_prompt = x[0].reshape(bsz, seq, d)
    y_sample = x[1][x[1].shape[0] - DEC_ROWS:][:nb].reshape(nb, 1, d)
    new_rows = jnp.stack(new_rows)
    shift_cfg = [(0, 0, 0), (0, 0, 0), (-1, 1, 0), (0, 0, 0), (0, 0, 0), (0, 0, 0)]
    win_s = [lax.dynamic_update_slice(lax.pad(caches[g], jnp.zeros((), F32), shift_cfg), new_rows[:, :, None, :, g],
                                      (0, 0, caches[g].shape[2] - 1, 0, 0, 0)) for g in range(n_g)]
    win_p = [jnp.transpose(w, (0, 3, 4, 1, 2, 5)) for w in win_p]
    return (y_prompt, y_sample, gla_p, win_p[0], win_p[1], win_p[2], gla_s, win_s[0], win_s[1], win_s[2])
```

```python
import functools

import jax
import jax.numpy as jnp
from jax import lax
from jax.experimental import pallas as pl
from jax.experimental.pallas import tpu as pltpu

F32 = jnp.float32
MXU_DTYPE = jnp.bfloat16

EPS = 1e-6
GLA_HEADS = 4
GLA_RANK = 16
GLA_TAU = 16.0
GLA_CHUNK = 256
GLA_SUPER = 128
ATT_GROUPS = ((128, 1), (512, 4), (2048, 16))
ATT_GROUP_HEADS = 8
ATT_HEAD_DIM = 128
ATT_HEADS = len(ATT_GROUPS) * ATT_GROUP_HEADS
ATT_GW = ATT_GROUP_HEADS * ATT_HEAD_DIM
ATT_W = ATT_HEADS * ATT_HEAD_DIM
ATT_BLOCK = 128
DEC_ROWS = 128
LANE = 128
NEG = -1e30

VMEM_LIMIT = 56 * 1024 * 1024


VMEM_LIMIT_MAX = 61 * 1024 * 1024


def _params(*sem, vmem=VMEM_LIMIT):
    return pltpu.CompilerParams(dimension_semantics=sem, vmem_limit_bytes=vmem)


def _pick_tile(n, target, mult):
    best = None
    for t in range(mult, min(n, target) + 1, mult):
        if n % t == 0:
            best = t
    assert best is not None, (n, target, mult)
    return best


_sigmoid = jax.nn.sigmoid


def _log_sigmoid(x):
    return jnp.minimum(x, 0.0) - jnp.log(1.0 + jnp.exp(-jnp.abs(x)))


def _dot(a, b):
    return jnp.dot(a, b, preferred_element_type=F32)


def _dot_nt(a, b):
    return lax.dot_general(a, b, (((1,), (1,)), ((), ())), preferred_element_type=F32)


def _dot_tn(a, b):
    return lax.dot_general(a, b, (((0,), (0,)), ((), ())), preferred_element_type=F32)


def _row_scale(ss_ref, d):
    return lax.rsqrt(jnp.sum(ss_ref[...], axis=-1, keepdims=True) / d + EPS)


def _emit_norm_operand(out, g_ref, xg_ref, ss_ref, j):
    xg_ref[...] = (out * g_ref[...]).astype(xg_ref.dtype)

    @pl.when(j == 0)
    def _():
        ss_ref[...] = jnp.zeros_like(ss_ref)

    lane = lax.broadcasted_iota(jnp.int32, ss_ref.shape, 1)
    ss_ref[...] = jnp.where(lane == j, jnp.sum(out * out, axis=-1, keepdims=True), ss_ref[...])


def _norm_out(m, n, tm, tn):
    assert n // tn <= LANE
    specs = [pl.BlockSpec((tm, tn), lambda i, j, *_: (i, j)), pl.BlockSpec((tm, LANE), lambda i, j, *_: (i, 0))]
    shapes = [jax.ShapeDtypeStruct((m, n), MXU_DTYPE), jax.ShapeDtypeStruct((m, LANE), F32)]
    return specs, shapes


def _gain_spec(tn, layer):
    return pl.BlockSpec((None, 1, tn), lambda i, j, *_: (layer, 0, j))


def _norm_operand_kernel(x_ref, g_ref, xg_ref, ss_ref):
    x = x_ref[...]
    xg_ref[...] = (x * g_ref[...]).astype(xg_ref.dtype)
    lane = lax.broadcasted_iota(jnp.int32, ss_ref.shape, 1)
    ss_ref[...] = jnp.where(lane == 0, jnp.sum(x * x, axis=-1, keepdims=True), 0.0)


def _norm_operand(x, gain, layer):
    m, d = x.shape
    tr = _pick_tile(m, 512, 16)
    return pl.pallas_call(
        _norm_operand_kernel,
        grid=(m // tr,),
        in_specs=[pl.BlockSpec((tr, d), lambda i: (i, 0)),
                  pl.BlockSpec((None, 1, d), lambda i: (layer, 0, 0))],
        out_specs=[pl.BlockSpec((tr, d), lambda i: (i, 0)), pl.BlockSpec((tr, LANE), lambda i: (i, 0))],
        out_shape=[jax.ShapeDtypeStruct((m, d), MXU_DTYPE), jax.ShapeDtypeStruct((m, LANE), F32)],
        compiler_params=_params("arbitrary"),
        name="norm_operand",
    )(x, gain)


def _mm_tiles(m, n):
    return _pick_tile(m, 1100, 16), _pick_tile(n, 512, LANE)


def _lhs_spec(m, k, big):
    if big:
        tm = _pick_tile(m, 2200, 16)
        return tm, pl.BlockSpec((tm, k), lambda i, j: (i, 0), pipeline_mode=pl.Buffered(1))
    tm = _pick_tile(m, 1100, 16)
    return tm, pl.BlockSpec((tm, k), lambda i, j: (i, 0))


def _w_spec(k, tn, layer, col_blk0):
    return pl.BlockSpec((None, k, tn), lambda i, j: (layer, 0, col_blk0 + j))


def _mm_kernel(h_ref, ss_ref, w_ref, o_ref, *, w_rows):
    w = (w_ref[0] if w_rows else w_ref[...]).astype(MXU_DTYPE)
    acc = _dot_nt(h_ref[...], w) if w_rows else _dot(h_ref[...], w)
    o_ref[...] = (acc * _row_scale(ss_ref, h_ref.shape[1])).astype(o_ref.dtype)


def _w_rows_spec(k, tn, layer, row0, col_map=None):
    col_map = col_map or (lambda j: j)
    return pl.BlockSpec((pl.Element(1), pl.Element(tn), pl.Element(k)),
                        lambda i, j: (layer, pl.multiple_of(row0 + col_map(j) * tn, 8), 0))


def _matmul(hn, w, layer, col0, n, out_dtype, tn=None, col_map=None, w_rows=False, big=False, name="matmul"):
    h, ss = hn
    m, k = h.shape
    tm, h_spec = _lhs_spec(m, k, big)
    tn = tn or _mm_tiles(m, n)[1]
    assert n % tn == 0
    if w_rows:
        w_spec = _w_rows_spec(k, tn, layer, col0, col_map)
    else:
        assert col0 % tn == 0
        col_map = col_map or (lambda j: j)
        w_spec = pl.BlockSpec((None, k, tn), lambda i, j: (layer, 0, col0 // tn + col_map(j)))
    return pl.pallas_call(
        functools.partial(_mm_kernel, w_rows=w_rows),
        grid=(m // tm, n // tn),
        in_specs=[h_spec, pl.BlockSpec((tm, LANE), lambda i, j: (i, 0)), w_spec],
        out_specs=pl.BlockSpec((tm, tn), lambda i, j: (i, j)),
        out_shape=jax.ShapeDtypeStruct((m, n), out_dtype),
        compiler_params=_params("arbitrary", "arbitrary"),
        name=name,
    )(h, ss, w)


def _mm_res_kernel(h_ref, w_ref, r_ref, g_ref, o_ref, xg_ref, ss_ref):
    out = r_ref[...] + _dot(h_ref[...], w_ref[...].astype(MXU_DTYPE))
    o_ref[...] = out
    _emit_norm_operand(out, g_ref, xg_ref, ss_ref, pl.program_id(1))


def _matmul_residual(h, w, layer, res, gain, glayer):
    m, k = h.shape
    n = w.shape[-1]
    tm, tn = _mm_tiles(m, n)
    nspecs, nshapes = _norm_out(m, n, tm, tn)
    x, xg, ss = pl.pallas_call(
        _mm_res_kernel,
        grid=(m // tm, n // tn),
        in_specs=[pl.BlockSpec((tm, k), lambda i, j: (i, 0)), _w_spec(k, tn, layer, 0),
                  pl.BlockSpec((tm, tn), lambda i, j: (i, j)), _gain_spec(tn, glayer)],
        out_specs=[pl.BlockSpec((tm, tn), lambda i, j: (i, j))] + nspecs,
        out_shape=[jax.ShapeDtypeStruct((m, n), F32)] + nshapes,
        compiler_params=_params("arbitrary", "arbitrary"),
        name="matmul_residual",
    )(h, w, res, gain)
    return x, (xg, ss)


def _swiglu_kernel(h_ref, ss_ref, wg_ref, wu_ref, o_ref):
    wg = wg_ref[...].astype(MXU_DTYPE)
    wu = wu_ref[...].astype(MXU_DTYPE)
    tm, k = h_ref.shape
    nchunk = 2 if tm % 32 == 0 else 1
    for rows in [pl.ds(c * (tm // nchunk), tm // nchunk) for c in range(nchunk)]:
        h = h_ref[rows, :]
        r = lax.rsqrt(jnp.sum(ss_ref[rows, :], axis=-1, keepdims=True) / k + EPS)
        g = _dot(h, wg) * r
        u = _dot(h, wu) * r
        o_ref[rows, :] = (g * _sigmoid(g) * u).astype(o_ref.dtype)


def _swiglu_in(hn, w, layer):
    h, ss = hn
    m, k = h.shape
    f = w.shape[-1] // 2
    tm, h_spec = _lhs_spec(m, k, True)
    tn = _pick_tile(f, 256, LANE)
    nf = f // tn
    return pl.pallas_call(
        _swiglu_kernel,
        grid=(m // tm, nf),
        in_specs=[h_spec, pl.BlockSpec((tm, LANE), lambda i, j: (i, 0)),
                  _w_spec(k, tn, layer, 0), _w_spec(k, tn, layer, nf)],
        out_specs=pl.BlockSpec((tm, tn), lambda i, j: (i, j)),
        out_shape=jax.ShapeDtypeStruct((m, f), MXU_DTYPE),
        compiler_params=_params("arbitrary", "arbitrary"),
        name="swiglu_in",
    )(h, ss, w, w)


def _ffn_out_kernel(a_ref, w_ref, r_ref, g_ref, o_ref, xg_ref, ss_ref, *, nk, k_valid_last, scale):
    kk = pl.program_id(2)
    tk = a_ref.shape[1]

    @pl.when(kk == 0)
    def _():
        o_ref[...] = jnp.zeros_like(o_ref)

    def accumulate(masked):
        a = a_ref[...]
        w = w_ref[...]
        if masked:
            a = jnp.where(lax.broadcasted_iota(jnp.int32, a.shape, 1) < k_valid_last, a, 0)
            w = jnp.where(lax.broadcasted_iota(jnp.int32, w.shape, 0) < k_valid_last, w, 0)
        o_ref[...] += _dot(a, w.astype(MXU_DTYPE))

    if k_valid_last == tk:
        accumulate(False)
    else:
        pl.when(kk < nk - 1)(functools.partial(accumulate, False))
        pl.when(kk == nk - 1)(functools.partial(accumulate, True))

    @pl.when(kk == nk - 1)
    def _():
        out = r_ref[...] + scale * o_ref[...]
        o_ref[...] = out
        _emit_norm_operand(out, g_ref, xg_ref, ss_ref, pl.program_id(1))


def _ffn_out(a, w, layer, res, scale, gain, glayer):
    m, k = a.shape
    n = w.shape[-1]
    tm = _pick_tile(m, 2200, 16)
    tn = _pick_tile(n, 1024, LANE)
    tk = 1024
    nk = pl.cdiv(k, tk)
    kern = functools.partial(_ffn_out_kernel, nk=nk, k_valid_last=k - (nk - 1) * tk, scale=scale)
    nspecs, nshapes = _norm_out(m, n, tm, tn)
    x, xg, ss = pl.pallas_call(
        kern,
        grid=(m // tm, n // tn, nk),
        in_specs=[pl.BlockSpec((tm, tk), lambda i, j, kk: (i, kk)),
                  pl.BlockSpec((None, tk, tn), lambda i, j, kk: (layer, kk, j)),
                  pl.BlockSpec((tm, tn), lambda i, j, kk: (i, j), pipeline_mode=pl.Buffered(1)),
                  _gain_spec(tn, glayer)],
        out_specs=[pl.BlockSpec((tm, tn), lambda i, j, kk: (i, j))] + nspecs,
        out_shape=[jax.ShapeDtypeStruct((m, n), F32)] + nshapes,
        compiler_params=_params("arbitrary", "arbitrary", "arbitrary", vmem=VMEM_LIMIT_MAX),
        name="ffn_out",
    )(a, w, res, gain)
    return x, (xg, ss)


def _merge_kernel(og_ref, oa_ref, wg_ref, wa_ref, zg_ref, za_ref, o_ref):
    bg = _dot(og_ref[...], wg_ref[...].astype(MXU_DTYPE))
    ba = _dot(oa_ref[...], wa_ref[...].astype(MXU_DTYPE))
    o_ref[...] = (_sigmoid(zg_ref[...]) * bg + _sigmoid(za_ref[...]) * ba).astype(o_ref.dtype)


def _merge(og, oa, w_gla_out, w_att_out, layer, y, zg_off, za_off):
    m, kg = og.shape
    ka = oa.shape[1]
    n = w_gla_out.shape[-1]
    tm, tn = _mm_tiles(m, n)
    assert zg_off % tn == 0 and za_off % tn == 0
    once = pl.Buffered(1)
    return pl.pallas_call(
        _merge_kernel,
        grid=(m // tm, n // tn),
        in_specs=[pl.BlockSpec((tm, kg), lambda i, j: (i, 0), pipeline_mode=once),
                  pl.BlockSpec((tm, ka), lambda i, j: (i, 0), pipeline_mode=once),
                  _w_spec(kg, tn, layer, 0), _w_spec(ka, tn, layer, 0),
                  pl.BlockSpec((tm, tn), lambda i, j: (i, zg_off // tn + j)),
                  pl.BlockSpec((tm, tn), lambda i, j: (i, za_off // tn + j))],
        out_specs=pl.BlockSpec((tm, tn), lambda i, j: (i, j)),
        out_shape=jax.ShapeDtypeStruct((m, n), MXU_DTYPE),
        compiler_params=_params("arbitrary", "arbitrary"),
        name="merge",
    )(og, oa, w_gla_out, w_att_out, y, y)


def _ple_kernel(h_ref, ss_ref, pe_ref, wg_ref, wp_ref, r_ref, *rest, final):
    gate = _dot(h_ref[...], wg_ref[...].astype(MXU_DTYPE)) * _row_scale(ss_ref, h_ref.shape[1])
    proj = _dot(pe_ref[...].astype(MXU_DTYPE), wp_ref[...].astype(MXU_DTYPE))
    out = r_ref[...] + _sigmoid(gate) * proj
    if final:
        o_ref, tail_ref = rest
        o_ref[...] = out
        tail_ref[...] = out[out.shape[0] - DEC_ROWS:]
    else:
        g_ref, o_ref, xg_ref, ss_out_ref = rest
        o_ref[...] = out
        _emit_norm_operand(out, g_ref, xg_ref, ss_out_ref, pl.program_id(1))


def _ple(hn, pe, w_gate, w_proj, layer, res, gain=None, glayer=None, m_real=None):
    h, ss = hn
    m, k = h.shape
    kp = pe.shape[1]
    n = w_gate.shape[-1]
    tm, tn = _mm_tiles(m, n)
    final = m_real is not None
    assert not final or (m - m_real == DEC_ROWS and tm >= DEC_ROWS)
    main = pl.BlockSpec((tm, tn), lambda i, j: (i, j))
    in_specs = [pl.BlockSpec((tm, k), lambda i, j: (i, 0)), pl.BlockSpec((tm, LANE), lambda i, j: (i, 0)),
                pl.BlockSpec((tm, kp), lambda i, j: (i, 0)),
                _w_spec(k, tn, layer, 0), _w_spec(kp, tn, layer, 0),
                pl.BlockSpec((tm, tn), lambda i, j: (i, j))]
    args = [h, ss, pe, w_gate, w_proj, res]
    if final:
        out_specs = [main, pl.BlockSpec((DEC_ROWS, tn), lambda i, j: (i, j))]
        out_shape = [jax.ShapeDtypeStruct((m_real, n), F32), jax.ShapeDtypeStruct((m // tm * DEC_ROWS, n), F32)]
    else:
        nspecs, nshapes = _norm_out(m, n, tm, tn)
        in_specs.append(_gain_spec(tn, glayer))
        args.append(gain)
        out_specs = [main] + nspecs
        out_shape = [jax.ShapeDtypeStruct((m, n), F32)] + nshapes
    outs = pl.pallas_call(
        functools.partial(_ple_kernel, final=final),
        grid=(m // tm, n // tn),
        in_specs=in_specs,
        out_specs=out_specs,
        out_shape=out_shape,
        compiler_params=_params("arbitrary", "arbitrary"),
        name="ple",
    )(*args)
    return outs if final else (outs[0], (outs[1], outs[2]))


def _split_cumsum(tri, x):
    hi = x.astype(MXU_DTYPE)
    r1 = x - hi.astype(F32)
    mid = r1.astype(MXU_DTYPE)
    lo = (r1 - mid.astype(F32)).astype(MXU_DTYPE)
    return _dot(tri, hi) + _dot(tri, mid) + _dot(tri, lo)


def _gla_out_gate(o, gn, r):
    ms = jnp.mean(o * o, axis=-1, keepdims=True)
    return o * lax.rsqrt(ms + EPS) * gn * (r * _sigmoid(r))


def _gla_kernel(q_ref, k_ref, v_ref, r_ref, lr_ref, wd_ref, bd_ref, gn_ref, *rest, nc, q_scale, aliased):
    o_ref, s_ref, st_ref = rest[1:] if aliased else rest
    c = pl.program_id(2)

    @pl.when(c == 0)
    def _():
        st_ref[...] = jnp.zeros_like(st_ref)

    cs, dk = q_ref.shape
    q = q_ref[...] * q_scale
    k = k_ref[...]
    v = v_ref[...].astype(MXU_DTYPE)
    z = _dot(lr_ref[...].astype(MXU_DTYPE), wd_ref[...].astype(MXU_DTYPE)) + bd_ref[...]
    log_a = _log_sigmoid(z) / GLA_TAU
    row = lax.broadcasted_iota(jnp.int32, (cs, cs), 0)
    col = lax.broadcasted_iota(jnp.int32, (cs, cs), 1)
    b = _split_cumsum(jnp.where(row >= col, 1.0, 0.0).astype(MXU_DTYPE), log_a)

    st = st_ref[...]
    o = _dot_nt((q * jnp.exp(b)).astype(MXU_DTYPE), st.astype(MXU_DTYPE))

    sup = min(GLA_SUPER, cs)
    n_sup = cs // sup
    rowv = lax.broadcasted_iota(jnp.int32, (cs, 1), 0)
    srow = lax.broadcasted_iota(jnp.int32, (sup, sup), 0)
    scol = lax.broadcasted_iota(jnp.int32, (sup, sup), 1)
    diag = jnp.sum(q * k, axis=-1, keepdims=True)
    a_sup = [jnp.where(srow == scol, diag[i * sup:(i + 1) * sup], 0.0) for i in range(n_sup)]
    a = jnp.zeros((cs, cs), F32)
    last = b
    half = 1
    while half < cs:
        blk = 2 * half
        shift = blk.bit_length() - 1
        if half % 8 == 0:
            split = lambda x: x.reshape(cs // blk, 2, half, dk)
            b4, q4, k4 = split(b), split(q), split(k)
            rho = b4[:, 0, half - 1:half]
            zeros = jnp.zeros((cs // blk, half, dk), F32)
            qs = jnp.stack([zeros, q4[:, 1] * jnp.exp(b4[:, 1] - rho)], axis=1).reshape(cs, dk).astype(MXU_DTYPE)
            ks = jnp.stack([k4[:, 0] * jnp.exp(rho - b4[:, 0]), zeros], axis=1).reshape(cs, dk).astype(MXU_DTYPE)
        else:
            bottom = (rowv & (blk - 1)) >= half
            rho = jnp.where(bottom, pltpu.roll(last, half, 0), last)
            f = jnp.exp(jnp.where(bottom, b - rho, rho - b))
            qs = jnp.where(bottom, q * f, 0.0).astype(MXU_DTYPE)
            ks = jnp.where(bottom, 0.0, k * f).astype(MXU_DTYPE)
            last = jnp.where(bottom, last, pltpu.roll(last, cs - half, 0))
        if blk <= sup:
            same_block = (srow >> shift) == (scol >> shift)
            a_sup = [a_i + jnp.where(same_block, _dot_nt(qs[i * sup:(i + 1) * sup], ks[i * sup:(i + 1) * sup]), 0.0)
                     for i, a_i in enumerate(a_sup)]
        else:
            a = a + jnp.where((row >> shift) == (col >> shift), _dot_nt(qs, ks), 0.0)
        half = blk
    zero = jnp.zeros((sup, sup), F32)
    a = a + jnp.concatenate([jnp.concatenate([a_sup[i] if j == i else zero for j in range(n_sup)], axis=1)
                             for i in range(n_sup)], axis=0)
    o = o + _dot(a.astype(MXU_DTYPE), v)

    b_last = b[cs - 1:cs]
    kd = (k * jnp.exp(b_last - b)).astype(MXU_DTYPE)
    st_new = st * jnp.exp(b_last) + _dot_tn(v, kd)
    st_ref[...] = st_new

    o_ref[...] = _gla_out_gate(o, gn_ref[...], r_ref[...]).astype(o_ref.dtype)

    @pl.when(c == nc - 1)
    def _():
        s_ref[...] = st_new.T


def _gla_prompt(qkv, y, lr, wd, bd, gn, s_all, layer, depth, bsz, seq, dk, dv, m_pad):
    cs = min(GLA_CHUNK, seq)
    assert seq % cs == 0 and cs & (cs - 1) == 0 and cs % 8 == 0
    nc = seq // cs
    h_ = GLA_HEADS
    aliased = s_all is not None
    kern = functools.partial(_gla_kernel, nc=nc, q_scale=dk ** -0.5, aliased=aliased)
    row = lambda b, h, c: b * nc + c
    in_specs = [pl.BlockSpec((cs, dk), lambda b, h, c: (row(b, h, c), h)),
                pl.BlockSpec((cs, dk), lambda b, h, c: (row(b, h, c), h_ + h)),
                pl.BlockSpec((cs, dv), lambda b, h, c: (row(b, h, c), 2 * h_ * dk // dv + h)),
                pl.BlockSpec((cs, dv), lambda b, h, c: (row(b, h, c), h)),
                pl.BlockSpec((cs, LANE), lambda b, h, c: (row(b, h, c), 0)),
                pl.BlockSpec((None, LANE, dk), lambda b, h, c: (layer, 0, h)),
                pl.BlockSpec((None, 1, dk), lambda b, h, c: (layer, 0, h)),
                pl.BlockSpec((None, 1, dv), lambda b, h, c: (layer, 0, 0))]
    args = [qkv, qkv, qkv, y, lr, wd, bd, gn]
    if aliased:
        in_specs.append(pl.BlockSpec(memory_space=pl.ANY))
        args.append(s_all)
    return pl.pallas_call(
        kern,
        grid=(bsz, h_, nc),
        in_specs=in_specs,
        out_specs=[pl.BlockSpec((cs, dv), lambda b, h, c: (row(b, h, c), h)),
                   pl.BlockSpec((None, None, None, dk, dv), lambda b, h, c: (layer, b, h, 0, 0))],
        out_shape=[jax.ShapeDtypeStruct((m_pad, h_ * dv), MXU_DTYPE),
                   jax.ShapeDtypeStruct((depth, bsz, h_, dk, dv), F32)],
        scratch_shapes=[pltpu.VMEM((dv, dk), F32)],
        input_output_aliases={8: 1} if aliased else {},
        compiler_params=_params("arbitrary", "arbitrary", "arbitrary"),
        name="gla_prompt",
    )(*args)


def _gla_decode_kernel(q_ref, k_ref, v_ref, r_ref, lr_ref, wd_ref, bd_ref, gn_ref, s0_ref, *rest, nb, q_scale):
    o_ref, s_ref, acc_ref = rest[-3:]
    bi = pl.program_id(1)

    @pl.when(bi == 0)
    def _():
        acc_ref[...] = jnp.zeros_like(acc_ref)

    rows = q_ref.shape[0]
    rid = lax.broadcasted_iota(jnp.int32, (rows, 1), 0)

    def pick(x):
        return jnp.sum(jnp.where(rid == bi, x, 0.0), axis=0, keepdims=True)

    def column(x_row, n):
        eye = lax.broadcasted_iota(jnp.int32, (n, n), 0) == lax.broadcasted_iota(jnp.int32, (n, n), 1)
        return jnp.sum(jnp.where(eye, x_row, 0.0), axis=1, keepdims=True)

    dk = q_ref.shape[1]
    q = pick(q_ref[...]) * q_scale
    k = pick(k_ref[...])
    v = pick(v_ref[...])
    z = _dot(lr_ref[...].astype(MXU_DTYPE), wd_ref[...].astype(MXU_DTYPE)) + bd_ref[...]
    decay = jnp.exp(pick(_log_sigmoid(z) / GLA_TAU))
    s0 = s0_ref[...]
    qd = jnp.broadcast_to(q * decay, (rows, dk)).astype(MXU_DTYPE)
    o = _dot(qd, s0.astype(MXU_DTYPE))[0:1]
    o = o + jnp.sum(q * k, axis=-1, keepdims=True) * v
    s_ref[...] = s0 * column(decay, dk) + column(k, dk) * v

    gated = _gla_out_gate(o, gn_ref[...], pick(r_ref[...]))
    acc = jnp.where(rid == bi, gated, acc_ref[...])
    acc_ref[...] = acc

    @pl.when(bi == nb - 1)
    def _():
        pad = jnp.zeros((o_ref.shape[0] - rows, o_ref.shape[1]), F32)
        o_ref[...] = jnp.concatenate([acc, pad], axis=0).astype(o_ref.dtype)


def _gla_decode(qkv, y, lr, wd, bd, gn, state, og, s_all, layer, nb, dk, dv, m_real):
    h_ = GLA_HEADS
    r8 = m_real // 8
    rdec = m_real // DEC_ROWS
    kern = functools.partial(_gla_decode_kernel, nb=nb, q_scale=dk ** -0.5)
    any_spec = pl.BlockSpec(memory_space=pl.ANY)
    extra = [] if s_all is None else [s_all]
    og_new, s_new = pl.pallas_call(
        kern,
        grid=(h_, nb),
        in_specs=[pl.BlockSpec((8, dk), lambda h, b: (r8, h)),
                  pl.BlockSpec((8, dk), lambda h, b: (r8, h_ + h)),
                  pl.BlockSpec((8, dv), lambda h, b: (r8, 2 * h_ * dk // dv + h)),
                  pl.BlockSpec((8, dv), lambda h, b: (r8, h)),
                  pl.BlockSpec((8, LANE), lambda h, b: (r8, 0)),
                  pl.BlockSpec((None, LANE, dk), lambda h, b: (layer, 0, h)),
                  pl.BlockSpec((None, 1, dk), lambda h, b: (layer, 0, h)),
                  pl.BlockSpec((None, 1, dv), lambda h, b: (layer, 0, 0)),
                  pl.BlockSpec((None, None, None, dk, dv), lambda h, b: (layer, b, h, 0, 0)),
                  any_spec] + [any_spec] * len(extra),
        out_specs=[pl.BlockSpec((DEC_ROWS, dv), lambda h, b: (rdec, h)),
                   pl.BlockSpec((None, None, None, dk, dv), lambda h, b: (layer, b, h, 0, 0))],
        out_shape=[jax.ShapeDtypeStruct(og.shape, og.dtype),
                   jax.ShapeDtypeStruct(state.shape, F32)],
        scratch_shapes=[pltpu.VMEM((8, dv), F32)],
        input_output_aliases={9: 0, 10: 1} if extra else {9: 0},
        compiler_params=_params("arbitrary", "arbitrary"),
        name="gla_decode",
    )(qkv, qkv, qkv, y, lr, wd, bd, gn, state, og, *extra)
    return og_new, s_new


def _att_proj_kernel(h_ref, ss_ref, w_ref, g_ref, o_ref, *, n_norm):
    w = w_ref[0].astype(MXU_DTYPE)
    tm, k = h_ref.shape
    nh, _, hd = o_ref.shape
    normed = pl.program_id(1) < n_norm
    nchunk = 2 if tm % 32 == 0 else 1
    for rows in [pl.ds(c * (tm // nchunk), tm // nchunk) for c in range(nchunk)]:
        r = lax.rsqrt(jnp.sum(ss_ref[rows, :], axis=-1, keepdims=True) / k + EPS)
        acc = _dot_nt(h_ref[rows, :], w) * r
        for hh in range(nh):
            x = acc[:, hh * hd:(hh + 1) * hd]
            ms = jnp.mean(x * x, axis=-1, keepdims=True)
            o_ref[hh, rows, :] = x * jnp.where(normed, lax.rsqrt(ms + EPS) * g_ref[:, hh * hd:(hh + 1) * hd], 1.0)


def _att_proj(hn, w, layer, row0, gains):
    h, ss = hn
    m, k = h.shape
    n = 3 * ATT_W
    tm, h_spec = _lhs_spec(m, k, True)
    tn = _mm_tiles(m, n)[1]
    assert (2 * ATT_W) % tn == 0 and tn % ATT_HEAD_DIM == 0
    nh = tn // ATT_HEAD_DIM
    n_norm = 2 * ATT_W // tn
    return pl.pallas_call(
        functools.partial(_att_proj_kernel, n_norm=n_norm),
        grid=(m // tm, n // tn),
        in_specs=[h_spec, pl.BlockSpec((tm, LANE), lambda i, j: (i, 0)), _w_rows_spec(k, tn, layer, row0),
                  pl.BlockSpec((None, 1, tn), lambda i, j: (layer, 0, jnp.minimum(j, n_norm - 1)))],
        out_specs=pl.BlockSpec((nh, tm, ATT_HEAD_DIM), lambda i, j: (j, i, 0)),
        out_shape=jax.ShapeDtypeStruct((n // ATT_HEAD_DIM, m, ATT_HEAD_DIM), F32),
        compiler_params=_params("arbitrary", "arbitrary"),
        name="in_proj_att",
    )(h, ss, w, gains)


def _softmax_mix(os, ls):
    m = functools.reduce(jnp.maximum, ls)
    ws = [jnp.exp(l - m) for l in ls]
    den = functools.reduce(lambda a, b: a + b, ws)
    return functools.reduce(lambda a, b: a + b, [(w / den) * o for w, o in zip(ws, os)])


def _attn_kernel(slope_ref, q0, q1, q2, k0, k1, k2, v0, v1, v2, *rest):
    oa_ref, c0, c1, c2, o_s, l_s = rest[-6:]
    j = pl.program_id(1)
    seq, hd = q0.shape
    tq = ATT_BLOCK
    row = lax.broadcasted_iota(jnp.int32, (tq, 2 * tq), 0)
    col = lax.broadcasted_iota(jnp.int32, (tq, 2 * tq), 1)
    steps = tq + row - col
    in_window = (steps >= 0) & (steps <= ATT_BLOCK)
    for g, (_, dil) in enumerate(ATT_GROUPS):
        q_ref, k_ref, v_ref = (q0, q1, q2)[g], (k0, k1, k2)[g], (v0, v1, v2)[g]
        nq = seq // (tq * dil)
        bias = slope_ref[g, j] * (steps * dil).astype(F32)

        def block(it, carry, q_ref=q_ref, k_ref=k_ref, v_ref=v_ref, g=g, dil=dil, nq=nq, bias=bias):
            r = it // nq
            i = it - r * nq
            start = r + i * (tq * dil)
            pstart = jnp.where(i > 0, start - tq * dil, start)

            def rows(st):
                return pl.ds(st, tq, stride=dil) if dil > 1 else pl.ds(pl.multiple_of(st, tq), tq)

            qh = q_ref[rows(start), :].astype(MXU_DTYPE)
            kk = jnp.concatenate([k_ref[rows(pstart), :], k_ref[rows(start), :]], axis=0).astype(MXU_DTYPE)
            vv = jnp.concatenate([v_ref[rows(pstart), :], v_ref[rows(start), :]], axis=0).astype(MXU_DTYPE)
            s = _dot_nt(qh, kk) * (hd ** -0.5)
            valid = in_window & ((col >= tq) | (i > 0))
            s = jnp.where(valid, s - bias, NEG)
            m = jnp.max(s, axis=-1, keepdims=True)
            p = jnp.exp(s - m)
            l = jnp.sum(p, axis=-1, keepdims=True)
            o_s[g, rows(start), :] = _dot(p.astype(MXU_DTYPE), vv) / l
            l_s[g, rows(start), :] = jnp.broadcast_to(m + jnp.log(l), (tq, hd))
            return carry

        lax.fori_loop(0, seq // tq, block, 0, unroll=8)
        c_ref = (c0, c1, c2)[g]
        win = c_ref.shape[1]
        c_ref[0] = k_ref[seq - win:, :]
        c_ref[1] = v_ref[seq - win:, :]
    n_g = len(ATT_GROUPS)
    oa_ref[...] = _softmax_mix([o_s[g] for g in range(n_g)], [l_s[g] for g in range(n_g)]).astype(oa_ref.dtype)


def _attn_prompt(att, slopes, bufs, layer, depth, bsz, seq, m_pad):
    for win, dil in ATT_GROUPS:
        assert win // dil == ATT_BLOCK and seq % (dil * ATT_BLOCK) == 0 and win <= seq
    hd = ATT_HEAD_DIM
    n_g = len(ATT_GROUPS)
    specs = [pl.BlockSpec((None, seq, hd), lambda b, j, h0=kind * ATT_HEADS + g * ATT_GROUP_HEADS: (h0 + j, b, 0))
             for kind in range(3) for g in range(n_g)]
    bufs = list(bufs or [])
    outs = pl.pallas_call(
        _attn_kernel,
        grid=(bsz, ATT_GROUP_HEADS),
        in_specs=[pl.BlockSpec(memory_space=pltpu.SMEM)] + specs + [pl.BlockSpec(memory_space=pl.ANY)] * len(bufs),
        out_specs=[pl.BlockSpec((seq, hd), lambda b, j: (b, j))]
        + [pl.BlockSpec((None, 2, None, None, win, hd), lambda b, j: (layer, 0, j, b, 0, 0)) for win, _ in ATT_GROUPS],
        out_shape=[jax.ShapeDtypeStruct((m_pad, ATT_GW), MXU_DTYPE)]
        + [jax.ShapeDtypeStruct((depth, 2, ATT_GROUP_HEADS, bsz, win, hd), F32) for win, _ in ATT_GROUPS],
        scratch_shapes=[pltpu.VMEM((n_g, seq, hd), F32), pltpu.VMEM((n_g, seq, hd), F32)],
        input_output_aliases={10 + g: 1 + g for g in range(len(bufs))},
        compiler_params=_params("arbitrary", "arbitrary"),
        name="attn_prompt",
    )(slopes, *([att] * 9), *bufs)
    return outs[0], outs[1:]


def _attn_decode_kernel(slope_ref, a_ref, c0_ref, c1_ref, c2_ref, o_ref):
    nh = ATT_GROUP_HEADS
    hd = ATT_HEAD_DIM
    nrow = lax.broadcasted_iota(jnp.int32, (ATT_BLOCK, 1, 1), 0)
    os, ls = [], []
    for g, (_, dil) in enumerate(ATT_GROUPS):
        c_ref = (c0_ref, c1_ref, c2_ref)[g]
        q = a_ref[g * nh:(g + 1) * nh]
        kn = a_ref[ATT_HEADS + g * nh:ATT_HEADS + (g + 1) * nh]
        vn = a_ref[2 * ATT_HEADS + g * nh:2 * ATT_HEADS + (g + 1) * nh]
        kb = c_ref[:, 0]
        vb = c_ref[:, 1]
        slope = slope_ref[g][:, 0:1]
        dist = ((ATT_BLOCK - nrow) * dil).astype(F32)
        sb = jnp.sum(kb * q, axis=-1, keepdims=True) * (hd ** -0.5) - slope * dist
        sn = jnp.sum(kn * q, axis=-1, keepdims=True) * (hd ** -0.5)
        m = jnp.maximum(jnp.max(sb, axis=0), sn)
        pb = jnp.exp(sb - m)
        pn = jnp.exp(sn - m)
        l = jnp.sum(pb, axis=0) + pn
        os.append((jnp.sum(pb * vb, axis=0) + pn * vn) / l)
        ls.append(m + jnp.log(l))
    o_ref[...] = _softmax_mix(os, ls)


def _attn_decode(a_dec, slopes_v, caches, layer):
    nb = a_dec.shape[0]
    views, specs = [], []
    for (win, dil), cbuf in zip(ATT_GROUPS, caches):
        depth, nb_, wb = cbuf.shape[:3]
        assert wb == win and nb_ == nb
        views.append(cbuf.reshape(depth, nb, win // dil, dil, 2, ATT_GROUP_HEADS, ATT_HEAD_DIM))
        specs.append(pl.BlockSpec((None, None, win // dil, None, 2, ATT_GROUP_HEADS, ATT_HEAD_DIM),
                                  lambda b: (layer, b, 0, 0, 0, 0, 0)))
    return pl.pallas_call(
        _attn_decode_kernel,
        grid=(nb,),
        in_specs=[pl.BlockSpec(slopes_v.shape, lambda b: (0, 0, 0)),
                  pl.BlockSpec((None,) + a_dec.shape[1:], lambda b: (b, 0, 0))] + specs,
        out_specs=pl.BlockSpec((None, ATT_GROUP_HEADS, ATT_HEAD_DIM), lambda b: (b, 0, 0)),
        out_shape=jax.ShapeDtypeStruct((nb, ATT_GROUP_HEADS, ATT_HEAD_DIM), F32),
        compiler_params=_params("arbitrary"),
        name="attn_decode",
    )(slopes_v, a_dec, *views)


def _place_rows_kernel(x_ref, dst_ref, o_ref):
    del dst_ref
    pad = jnp.zeros((o_ref.shape[0] - x_ref.shape[0], o_ref.shape[1]), F32)
    o_ref[...] = jnp.concatenate([x_ref[...], pad], axis=0).astype(o_ref.dtype)


def _place_sample_rows(x, dst, m_real):
    n = dst.shape[1]
    return pl.pallas_call(
        _place_rows_kernel,
        grid=(1,),
        in_specs=[pl.BlockSpec(x.shape, lambda i: (0, 0)), pl.BlockSpec(memory_space=pl.ANY)],
        out_specs=pl.BlockSpec((DEC_ROWS, n), lambda i: (m_real // DEC_ROWS, 0)),
        out_shape=jax.ShapeDtypeStruct(dst.shape, dst.dtype),
        input_output_aliases={1: 0},
        compiler_params=_params("arbitrary"),
        name="place_sample_rows",
    )(x, dst)


def kernel(x_prompt, x_sample, state_gla, cache_w128, cache_w512, cache_w2048, p_prompt, p_sample, norm_ffn1, ffn1_w_in, ffn1_w_out, norm_mix, w_in, gla_w_decay, gla_b_decay, gla_norm, att_q_norm, att_k_norm, w_gla_out, w_att_out, w_out, norm_ffn2, ffn2_w_in, ffn2_w_out, norm_ple, w_ple_gate, w_ple_proj):
    bsz, seq, d = x_prompt.shape
    nb, dec_seq, _ = x_sample.shape
    depth = norm_ffn1.shape[0]
    assert dec_seq == 1 and nb == 8
    m_real = bsz * seq
    assert m_real % DEC_ROWS == 0
    m_pad = m_real + DEC_ROWS
    gla_qk = gla_w_decay.shape[-1]
    dk = gla_qk // GLA_HEADS
    gla_v = d
    dv = gla_v // GLA_HEADS
    caches = (cache_w128, cache_w512, cache_w2048)
    n_g = len(ATT_GROUPS)
    nh, hd = ATT_GROUP_HEADS, ATT_HEAD_DIM

    lr_off = 2 * gla_qk + gla_v
    r_off = lr_off + GLA_RANK
    q_off = r_off + gla_v
    z_off = q_off + 3 * ATT_W
    assert w_in.shape[-1] == z_off + 2 * d and lr_off % LANE == 0 and r_off % 8 == 0
    w_in_t = jnp.swapaxes(w_in, 1, 2)

    zrow = lambda n, w: jnp.zeros((n, w), F32)
    x = jnp.concatenate([x_prompt.reshape(m_real, d), x_sample.reshape(nb, d), zrow(DEC_ROWS - nb, d)], axis=0)
    pdim = p_prompt.shape[-1]
    pe = jnp.concatenate([p_prompt.reshape(depth, m_real, pdim), p_sample.reshape(depth, nb, pdim),
                          jnp.zeros((depth, DEC_ROWS - nb, pdim), F32)], axis=1)

    wd = jnp.zeros((depth, LANE, gla_qk), F32).at[:, :GLA_RANK].set(gla_w_decay)
    bd = gla_b_decay.reshape(depth, 1, gla_qk)
    gn = gla_norm.reshape(depth, 1, dv)
    att_gain = jnp.concatenate([jnp.tile(att_q_norm, (1, ATT_HEADS)), jnp.tile(att_k_norm, (1, ATT_HEADS))],
                               axis=1).reshape(depth, 1, 2 * ATT_W)
    slopes = jnp.exp2(-8.0 * jnp.arange(1, ATT_HEADS + 1, dtype=F32) / ATT_HEADS)
    slopes_s = slopes.reshape(n_g, nh)
    slopes_v = jnp.broadcast_to(slopes.reshape(n_g, nh, 1), (n_g, nh, LANE))
    g3 = lambda a: a.reshape(depth, 1, d)
    n_f1, n_mix, n_f2, n_ple = g3(norm_ffn1), g3(norm_mix), g3(norm_ffn2), g3(norm_ple)

    _, tn_g = _mm_tiles(m_pad, gla_v + 2 * d)
    n_r, n_att = gla_v // tn_g, 3 * ATT_W // tn_g
    gate_cols = lambda j: jnp.where(j < n_r, j, j + n_att)

    gla_p, gla_s, win_p, new_rows = None, None, None, []
    hn = _norm_operand(x, n_f1, 0)
    for i in range(depth):
        x, h = _ffn_out(_swiglu_in(hn, ffn1_w_in, i), ffn1_w_out, i, x, 0.5, n_mix, i)
        qkv = _matmul(h, w_in_t, i, 0, lr_off, F32, w_rows=True, big=True, name="in_proj_gla")
        lr = _matmul(h, w_in_t, i, lr_off, LANE, F32, tn=LANE, w_rows=True, name="in_proj_decay")
        gates = _matmul(h, w_in_t, i, r_off, gla_v + 2 * d, F32, col_map=gate_cols, w_rows=True, big=True,
                        name="in_proj_gates")
        att = _att_proj(h, w_in_t, i, q_off, att_gain)

        og, gla_p = _gla_prompt(qkv, gates, lr, wd, bd, gn, gla_p, i, depth, bsz, seq, dk, dv, m_pad)
        og, gla_s = _gla_decode(qkv, gates, lr, wd, bd, gn, state_gla, og, gla_s, i, nb, dk, dv, m_real)

        oa, win_p = _attn_prompt(att, slopes_s, win_p, i, depth, bsz, seq, m_pad)
        a_dec = jnp.transpose(att[:, m_real:m_real + nb], (1, 0, 2))
        o_dec = _attn_decode(a_dec, slopes_v, caches, i)
        oa = _place_sample_rows(o_dec.reshape(nb, ATT_GW), oa, m_real)

        merged = _merge(og, oa, w_gla_out, w_att_out, i, gates, gla_v, gla_v + d)
        x, hn = _matmul_residual(merged, w_out, i, x, n_f2, i)
        x, hn = _ffn_out(_swiglu_in(hn, ffn2_w_in, i), ffn2_w_out, i, x, 0.5, n_ple, i)
        if i == depth - 1:
            x = _ple(hn, pe[i], w_ple_gate, w_ple_proj, i, x, m_real=m_real)
        else:
            x, hn = _ple(hn, pe[i], w_ple_gate, w_ple_proj, i, x, n_f1, i + 1)

        new_rows.append(a_dec.reshape(nb, 3, n_g, nh, hd)[:, 1:])

    y_prompt = x[0].reshape(bsz, seq, d)
    y_sample = x[1][x[1].shape[0] - DEC_ROWS:][:nb].reshape(nb, 1, d)
    new_rows = jnp.stack(new_rows)
    shift_cfg = [(0, 0, 0), (0, 0, 0), (-1, 1, 0), (0, 0, 0), (0, 0, 0), (0, 0, 0)]
    win_s = [lax.dynamic_update_slice(lax.pad(caches[g], jnp.zeros((), F32), shift_cfg), new_rows[:, :, None, :, g],
                                      (0, 0, caches[g].shape[2] - 1, 0, 0, 0)) for g in range(n_g)]
    win_p = [jnp.transpose(w, (0, 3, 4, 1, 2, 5)) for w in win_p]
    return (y_prompt, y_sample, gla_p, win_p[0], win_p[1], win_p[2], gla_s, win_s[0], win_s[1], win_s[2])
```

```python
import functools

import jax
import jax.numpy as jnp
from jax import lax
from jax.experimental import pallas as pl
from jax.experimental.pallas import tpu as pltpu

F32 = jnp.float32
MXU_DTYPE = jnp.bfloat16

EPS = 1e-6
GLA_HEADS = 4
GLA_RANK = 16
GLA_TAU = 16.0
GLA_CHUNK = 256
GLA_SUPER = 128
ATT_GROUPS = ((128, 1), (512, 4), (2048, 16))
ATT_GROUP_HEADS = 8
ATT_HEAD_DIM = 128
ATT_HEADS = len(ATT_GROUPS) * ATT_GROUP_HEADS
ATT_GW = ATT_GROUP_HEADS * ATT_HEAD_DIM
ATT_W = ATT_HEADS * ATT_HEAD_DIM
ATT_BLOCK = 128
DEC_ROWS = 128
LANE = 128
NEG = -1e30

VMEM_LIMIT = 56 * 1024 * 1024


FFN_OUT_CHUNK = 512
VMEM_LIMIT_MAX =61 * 1024 * 1024


def _params(*sem, vmem=VMEM_LIMIT):
    return pltpu.CompilerParams(dimension_semantics=sem, vmem_limit_bytes=vmem)


def _pick_tile(n, target, mult):
    best = None
    for t in range(mult, min(n, target) + 1, mult):
        if n % t == 0:
            best = t
    assert best is not None, (n, target, mult)
    return best


_sigmoid = jax.nn.sigmoid


def _log_sigmoid(x):
    return jnp.minimum(x, 0.0) - jnp.log(1.0 + jnp.exp(-jnp.abs(x)))


def _dot(a, b):
    return jnp.dot(a, b, preferred_element_type=F32)


def _dot_nt(a, b):
    return lax.dot_general(a, b, (((1,), (1,)), ((), ())), preferred_element_type=F32)


def _dot_tn(a, b):
    return lax.dot_general(a, b, (((0,), (0,)), ((), ())), preferred_element_type=F32)


def _row_scale(ss_ref, d):
    return lax.rsqrt(jnp.sum(ss_ref[...], axis=-1, keepdims=True) / d + EPS)


def _emit_norm_operand(out, g_ref, xg_ref, ss_ref, j):
    xg_ref[...] = (out * g_ref[...]).astype(xg_ref.dtype)

    @pl.when(j == 0)
    def _():
        ss_ref[...] = jnp.zeros_like(ss_ref)

    lane = lax.broadcasted_iota(jnp.int32, ss_ref.shape, 1)
    ss_ref[...] = jnp.where(lane == j, jnp.sum(out * out, axis=-1, keepdims=True), ss_ref[...])


def _norm_out(m, n, tm, tn):
    assert n // tn <= LANE
    specs = [pl.BlockSpec((tm, tn), lambda i, j, *_: (i, j)), pl.BlockSpec((tm, LANE), lambda i, j, *_: (i, 0))]
    shapes = [jax.ShapeDtypeStruct((m, n), MXU_DTYPE), jax.ShapeDtypeStruct((m, LANE), F32)]
    return specs, shapes


def _gain_spec(tn, layer):
    return pl.BlockSpec((None, 1, tn), lambda i, j, *_: (layer, 0, j))


def _norm_operand_kernel(x_ref, g_ref, xg_ref, ss_ref):
    x = x_ref[...]
    xg_ref[...] = (x * g_ref[...]).astype(xg_ref.dtype)
    lane = lax.broadcasted_iota(jnp.int32, ss_ref.shape, 1)
    ss_ref[...] = jnp.where(lane == 0, jnp.sum(x * x, axis=-1, keepdims=True), 0.0)


def _norm_operand(x, gain, layer):
    m, d = x.shape
    tr = _pick_tile(m, 512, 16)
    return pl.pallas_call(
        _norm_operand_kernel,
        grid=(m // tr,),
        in_specs=[pl.BlockSpec((tr, d), lambda i: (i, 0)),
                  pl.BlockSpec((None, 1, d), lambda i: (layer, 0, 0))],
        out_specs=[pl.BlockSpec((tr, d), lambda i: (i, 0)), pl.BlockSpec((tr, LANE), lambda i: (i, 0))],
        out_shape=[jax.ShapeDtypeStruct((m, d), MXU_DTYPE), jax.ShapeDtypeStruct((m, LANE), F32)],
        compiler_params=_params("arbitrary"),
        name="norm_operand",
    )(x, gain)


def _mm_tiles(m, n):
    return _pick_tile(m, 1100, 16), _pick_tile(n, 512, LANE)


def _lhs_spec(m, k, big):
    if big:
        tm = _pick_tile(m, 2200, 16)
        return tm, pl.BlockSpec((tm, k), lambda i, j: (i, 0), pipeline_mode=pl.Buffered(1))
    tm = _pick_tile(m, 1100, 16)
    return tm, pl.BlockSpec((tm, k), lambda i, j: (i, 0))


def _w_spec(k, tn, layer, col_blk0):
    return pl.BlockSpec((None, k, tn), lambda i, j: (layer, 0, col_blk0 + j))


def _mm_kernel(h_ref, ss_ref, w_ref, o_ref, *, w_rows):
    w = (w_ref[0] if w_rows else w_ref[...]).astype(MXU_DTYPE)
    tm, k = h_ref.shape
    nchunk = 2 if tm % 32 == 0 else 1
    for rows in [pl.ds(c * (tm // nchunk), tm // nchunk) for c in range(nchunk)]:
        h = h_ref[rows, :]
        acc = _dot_nt(h, w) if w_rows else _dot(h, w)
        r = lax.rsqrt(jnp.sum(ss_ref[rows, :], axis=-1, keepdims=True) / k + EPS)
        o_ref[rows, :] = (acc * r).astype(o_ref.dtype)


def _w_rows_spec(k, tn, layer, row0, col_map=None):
    col_map = col_map or (lambda j: j)
    return pl.BlockSpec((pl.Element(1), pl.Element(tn), pl.Element(k)),
                        lambda i, j: (layer, pl.multiple_of(row0 + col_map(j) * tn, 8), 0))


def _matmul(hn, w, layer, col0, n, out_dtype, tn=None, col_map=None, w_rows=False, big=False, name="matmul"):
    h, ss = hn
    m, k = h.shape
    tm, h_spec = _lhs_spec(m, k, big)
    tn = tn or _mm_tiles(m, n)[1]
    assert n % tn == 0
    if w_rows:
        w_spec = _w_rows_spec(k, tn, layer, col0, col_map)
    else:
        assert col0 % tn == 0
        col_map = col_map or (lambda j: j)
        w_spec = pl.BlockSpec((None, k, tn), lambda i, j: (layer, 0, col0 // tn + col_map(j)))
    return pl.pallas_call(
        functools.partial(_mm_kernel, w_rows=w_rows),
        grid=(m // tm, n // tn),
        in_specs=[h_spec, pl.BlockSpec((tm, LANE), lambda i, j: (i, 0)), w_spec],
        out_specs=pl.BlockSpec((tm, tn), lambda i, j: (i, j)),
        out_shape=jax.ShapeDtypeStruct((m, n), out_dtype),
        compiler_params=_params("arbitrary", "arbitrary"),
        name=name,
    )(h, ss, w)


def _mm_res_kernel(h_ref, w_ref, r_ref, g_ref, o_ref, xg_ref, ss_ref):
    out = r_ref[...] + _dot(h_ref[...], w_ref[...].astype(MXU_DTYPE))
    o_ref[...] = out
    _emit_norm_operand(out, g_ref, xg_ref, ss_ref, pl.program_id(1))


def _matmul_residual(h, w, layer, res, gain, glayer):
    m, k = h.shape
    n = w.shape[-1]
    tm, tn = _mm_tiles(m, n)
    nspecs, nshapes = _norm_out(m, n, tm, tn)
    x, xg, ss = pl.pallas_call(
        _mm_res_kernel,
        grid=(m // tm, n // tn),
        in_specs=[pl.BlockSpec((tm, k), lambda i, j: (i, 0)), _w_spec(k, tn, layer, 0),
                  pl.BlockSpec((tm, tn), lambda i, j: (i, j)), _gain_spec(tn, glayer)],
        out_specs=[pl.BlockSpec((tm, tn), lambda i, j: (i, j))] + nspecs,
        out_shape=[jax.ShapeDtypeStruct((m, n), F32)] + nshapes,
        compiler_params=_params("arbitrary", "arbitrary"),
        name="matmul_residual",
    )(h, w, res, gain)
    return x, (xg, ss)


def _swiglu_kernel(h_ref, ss_ref, wg_ref, wu_ref, o_ref):
    wg = wg_ref[...].astype(MXU_DTYPE)
    wu = wu_ref[...].astype(MXU_DTYPE)
    tm, k = h_ref.shape
    nchunk = 2 if tm % 32 == 0 else 1
    for rows in [pl.ds(c * (tm // nchunk), tm // nchunk) for c in range(nchunk)]:
        h = h_ref[rows, :]
        r = lax.rsqrt(jnp.sum(ss_ref[rows, :], axis=-1, keepdims=True) / k + EPS)
        g = _dot(h, wg) * r
        u = _dot(h, wu) * r
        o_ref[rows, :] = (g * _sigmoid(g) * u).astype(o_ref.dtype)


def _swiglu_in(hn, w, layer):
    h, ss = hn
    m, k = h.shape
    f = w.shape[-1] // 2
    tm, h_spec = _lhs_spec(m, k, True)
    tn = _pick_tile(f, 256, LANE)
    nf = f // tn
    return pl.pallas_call(
        _swiglu_kernel,
        grid=(m // tm, nf),
        in_specs=[h_spec, pl.BlockSpec((tm, LANE), lambda i, j: (i, 0)),
                  _w_spec(k, tn, layer, 0), _w_spec(k, tn, layer, nf)],
        out_specs=pl.BlockSpec((tm, tn), lambda i, j: (i, j)),
        out_shape=jax.ShapeDtypeStruct((m, f), MXU_DTYPE),
        compiler_params=_params("arbitrary", "arbitrary"),
        name="swiglu_in",
    )(h, ss, w, w)


def _ffn_out_kernel(a_ref, w_ref, r_ref, g_ref, o_ref, xg_ref, ss_ref, *, nk, k_valid_last, scale):
    kk = pl.program_id(2)
    tk = a_ref.shape[1]

    @pl.when(kk == 0)
    def _():
        o_ref[...] = jnp.zeros_like(o_ref)

    def accumulate(masked):
        w = w_ref[...]
        if masked:
            w = jnp.where(lax.broadcasted_iota(jnp.int32, w.shape, 0) < k_valid_last, w, 0)
        w = w.astype(MXU_DTYPE)
        tm = a_ref.shape[0]
        cm = _pick_tile(tm, FFN_OUT_CHUNK, 16)
        for c in range(tm // cm):
            rows = pl.ds(c * cm, cm)
            a = a_ref[rows, :]
            if masked:
                a = jnp.where(lax.broadcasted_iota(jnp.int32, a.shape, 1) < k_valid_last, a, 0)
            o_ref[rows, :] += _dot(a, w)

    if k_valid_last == tk:
        accumulate(False)
    else:
        pl.when(kk < nk - 1)(functools.partial(accumulate, False))
        pl.when(kk == nk - 1)(functools.partial(accumulate, True))

    @pl.when(kk == nk - 1)
    def _():
        out = r_ref[...] + scale * o_ref[...]
        o_ref[...] = out
        _emit_norm_operand(out, g_ref, xg_ref, ss_ref, pl.program_id(1))


def _ffn_out(a, w, layer, res, scale, gain, glayer):
    m, k = a.shape
    n = w.shape[-1]
    tm = _pick_tile(m, 2200, 16)
    tn = _pick_tile(n, 1024, LANE)
    tk = 1024
    nk = pl.cdiv(k, tk)
    kern = functools.partial(_ffn_out_kernel, nk=nk, k_valid_last=k - (nk - 1) * tk, scale=scale)
    nspecs, nshapes = _norm_out(m, n, tm, tn)
    x, xg, ss = pl.pallas_call(
        kern,
        grid=(m // tm, n // tn, nk),
        in_specs=[pl.BlockSpec((tm, tk), lambda i, j, kk: (i, kk)),
                  pl.BlockSpec((None, tk, tn), lambda i, j, kk: (layer, kk, j)),
                  pl.BlockSpec((tm, tn), lambda i, j, kk: (i, j), pipeline_mode=pl.Buffered(1)),
                  _gain_spec(tn, glayer)],
        out_specs=[pl.BlockSpec((tm, tn), lambda i, j, kk: (i, j))] + nspecs,
        out_shape=[jax.ShapeDtypeStruct((m, n), F32)] + nshapes,
        compiler_params=_params("arbitrary", "arbitrary", "arbitrary", vmem=VMEM_LIMIT_MAX),
        name="ffn_out",
    )(a, w, res, gain)
    return x, (xg, ss)


def _merge_kernel(og_ref, oa_ref, wg_ref, wa_ref, zg_ref, za_ref, o_ref):
    bg = _dot(og_ref[...], wg_ref[...].astype(MXU_DTYPE))
    ba = _dot(oa_ref[...], wa_ref[...].astype(MXU_DTYPE))
    o_ref[...] = (_sigmoid(zg_ref[...]) * bg + _sigmoid(za_ref[...]) * ba).astype(o_ref.dtype)


def _merge(og, oa, w_gla_out, w_att_out, layer, y, zg_off, za_off):
    m, kg = og.shape
    ka = oa.shape[1]
    n = w_gla_out.shape[-1]
    tm, tn = _mm_tiles(m, n)
    assert zg_off % tn == 0 and za_off % tn == 0
    once = pl.Buffered(1)
    return pl.pallas_call(
        _merge_kernel,
        grid=(m // tm, n // tn),
        in_specs=[pl.BlockSpec((tm, kg), lambda i, j: (i, 0), pipeline_mode=once),
                  pl.BlockSpec((tm, ka), lambda i, j: (i, 0), pipeline_mode=once),
                  _w_spec(kg, tn, layer, 0), _w_spec(ka, tn, layer, 0),
                  pl.BlockSpec((tm, tn), lambda i, j: (i, zg_off // tn + j)),
                  pl.BlockSpec((tm, tn), lambda i, j: (i, za_off // tn + j))],
        out_specs=pl.BlockSpec((tm, tn), lambda i, j: (i, j)),
        out_shape=jax.ShapeDtypeStruct((m, n), MXU_DTYPE),
        compiler_params=_params("arbitrary", "arbitrary"),
        name="merge",
    )(og, oa, w_gla_out, w_att_out, y, y)


def _ple_kernel(h_ref, ss_ref, pe_ref, wg_ref, wp_ref, r_ref, *rest, final):
    gate = _dot(h_ref[...], wg_ref[...].astype(MXU_DTYPE)) * _row_scale(ss_ref, h_ref.shape[1])
    proj = _dot(pe_ref[...].astype(MXU_DTYPE), wp_ref[...].astype(MXU_DTYPE))
    out = r_ref[...] + _sigmoid(gate) * proj
    if final:
        o_ref, tail_ref = rest
        o_ref[...] = out
        tail_ref[...] = out[out.shape[0] - DEC_ROWS:]
    else:
        g_ref, o_ref, xg_ref, ss_out_ref = rest
        o_ref[...] = out
        _emit_norm_operand(out, g_ref, xg_ref, ss_out_ref, pl.program_id(1))


def _ple(hn, pe, w_gate, w_proj, layer, res, gain=None, glayer=None, m_real=None):
    h, ss = hn
    m, k = h.shape
    kp = pe.shape[1]
    n = w_gate.shape[-1]
    tm, tn = _mm_tiles(m, n)
    final = m_real is not None
    assert not final or (m - m_real == DEC_ROWS and tm >= DEC_ROWS)
    main = pl.BlockSpec((tm, tn), lambda i, j: (i, j))
    in_specs = [pl.BlockSpec((tm, k), lambda i, j: (i, 0)), pl.BlockSpec((tm, LANE), lambda i, j: (i, 0)),
                pl.BlockSpec((tm, kp), lambda i, j: (i, 0)),
                _w_spec(k, tn, layer, 0), _w_spec(kp, tn, layer, 0),
                pl.BlockSpec((tm, tn), lambda i, j: (i, j))]
    args = [h, ss, pe, w_gate, w_proj, res]
    if final:
        out_specs = [main, pl.BlockSpec((DEC_ROWS, tn), lambda i, j: (i, j))]
        out_shape = [jax.ShapeDtypeStruct((m_real, n), F32), jax.ShapeDtypeStruct((m // tm * DEC_ROWS, n), F32)]
    else:
        nspecs, nshapes = _norm_out(m, n, tm, tn)
        in_specs.append(_gain_spec(tn, glayer))
        args.append(gain)
        out_specs = [main] + nspecs
        out_shape = [jax.ShapeDtypeStruct((m, n), F32)] + nshapes
    outs = pl.pallas_call(
        functools.partial(_ple_kernel, final=final),
        grid=(m // tm, n // tn),
        in_specs=in_specs,
        out_specs=out_specs,
        out_shape=out_shape,
        compiler_params=_params("arbitrary", "arbitrary"),
        name="ple",
    )(*args)
    return outs if final else (outs[0], (outs[1], outs[2]))


def _split_cumsum(tri, x):
    hi = x.astype(MXU_DTYPE)
    r1 = x - hi.astype(F32)
    mid = r1.astype(MXU_DTYPE)
    lo = (r1 - mid.astype(F32)).astype(MXU_DTYPE)
    return _dot(tri, hi) + _dot(tri, mid) + _dot(tri, lo)


def _gla_out_gate(o, gn, r):
    ms = jnp.mean(o * o, axis=-1, keepdims=True)
    return o * lax.rsqrt(ms + EPS) * gn * (r * _sigmoid(r))


def _gla_kernel(q_ref, k_ref, v_ref, r_ref, lr_ref, wd_ref, bd_ref, gn_ref, *rest, nc, q_scale, aliased):
    o_ref, s_ref, st_ref = rest[1:] if aliased else rest
    c = pl.program_id(2)

    @pl.when(c == 0)
    def _():
        st_ref[...] = jnp.zeros_like(st_ref)

    cs, dk = q_ref.shape
    q = q_ref[...] * q_scale
    k = k_ref[...]
    v = v_ref[...].astype(MXU_DTYPE)
    z = _dot(lr_ref[...].astype(MXU_DTYPE), wd_ref[...].astype(MXU_DTYPE)) + bd_ref[...]
    log_a = _log_sigmoid(z) / GLA_TAU
    row = lax.broadcasted_iota(jnp.int32, (cs, cs), 0)
    col = lax.broadcasted_iota(jnp.int32, (cs, cs), 1)
    b = _split_cumsum(jnp.where(row >= col, 1.0, 0.0).astype(MXU_DTYPE), log_a)

    st = st_ref[...]
    o = _dot_nt((q * jnp.exp(b)).astype(MXU_DTYPE), st.astype(MXU_DTYPE))

    sup = min(GLA_SUPER, cs)
    n_sup = cs // sup
    rowv = lax.broadcasted_iota(jnp.int32, (cs, 1), 0)
    srow = lax.broadcasted_iota(jnp.int32, (sup, sup), 0)
    scol = lax.broadcasted_iota(jnp.int32, (sup, sup), 1)
    diag = jnp.sum(q * k, axis=-1, keepdims=True)
    a_sup = [jnp.where(srow == scol, diag[i * sup:(i + 1) * sup], 0.0) for i in range(n_sup)]
    a = jnp.zeros((cs, cs), F32)
    last = b
    half = 1
    while half < cs:
        blk = 2 * half
        shift = blk.bit_length() - 1
        if half % 8 == 0:
            split = lambda x: x.reshape(cs // blk, 2, half, dk)
            b4, q4, k4 = split(b), split(q), split(k)
            rho = b4[:, 0, half - 1:half]
            zeros = jnp.zeros((cs // blk, half, dk), F32)
            qs = jnp.stack([zeros, q4[:, 1] * jnp.exp(b4[:, 1] - rho)], axis=1).reshape(cs, dk).astype(MXU_DTYPE)
            ks = jnp.stack([k4[:, 0] * jnp.exp(rho - b4[:, 0]), zeros], axis=1).reshape(cs, dk).astype(MXU_DTYPE)
        else:
            bottom = (rowv & (blk - 1)) >= half
            rho = jnp.where(bottom, pltpu.roll(last, half, 0), last)
            f = jnp.exp(jnp.where(bottom, b - rho, rho - b))
            qs = jnp.where(bottom, q * f, 0.0).astype(MXU_DTYPE)
            ks = jnp.where(bottom, 0.0, k * f).astype(MXU_DTYPE)
            last = jnp.where(bottom, last, pltpu.roll(last, cs - half, 0))
        if blk <= sup:
            same_block = (srow >> shift) == (scol >> shift)
            a_sup = [a_i + jnp.where(same_block, _dot_nt(qs[i * sup:(i + 1) * sup], ks[i * sup:(i + 1) * sup]), 0.0)
                     for i, a_i in enumerate(a_sup)]
        else:
            a = a + jnp.where((row >> shift) == (col >> shift), _dot_nt(qs, ks), 0.0)
        half = blk
    zero = jnp.zeros((sup, sup), F32)
    a = a + jnp.concatenate([jnp.concatenate([a_sup[i] if j == i else zero for j in range(n_sup)], axis=1)
                             for i in range(n_sup)], axis=0)
    o = o + _dot(a.astype(MXU_DTYPE), v)

    b_last = b[cs - 1:cs]
    kd = (k * jnp.exp(b_last - b)).astype(MXU_DTYPE)
    st_new = st * jnp.exp(b_last) + _dot_tn(v, kd)
    st_ref[...] = st_new

    o_ref[...] = _gla_out_gate(o, gn_ref[...], r_ref[...]).astype(o_ref.dtype)

    @pl.when(c == nc - 1)
    def _():
        s_ref[...] = st_new.T


def _gla_prompt(qkv, y, lr, wd, bd, gn, s_all, layer, depth, bsz, seq, dk, dv, m_pad):
    cs = min(GLA_CHUNK, seq)
    assert seq % cs == 0 and cs & (cs - 1) == 0 and cs % 8 == 0
    nc = seq // cs
    h_ = GLA_HEADS
    aliased = s_all is not None
    kern = functools.partial(_gla_kernel, nc=nc, q_scale=dk ** -0.5, aliased=aliased)
    row = lambda b, h, c: b * nc + c
    in_specs = [pl.BlockSpec((cs, dk), lambda b, h, c: (row(b, h, c), h)),
                pl.BlockSpec((cs, dk), lambda b, h, c: (row(b, h, c), h_ + h)),
                pl.BlockSpec((cs, dv), lambda b, h, c: (row(b, h, c), 2 * h_ * dk // dv + h)),
                pl.BlockSpec((cs, dv), lambda b, h, c: (row(b, h, c), h)),
                pl.BlockSpec((cs, LANE), lambda b, h, c: (row(b, h, c), 0)),
                pl.BlockSpec((None, LANE, dk), lambda b, h, c: (layer, 0, h)),
                pl.BlockSpec((None, 1, dk), lambda b, h, c: (layer, 0, h)),
                pl.BlockSpec((None, 1, dv), lambda b, h, c: (layer, 0, 0))]
    args = [qkv, qkv, qkv, y, lr, wd, bd, gn]
    if aliased:
        in_specs.append(pl.BlockSpec(memory_space=pl.ANY))
        args.append(s_all)
    return pl.pallas_call(
        kern,
        grid=(bsz, h_, nc),
        in_specs=in_specs,
        out_specs=[pl.BlockSpec((cs, dv), lambda b, h, c: (row(b, h, c), h)),
                   pl.BlockSpec((None, None, None, dk, dv), lambda b, h, c: (layer, b, h, 0, 0))],
        out_shape=[jax.ShapeDtypeStruct((m_pad, h_ * dv), MXU_DTYPE),
                   jax.ShapeDtypeStruct((depth, bsz, h_, dk, dv), F32)],
        scratch_shapes=[pltpu.VMEM((dv, dk), F32)],
        input_output_aliases={8: 1} if aliased else {},
        compiler_params=_params("arbitrary", "arbitrary", "arbitrary"),
        name="gla_prompt",
    )(*args)


def _gla_decode_kernel(q_ref, k_ref, v_ref, r_ref, lr_ref, wd_ref, bd_ref, gn_ref, s0_ref, *rest, nb, q_scale):
    o_ref, s_ref, acc_ref = rest[-3:]
    bi = pl.program_id(1)

    @pl.when(bi == 0)
    def _():
        acc_ref[...] = jnp.zeros_like(acc_ref)

    rows = q_ref.shape[0]
    rid = lax.broadcasted_iota(jnp.int32, (rows, 1), 0)

    def pick(x):
        return jnp.sum(jnp.where(rid == bi, x, 0.0), axis=0, keepdims=True)

    def column(x_row, n):
        eye = lax.broadcasted_iota(jnp.int32, (n, n), 0) == lax.broadcasted_iota(jnp.int32, (n, n), 1)
        return jnp.sum(jnp.where(eye, x_row, 0.0), axis=1, keepdims=True)

    dk = q_ref.shape[1]
    q = pick(q_ref[...]) * q_scale
    k = pick(k_ref[...])
    v = pick(v_ref[...])
    z = _dot(lr_ref[...].astype(MXU_DTYPE), wd_ref[...].astype(MXU_DTYPE)) + bd_ref[...]
    decay = jnp.exp(pick(_log_sigmoid(z) / GLA_TAU))
    s0 = s0_ref[...]
    qd = jnp.broadcast_to(q * decay, (rows, dk)).astype(MXU_DTYPE)
    o = _dot(qd, s0.astype(MXU_DTYPE))[0:1]
    o = o + jnp.sum(q * k, axis=-1, keepdims=True) * v
    s_ref[...] = s0 * column(decay, dk) + column(k, dk) * v

    gated = _gla_out_gate(o, gn_ref[...], pick(r_ref[...]))
    acc = jnp.where(rid == bi, gated, acc_ref[...])
    acc_ref[...] = acc

    @pl.when(bi == nb - 1)
    def _():
        pad = jnp.zeros((o_ref.shape[0] - rows, o_ref.shape[1]), F32)
        o_ref[...] = jnp.concatenate([acc, pad], axis=0).astype(o_ref.dtype)


def _gla_decode(qkv, y, lr, wd, bd, gn, state, og, s_all, layer, nb, dk, dv, m_real):
    h_ = GLA_HEADS
    r8 = m_real // 8
    rdec = m_real // DEC_ROWS
    kern = functools.partial(_gla_decode_kernel, nb=nb, q_scale=dk ** -0.5)
    any_spec = pl.BlockSpec(memory_space=pl.ANY)
    extra = [] if s_all is None else [s_all]
    og_new, s_new = pl.pallas_call(
        kern,
        grid=(h_, nb),
        in_specs=[pl.BlockSpec((8, dk), lambda h, b: (r8, h)),
                  pl.BlockSpec((8, dk), lambda h, b: (r8, h_ + h)),
                  pl.BlockSpec((8, dv), lambda h, b: (r8, 2 * h_ * dk // dv + h)),
                  pl.BlockSpec((8, dv), lambda h, b: (r8, h)),
                  pl.BlockSpec((8, LANE), lambda h, b: (r8, 0)),
                  pl.BlockSpec((None, LANE, dk), lambda h, b: (layer, 0, h)),
                  pl.BlockSpec((None, 1, dk), lambda h, b: (layer, 0, h)),
                  pl.BlockSpec((None, 1, dv), lambda h, b: (layer, 0, 0)),
                  pl.BlockSpec((None, None, None, dk, dv), lambda h, b: (layer, b, h, 0, 0)),
                  any_spec] + [any_spec] * len(extra),
        out_specs=[pl.BlockSpec((DEC_ROWS, dv), lambda h, b: (rdec, h)),
                   pl.BlockSpec((None, None, None, dk, dv), lambda h, b: (layer, b, h, 0, 0))],
        out_shape=[jax.ShapeDtypeStruct(og.shape, og.dtype),
                   jax.ShapeDtypeStruct(state.shape, F32)],
        scratch_shapes=[pltpu.VMEM((8, dv), F32)],
        input_output_aliases={9: 0, 10: 1} if extra else {9: 0},
        compiler_params=_params("arbitrary", "arbitrary"),
        name="gla_decode",
    )(qkv, qkv, qkv, y, lr, wd, bd, gn, state, og, *extra)
    return og_new, s_new


def _att_proj_kernel(h_ref, ss_ref, w_ref, g_ref, o_ref, *, n_norm):
    w = w_ref[0].astype(MXU_DTYPE)
    tm, k = h_ref.shape
    nh, _, hd = o_ref.shape
    normed = pl.program_id(1) < n_norm
    nchunk = 2 if tm % 32 == 0 else 1
    for rows in [pl.ds(c * (tm // nchunk), tm // nchunk) for c in range(nchunk)]:
        r = lax.rsqrt(jnp.sum(ss_ref[rows, :], axis=-1, keepdims=True) / k + EPS)
        acc = _dot_nt(h_ref[rows, :], w) * r
        for hh in range(nh):
            x = acc[:, hh * hd:(hh + 1) * hd]
            ms = jnp.mean(x * x, axis=-1, keepdims=True)
            o_ref[hh, rows, :] = x * jnp.where(normed, lax.rsqrt(ms + EPS) * g_ref[:, hh * hd:(hh + 1) * hd], 1.0)


def _att_proj(hn, w, layer, row0, gains):
    h, ss = hn
    m, k = h.shape
    n = 3 * ATT_W
    tm, h_spec = _lhs_spec(m, k, True)
    tn = _mm_tiles(m, n)[1]
    assert (2 * ATT_W) % tn == 0 and tn % ATT_HEAD_DIM == 0
    nh = tn // ATT_HEAD_DIM
    n_norm = 2 * ATT_W // tn
    return pl.pallas_call(
        functools.partial(_att_proj_kernel, n_norm=n_norm),
        grid=(m // tm, n // tn),
        in_specs=[h_spec, pl.BlockSpec((tm, LANE), lambda i, j: (i, 0)), _w_rows_spec(k, tn, layer, row0),
                  pl.BlockSpec((None, 1, tn), lambda i, j: (layer, 0, jnp.minimum(j, n_norm - 1)))],
        out_specs=pl.BlockSpec((nh, tm, ATT_HEAD_DIM), lambda i, j: (j, i, 0)),
        out_shape=jax.ShapeDtypeStruct((n // ATT_HEAD_DIM, m, ATT_HEAD_DIM), F32),
        compiler_params=_params("arbitrary", "arbitrary"),
        name="in_proj_att",
    )(h, ss, w, gains)


def _softmax_mix(os, ls):
    m = functools.reduce(jnp.maximum, ls)
    ws = [jnp.exp(l - m) for l in ls]
    den = functools.reduce(lambda a, b: a + b, ws)
    return functools.reduce(lambda a, b: a + b, [(w / den) * o for w, o in zip(ws, os)])


def _attn_kernel(slope_ref, q0, q1, q2, k0, k1, k2, v0, v1, v2, *rest):
    oa_ref, c0, c1, c2, o_s, l_s = rest[-6:]
    j = pl.program_id(1)
    seq, hd = q0.shape
    tq = ATT_BLOCK
    row = lax.broadcasted_iota(jnp.int32, (tq, 2 * tq), 0)
    col = lax.broadcasted_iota(jnp.int32, (tq, 2 * tq), 1)
    steps = tq + row - col
    in_window = (steps >= 0) & (steps <= ATT_BLOCK)
    for g, (_, dil) in enumerate(ATT_GROUPS):
        q_ref, k_ref, v_ref = (q0, q1, q2)[g], (k0, k1, k2)[g], (v0, v1, v2)[g]
        nq = seq // (tq * dil)
        bias = slope_ref[g, j] * (steps * dil).astype(F32)

        def block(it, carry, q_ref=q_ref, k_ref=k_ref, v_ref=v_ref, g=g, dil=dil, nq=nq, bias=bias):
            r = it // nq
            i = it - r * nq
            start = r + i * (tq * dil)
            pstart = jnp.where(i > 0, start - tq * dil, start)

            def rows(st):
                return pl.ds(st, tq, stride=dil) if dil > 1 else pl.ds(pl.multiple_of(st, tq), tq)

            qh = q_ref[rows(start), :].astype(MXU_DTYPE)
            kk = jnp.concatenate([k_ref[rows(pstart), :], k_ref[rows(start), :]], axis=0).astype(MXU_DTYPE)
            vv = jnp.concatenate([v_ref[rows(pstart), :], v_ref[rows(start), :]], axis=0).astype(MXU_DTYPE)
            s = _dot_nt(qh, kk) * (hd ** -0.5)
            valid = in_window & ((col >= tq) | (i > 0))
            s = jnp.where(valid, s - bias, NEG)
            m = jnp.max(s, axis=-1, keepdims=True)
            p = jnp.exp(s - m)
            l = jnp.sum(p, axis=-1, keepdims=True)
            o_s[g, rows(start), :] = _dot(p.astype(MXU_DTYPE), vv) / l
            l_s[g, rows(start), :] = jnp.broadcast_to(m + jnp.log(l), (tq, hd))
            return carry

        lax.fori_loop(0, seq // tq, block, 0, unroll=8)
        c_ref = (c0, c1, c2)[g]
        win = c_ref.shape[1]
        c_ref[0] = k_ref[seq - win:, :]
        c_ref[1] = v_ref[seq - win:, :]
    n_g = len(ATT_GROUPS)
    oa_ref[...] = _softmax_mix([o_s[g] for g in range(n_g)], [l_s[g] for g in range(n_g)]).astype(oa_ref.dtype)


def _attn_prompt(att, slopes, bufs, layer, depth, bsz, seq, m_pad):
    for win, dil in ATT_GROUPS:
        assert win // dil == ATT_BLOCK and seq % (dil * ATT_BLOCK) == 0 and win <= seq
    hd = ATT_HEAD_DIM
    n_g = len(ATT_GROUPS)
    specs = [pl.BlockSpec((None, seq, hd), lambda b, j, h0=kind * ATT_HEADS + g * ATT_GROUP_HEADS: (h0 + j, b, 0))
             for kind in range(3) for g in range(n_g)]
    bufs = list(bufs or [])
    outs = pl.pallas_call(
        _attn_kernel,
        grid=(bsz, ATT_GROUP_HEADS),
        in_specs=[pl.BlockSpec(memory_space=pltpu.SMEM)] + specs + [pl.BlockSpec(memory_space=pl.ANY)] * len(bufs),
        out_specs=[pl.BlockSpec((seq, hd), lambda b, j: (b, j))]
        + [pl.BlockSpec((None, 2, None, None, win, hd), lambda b, j: (layer, 0, j, b, 0, 0)) for win, _ in ATT_GROUPS],
        out_shape=[jax.ShapeDtypeStruct((m_pad, ATT_GW), MXU_DTYPE)]
        + [jax.ShapeDtypeStruct((depth, 2, ATT_GROUP_HEADS, bsz, win, hd), F32) for win, _ in ATT_GROUPS],
        scratch_shapes=[pltpu.VMEM((n_g, seq, hd), F32), pltpu.VMEM((n_g, seq, hd), F32)],
        input_output_aliases={10 + g: 1 + g for g in range(len(bufs))},
        compiler_params=_params("arbitrary", "arbitrary"),
        name="attn_prompt",
    )(slopes, *([att] * 9), *bufs)
    return outs[0], outs[1:]


def _attn_decode_kernel(slope_ref, a_ref, c0_ref, c1_ref, c2_ref, o_ref):
    nh = ATT_GROUP_HEADS
    hd = ATT_HEAD_DIM
    nrow = lax.broadcasted_iota(jnp.int32, (ATT_BLOCK, 1, 1), 0)
    os, ls = [], []
    for g, (_, dil) in enumerate(ATT_GROUPS):
        c_ref = (c0_ref, c1_ref, c2_ref)[g]
        q = a_ref[g * nh:(g + 1) * nh]
        kn = a_ref[ATT_HEADS + g * nh:ATT_HEADS + (g + 1) * nh]
        vn = a_ref[2 * ATT_HEADS + g * nh:2 * ATT_HEADS + (g + 1) * nh]
        kb = c_ref[:, 0]
        vb = c_ref[:, 1]
        slope = slope_ref[g][:, 0:1]
        dist = ((ATT_BLOCK - nrow) * dil).astype(F32)
        sb = jnp.sum(kb * q, axis=-1, keepdims=True) * (hd ** -0.5) - slope * dist
        sn = jnp.sum(kn * q, axis=-1, keepdims=True) * (hd ** -0.5)
        m = jnp.maximum(jnp.max(sb, axis=0), sn)
        pb = jnp.exp(sb - m)
        pn = jnp.exp(sn - m)
        l = jnp.sum(pb, axis=0) + pn
        os.append((jnp.sum(pb * vb, axis=0) + pn * vn) / l)
        ls.append(m + jnp.log(l))
    o_ref[...] = _softmax_mix(os, ls)


def _attn_decode(a_dec, slopes_v, caches, layer):
    nb = a_dec.shape[0]
    views, specs = [], []
    for (win, dil), cbuf in zip(ATT_GROUPS, caches):
        depth, nb_, wb = cbuf.shape[:3]
        assert wb == win and nb_ == nb
        views.append(cbuf.reshape(depth, nb, win // dil, dil, 2, ATT_GROUP_HEADS, ATT_HEAD_DIM))
        specs.append(pl.BlockSpec((None, None, win // dil, None, 2, ATT_GROUP_HEADS, ATT_HEAD_DIM),
                                  lambda b: (layer, b, 0, 0, 0, 0, 0)))
    return pl.pallas_call(
        _attn_decode_kernel,
        grid=(nb,),
        in_specs=[pl.BlockSpec(slopes_v.shape, lambda b: (0, 0, 0)),
                  pl.BlockSpec((None,) + a_dec.shape[1:], lambda b: (b, 0, 0))] + specs,
        out_specs=pl.BlockSpec((None, ATT_GROUP_HEADS, ATT_HEAD_DIM), lambda b: (b, 0, 0)),
        out_shape=jax.ShapeDtypeStruct((nb, ATT_GROUP_HEADS, ATT_HEAD_DIM), F32),
        compiler_params=_params("arbitrary"),
        name="attn_decode",
    )(slopes_v, a_dec, *views)


def _place_rows_kernel(x_ref, dst_ref, o_ref):
    del dst_ref
    pad = jnp.zeros((o_ref.shape[0] - x_ref.shape[0], o_ref.shape[1]), F32)
    o_ref[...] = jnp.concatenate([x_ref[...], pad], axis=0).astype(o_ref.dtype)


def _place_sample_rows(x, dst, m_real):
    n = dst.shape[1]
    return pl.pallas_call(
        _place_rows_kernel,
        grid=(1,),
        in_specs=[pl.BlockSpec(x.shape, lambda i: (0, 0)), pl.BlockSpec(memory_space=pl.ANY)],
        out_specs=pl.BlockSpec((DEC_ROWS, n), lambda i: (m_real // DEC_ROWS, 0)),
        out_shape=jax.ShapeDtypeStruct(dst.shape, dst.dtype),
        input_output_aliases={1: 0},
        compiler_params=_params("arbitrary"),
        name="place_sample_rows",
    )(x, dst)


def kernel(x_prompt, x_sample, state_gla, cache_w128, cache_w512, cache_w2048, p_prompt, p_sample, norm_ffn1, ffn1_w_in, ffn1_w_out, norm_mix, w_in, gla_w_decay, gla_b_decay, gla_norm, att_q_norm, att_k_norm, w_gla_out, w_att_out, w_out, norm_ffn2, ffn2_w_in, ffn2_w_out, norm_ple, w_ple_gate, w_ple_proj):
    bsz, seq, d = x_prompt.shape
    nb, dec_seq, _ = x_sample.shape
    depth = norm_ffn1.shape[0]
    assert dec_seq == 1 and nb == 8
    m_real = bsz * seq
    assert m_real % DEC_ROWS == 0
    m_pad = m_real + DEC_ROWS
    gla_qk = gla_w_decay.shape[-1]
    dk = gla_qk // GLA_HEADS
    gla_v = d
    dv = gla_v // GLA_HEADS
    caches = (cache_w128, cache_w512, cache_w2048)
    n_g = len(ATT_GROUPS)
    nh, hd = ATT_GROUP_HEADS, ATT_HEAD_DIM

    lr_off = 2 * gla_qk + gla_v
    r_off = lr_off + GLA_RANK
    q_off = r_off + gla_v
    z_off = q_off + 3 * ATT_W
    assert w_in.shape[-1] == z_off + 2 * d and lr_off % LANE == 0 and r_off % 8 == 0
    w_in_t = jnp.swapaxes(w_in, 1, 2)

    zrow = lambda n, w: jnp.zeros((n, w), F32)
    x = jnp.concatenate([x_prompt.reshape(m_real, d), x_sample.reshape(nb, d), zrow(DEC_ROWS - nb, d)], axis=0)
    pdim = p_prompt.shape[-1]
    pe = jnp.concatenate([p_prompt.reshape(depth, m_real, pdim), p_sample.reshape(depth, nb, pdim),
                          jnp.zeros((depth, DEC_ROWS - nb, pdim), F32)], axis=1)

    wd = jnp.zeros((depth, LANE, gla_qk), F32).at[:, :GLA_RANK].set(gla_w_decay)
    bd = gla_b_decay.reshape(depth, 1, gla_qk)
    gn = gla_norm.reshape(depth, 1, dv)
    att_gain = jnp.concatenate([jnp.tile(att_q_norm, (1, ATT_HEADS)), jnp.tile(att_k_norm, (1, ATT_HEADS))],
                               axis=1).reshape(depth, 1, 2 * ATT_W)
    slopes = jnp.exp2(-8.0 * jnp.arange(1, ATT_HEADS + 1, dtype=F32) / ATT_HEADS)
    slopes_s = slopes.reshape(n_g, nh)
    slopes_v = jnp.broadcast_to(slopes.reshape(n_g, nh, 1), (n_g, nh, LANE))
    g3 = lambda a: a.reshape(depth, 1, d)
    n_f1, n_mix, n_f2, n_ple = g3(norm_ffn1), g3(norm_mix), g3(norm_ffn2), g3(norm_ple)

    _, tn_g = _mm_tiles(m_pad, gla_v + 2 * d)
    n_r, n_att = gla_v // tn_g, 3 * ATT_W // tn_g
    gate_cols = lambda j: jnp.where(j < n_r, j, j + n_att)

    gla_p, gla_s, win_p, new_rows = None, None, None, []
    hn = _norm_operand(x, n_f1, 0)
    for i in range(depth):
        x, h = _ffn_out(_swiglu_in(hn, ffn1_w_in, i), ffn1_w_out, i, x, 0.5, n_mix, i)
        qkv = _matmul(h, w_in_t, i, 0, lr_off, F32, w_rows=True, big=True, name="in_proj_gla")
        lr = _matmul(h, w_in_t, i, lr_off, LANE, F32, tn=LANE, w_rows=True, name="in_proj_decay")
        gates = _matmul(h, w_in_t, i, r_off, gla_v + 2 * d, F32, col_map=gate_cols, w_rows=True, big=True,
                        name="in_proj_gates")
        att = _att_proj(h, w_in_t, i, q_off, att_gain)

        og, gla_p = _gla_prompt(qkv, gates, lr, wd, bd, gn, gla_p, i, depth, bsz, seq, dk, dv, m_pad)
        og, gla_s = _gla_decode(qkv, gates, lr, wd, bd, gn, state_gla, og, gla_s, i, nb, dk, dv, m_real)

        oa, win_p = _attn_prompt(att, slopes_s, win_p, i, depth, bsz, seq, m_pad)
        a_dec = jnp.transpose(att[:, m_real:m_real + nb], (1, 0, 2))
        o_dec = _attn_decode(a_dec, slopes_v, caches, i)
        oa = _place_sample_rows(o_dec.reshape(nb, ATT_GW), oa, m_real)

        merged = _merge(og, oa, w_gla_out, w_att_out, i, gates, gla_v, gla_v + d)
        x, hn = _matmul_residual(merged, w_out, i, x, n_f2, i)
        x, hn = _ffn_out(_swiglu_in(hn, ffn2_w_in, i), ffn2_w_out, i, x, 0.5, n_ple, i)
        if i == depth - 1:
            x = _ple(hn, pe[i], w_ple_gate, w_ple_proj, i, x, m_real=m_real)
        else:
            x, hn = _ple(hn, pe[i], w_ple_gate, w_ple_proj, i, x, n_f1, i + 1)

        new_rows.append(a_dec.reshape(nb, 3, n_g, nh, hd)[:, 1:])

    y_prompt = x[0].reshape(bsz, seq, d)
    y_sample = x[1][x[1].shape[0] - DEC_ROWS:][:nb].reshape(nb, 1, d)
    new_rows = jnp.stack(new_rows)
    shift_cfg = [(0, 0, 0), (0, 0, 0), (-1, 1, 0), (0, 0, 0), (0, 0, 0), (0, 0, 0)]
    win_s = [lax.dynamic_update_slice(lax.pad(caches[g], jnp.zeros((), F32), shift_cfg), new_rows[:, :, None, :, g],
                                      (0, 0, caches[g].shape[2] - 1, 0, 0, 0)) for g in range(n_g)]
    win_p = [jnp.transpose(w, (0, 3, 4, 1, 2, 5)) for w in win_p]
    return (y_prompt, y_sample, gla_p, win_p[0], win_p[1], win_p[2], gla_s, win_s[0], win_s[1], win_s[2])
```

```python
import functools

import jax
import jax.numpy as jnp
from jax import lax
from jax.experimental import pallas as pl
from jax.experimental.pallas import tpu as pltpu

F32 = jnp.float32
MXU_DTYPE = jnp.bfloat16

EPS = 1e-6
GLA_HEADS = 4
GLA_RANK = 16
GLA_TAU = 16.0
GLA_CHUNK = 256
GLA_SUPER = 128
ATT_GROUPS = ((128, 1), (512, 4), (2048, 16))
ATT_GROUP_HEADS = 8
ATT_HEAD_DIM = 128
ATT_HEADS = len(ATT_GROUPS) * ATT_GROUP_HEADS
ATT_GW = ATT_GROUP_HEADS * ATT_HEAD_DIM
ATT_W = ATT_HEADS * ATT_HEAD_DIM
ATT_BLOCK = 128
DEC_ROWS = 128
LANE = 128
NEG = -1e30

VMEM_LIMIT = 56 * 1024 * 1024


VMEM_LIMIT_MAX =61 * 1024 * 1024


def _params(*sem, vmem=VMEM_LIMIT):
    return pltpu.CompilerParams(dimension_semantics=sem, vmem_limit_bytes=vmem)


def _pick_tile(n, target, mult):
    best = None
    for t in range(mult, min(n, target) + 1, mult):
        if n % t == 0:
            best = t
    assert best is not None, (n, target, mult)
    return best


_sigmoid = jax.nn.sigmoid


def _log_sigmoid(x):
    return jnp.minimum(x, 0.0) - jnp.log(1.0 + jnp.exp(-jnp.abs(x)))


def _dot(a, b):
    return jnp.dot(a, b, preferred_element_type=F32)


def _dot_nt(a, b):
    return lax.dot_general(a, b, (((1,), (1,)), ((), ())), preferred_element_type=F32)


def _dot_tn(a, b):
    return lax.dot_general(a, b, (((0,), (0,)), ((), ())), preferred_element_type=F32)


def _row_scale(ss_ref, d):
    return lax.rsqrt(jnp.sum(ss_ref[...], axis=-1, keepdims=True) / d + EPS)


def _emit_norm_operand(out, g_ref, xg_ref, ss_ref, j):
    xg_ref[...] = (out * g_ref[...]).astype(xg_ref.dtype)

    @pl.when(j == 0)
    def _():
        ss_ref[...] = jnp.zeros_like(ss_ref)

    lane = lax.broadcasted_iota(jnp.int32, ss_ref.shape, 1)
    ss_ref[...] = jnp.where(lane == j, jnp.sum(out * out, axis=-1, keepdims=True), ss_ref[...])


def _norm_out(m, n, tm, tn):
    assert n // tn <= LANE
    specs = [pl.BlockSpec((tm, tn), lambda i, j, *_: (i, j)), pl.BlockSpec((tm, LANE), lambda i, j, *_: (i, 0))]
    shapes = [jax.ShapeDtypeStruct((m, n), MXU_DTYPE), jax.ShapeDtypeStruct((m, LANE), F32)]
    return specs, shapes


def _gain_spec(tn, layer):
    return pl.BlockSpec((None, 1, tn), lambda i, j, *_: (layer, 0, j))


def _norm_operand_kernel(x_ref, g_ref, xg_ref, ss_ref):
    x = x_ref[...]
    xg_ref[...] = (x * g_ref[...]).astype(xg_ref.dtype)
    lane = lax.broadcasted_iota(jnp.int32, ss_ref.shape, 1)
    ss_ref[...] = jnp.where(lane == 0, jnp.sum(x * x, axis=-1, keepdims=True), 0.0)


def _norm_operand(x, gain, layer):
    m, d = x.shape
    tr = _pick_tile(m, 512, 16)
    return pl.pallas_call(
        _norm_operand_kernel,
        grid=(m // tr,),
        in_specs=[pl.BlockSpec((tr, d), lambda i: (i, 0)),
                  pl.BlockSpec((None, 1, d), lambda i: (layer, 0, 0))],
        out_specs=[pl.BlockSpec((tr, d), lambda i: (i, 0)), pl.BlockSpec((tr, LANE), lambda i: (i, 0))],
        out_shape=[jax.ShapeDtypeStruct((m, d), MXU_DTYPE), jax.ShapeDtypeStruct((m, LANE), F32)],
        compiler_params=_params("arbitrary"),
        name="norm_operand",
    )(x, gain)


def _mm_tiles(m, n):
    return _pick_tile(m, 1100, 16), _pick_tile(n, 512, LANE)


def _lhs_spec(m, k, big):
    if big:
        tm = _pick_tile(m, 2200, 16)
        return tm, pl.BlockSpec((tm, k), lambda i, j: (i, 0), pipeline_mode=pl.Buffered(1))
    tm = _pick_tile(m, 1100, 16)
    return tm, pl.BlockSpec((tm, k), lambda i, j: (i, 0))


def _w_spec(k, tn, layer, col_blk0):
    return pl.BlockSpec((None, k, tn), lambda i, j: (layer, 0, col_blk0 + j))


def _mm_kernel(h_ref, ss_ref, w_ref, o_ref, *, w_rows):
    w = (w_ref[0] if w_rows else w_ref[...]).astype(MXU_DTYPE)
    tm, k = h_ref.shape
    nchunk = 2 if tm % 32 == 0 else 1
    for rows in [pl.ds(c * (tm // nchunk), tm // nchunk) for c in range(nchunk)]:
        h = h_ref[rows, :]
        acc = _dot_nt(h, w) if w_rows else _dot(h, w)
        r = lax.rsqrt(jnp.sum(ss_ref[rows, :], axis=-1, keepdims=True) / k + EPS)
        o_ref[rows, :] = (acc * r).astype(o_ref.dtype)


def _w_rows_spec(k, tn, layer, row0, col_map=None):
    col_map = col_map or (lambda j: j)
    return pl.BlockSpec((pl.Element(1), pl.Element(tn), pl.Element(k)),
                        lambda i, j: (layer, pl.multiple_of(row0 + col_map(j) * tn, 8), 0))


def _matmul(hn, w, layer, col0, n, out_dtype, tn=None, col_map=None, w_rows=False, big=False, name="matmul"):
    h, ss = hn
    m, k = h.shape
    tm, h_spec = _lhs_spec(m, k, big)
    tn = tn or _mm_tiles(m, n)[1]
    assert n % tn == 0
    if w_rows:
        w_spec = _w_rows_spec(k, tn, layer, col0, col_map)
    else:
        assert col0 % tn == 0
        col_map = col_map or (lambda j: j)
        w_spec = pl.BlockSpec((None, k, tn), lambda i, j: (layer, 0, col0 // tn + col_map(j)))
    return pl.pallas_call(
        functools.partial(_mm_kernel, w_rows=w_rows),
        grid=(m // tm, n // tn),
        in_specs=[h_spec, pl.BlockSpec((tm, LANE), lambda i, j: (i, 0)), w_spec],
        out_specs=pl.BlockSpec((tm, tn), lambda i, j: (i, j)),
        out_shape=jax.ShapeDtypeStruct((m, n), out_dtype),
        compiler_params=_params("arbitrary", "arbitrary"),
        name=name,
    )(h, ss, w)


def _mm_res_kernel(h_ref, w_ref, r_ref, g_ref, o_ref, xg_ref, ss_ref, *, scale):
    acc = _dot(h_ref[...], w_ref[...].astype(MXU_DTYPE))
    out = r_ref[...] + (acc if scale is None else scale * acc)
    o_ref[...] = out
    _emit_norm_operand(out, g_ref, xg_ref, ss_ref, pl.program_id(1))


def _matmul_residual(h, w, layer, res, gain, glayer, scale=None, tn=None, lhs_once=False, vmem=VMEM_LIMIT,
                     name="matmul_residual"):
    m, k = h.shape
    n = w.shape[-1]
    tm, tn_ = _mm_tiles(m, n)
    tn = tn or tn_
    nspecs, nshapes = _norm_out(m, n, tm, tn)
    h_spec = pl.BlockSpec((tm, k), lambda i, j: (i, 0), **({"pipeline_mode": pl.Buffered(1)} if lhs_once else {}))
    x, xg, ss = pl.pallas_call(
        functools.partial(_mm_res_kernel, scale=scale),
        grid=(m // tm, n // tn),
        in_specs=[h_spec, _w_spec(k, tn, layer, 0),
                  pl.BlockSpec((tm, tn), lambda i, j: (i, j)), _gain_spec(tn, glayer)],
        out_specs=[pl.BlockSpec((tm, tn), lambda i, j: (i, j))] + nspecs,
        out_shape=[jax.ShapeDtypeStruct((m, n), F32)] + nshapes,
        compiler_params=_params("arbitrary", "arbitrary", vmem=vmem),
        name=name,
    )(h, w, res, gain)
    return x, (xg, ss)


def _swiglu_kernel(h_ref, ss_ref, wg_ref, wu_ref, o_ref):
    wg = wg_ref[...].astype(MXU_DTYPE)
    wu = wu_ref[...].astype(MXU_DTYPE)
    tm, k = h_ref.shape
    nchunk = 2 if tm % 32 == 0 else 1
    for rows in [pl.ds(c * (tm // nchunk), tm // nchunk) for c in range(nchunk)]:
        h = h_ref[rows, :]
        r = lax.rsqrt(jnp.sum(ss_ref[rows, :], axis=-1, keepdims=True) / k + EPS)
        g = _dot(h, wg) * r
        u = _dot(h, wu) * r
        o_ref[rows, :] = (g * _sigmoid(g) * u).astype(o_ref.dtype)


def _swiglu_in(hn, w, layer):
    h, ss = hn
    m, k = h.shape
    f = w.shape[-1] // 2
    tm, h_spec = _lhs_spec(m, k, True)
    tn = _pick_tile(f, 256, LANE)
    nf = f // tn
    return pl.pallas_call(
        _swiglu_kernel,
        grid=(m // tm, nf),
        in_specs=[h_spec, pl.BlockSpec((tm, LANE), lambda i, j: (i, 0)),
                  _w_spec(k, tn, layer, 0), _w_spec(k, tn, layer, nf)],
        out_specs=pl.BlockSpec((tm, tn), lambda i, j: (i, j)),
        out_shape=jax.ShapeDtypeStruct((m, f), MXU_DTYPE),
        compiler_params=_params("arbitrary", "arbitrary"),
        name="swiglu_in",
    )(h, ss, w, w)


def _ffn_out(a, w, layer, res, gain, glayer):
    return _matmul_residual(a, w, layer, res, gain, glayer, scale=0.5, tn=_pick_tile(w.shape[-1], 256, LANE),
                            lhs_once=True, vmem=VMEM_LIMIT_MAX, name="ffn_out")


def _merge_kernel(og_ref, oa_ref, wg_ref, wa_ref, zg_ref, za_ref, o_ref):
    bg = _dot(og_ref[...], wg_ref[...].astype(MXU_DTYPE))
    ba = _dot(oa_ref[...], wa_ref[...].astype(MXU_DTYPE))
    o_ref[...] = (_sigmoid(zg_ref[...]) * bg + _sigmoid(za_ref[...]) * ba).astype(o_ref.dtype)


def _merge(og, oa, w_gla_out, w_att_out, layer, y, zg_off, za_off):
    m, kg = og.shape
    ka = oa.shape[1]
    n = w_gla_out.shape[-1]
    tm, tn = _mm_tiles(m, n)
    assert zg_off % tn == 0 and za_off % tn == 0
    once = pl.Buffered(1)
    return pl.pallas_call(
        _merge_kernel,
        grid=(m // tm, n // tn),
        in_specs=[pl.BlockSpec((tm, kg), lambda i, j: (i, 0), pipeline_mode=once),
                  pl.BlockSpec((tm, ka), lambda i, j: (i, 0), pipeline_mode=once),
                  _w_spec(kg, tn, layer, 0), _w_spec(ka, tn, layer, 0),
                  pl.BlockSpec((tm, tn), lambda i, j: (i, zg_off // tn + j)),
                  pl.BlockSpec((tm, tn), lambda i, j: (i, za_off // tn + j))],
        out_specs=pl.BlockSpec((tm, tn), lambda i, j: (i, j)),
        out_shape=jax.ShapeDtypeStruct((m, n), MXU_DTYPE),
        compiler_params=_params("arbitrary", "arbitrary"),
        name="merge",
    )(og, oa, w_gla_out, w_att_out, y, y)


def _ple_kernel(h_ref, ss_ref, pe_ref, wg_ref, wp_ref, r_ref, *rest, final):
    gate = _dot(h_ref[...], wg_ref[...].astype(MXU_DTYPE)) * _row_scale(ss_ref, h_ref.shape[1])
    proj = _dot(pe_ref[...].astype(MXU_DTYPE), wp_ref[...].astype(MXU_DTYPE))
    out = r_ref[...] + _sigmoid(gate) * proj
    if final:
        o_ref, tail_ref = rest
        o_ref[...] = out
        tail_ref[...] = out[out.shape[0] - DEC_ROWS:]
    else:
        g_ref, o_ref, xg_ref, ss_out_ref = rest
        o_ref[...] = out
        _emit_norm_operand(out, g_ref, xg_ref, ss_out_ref, pl.program_id(1))


def _ple(hn, pe, w_gate, w_proj, layer, res, gain=None, glayer=None, m_real=None):
    h, ss = hn
    m, k = h.shape
    kp = pe.shape[1]
    n = w_gate.shape[-1]
    tm, tn = _mm_tiles(m, n)
    final = m_real is not None
    assert not final or (m - m_real == DEC_ROWS and tm >= DEC_ROWS)
    main = pl.BlockSpec((tm, tn), lambda i, j: (i, j))
    in_specs = [pl.BlockSpec((tm, k), lambda i, j: (i, 0)), pl.BlockSpec((tm, LANE), lambda i, j: (i, 0)),
                pl.BlockSpec((tm, kp), lambda i, j: (i, 0)),
                _w_spec(k, tn, layer, 0), _w_spec(kp, tn, layer, 0),
                pl.BlockSpec((tm, tn), lambda i, j: (i, j))]
    args = [h, ss, pe, w_gate, w_proj, res]
    if final:
        out_specs = [main, pl.BlockSpec((DEC_ROWS, tn), lambda i, j: (i, j))]
        out_shape = [jax.ShapeDtypeStruct((m_real, n), F32), jax.ShapeDtypeStruct((m // tm * DEC_ROWS, n), F32)]
    else:
        nspecs, nshapes = _norm_out(m, n, tm, tn)
        in_specs.append(_gain_spec(tn, glayer))
        args.append(gain)
        out_specs = [main] + nspecs
        out_shape = [jax.ShapeDtypeStruct((m, n), F32)] + nshapes
    outs = pl.pallas_call(
        functools.partial(_ple_kernel, final=final),
        grid=(m // tm, n // tn),
        in_specs=in_specs,
        out_specs=out_specs,
        out_shape=out_shape,
        compiler_params=_params("arbitrary", "arbitrary"),
        name="ple",
    )(*args)
    return outs if final else (outs[0], (outs[1], outs[2]))


def _split_cumsum(tri, x):
    hi = x.astype(MXU_DTYPE)
    r1 = x - hi.astype(F32)
    mid = r1.astype(MXU_DTYPE)
    lo = (r1 - mid.astype(F32)).astype(MXU_DTYPE)
    return _dot(tri, hi) + _dot(tri, mid) + _dot(tri, lo)


def _gla_out_gate(o, gn, r):
    ms = jnp.mean(o * o, axis=-1, keepdims=True)
    return o * lax.rsqrt(ms + EPS) * gn * (r * _sigmoid(r))


def _gla_kernel(q_ref, k_ref, v_ref, r_ref, lr_ref, wd_ref, bd_ref, gn_ref, *rest, nc, q_scale, aliased):
    o_ref, s_ref, st_ref = rest[1:] if aliased else rest
    c = pl.program_id(2)

    @pl.when(c == 0)
    def _():
        st_ref[...] = jnp.zeros_like(st_ref)

    cs, dk = q_ref.shape
    q = q_ref[...] * q_scale
    k = k_ref[...]
    v = v_ref[...].astype(MXU_DTYPE)
    z = _dot(lr_ref[...].astype(MXU_DTYPE), wd_ref[...].astype(MXU_DTYPE)) + bd_ref[...]
    log_a = _log_sigmoid(z) / GLA_TAU
    row = lax.broadcasted_iota(jnp.int32, (cs, cs), 0)
    col = lax.broadcasted_iota(jnp.int32, (cs, cs), 1)
    b = _split_cumsum(jnp.where(row >= col, 1.0, 0.0).astype(MXU_DTYPE), log_a)

    st = st_ref[...]
    o = _dot_nt((q * jnp.exp(b)).astype(MXU_DTYPE), st.astype(MXU_DTYPE))

    sup = min(GLA_SUPER, cs)
    n_sup = cs // sup
    rowv = lax.broadcasted_iota(jnp.int32, (cs, 1), 0)
    srow = lax.broadcasted_iota(jnp.int32, (sup, sup), 0)
    scol = lax.broadcasted_iota(jnp.int32, (sup, sup), 1)
    diag = jnp.sum(q * k, axis=-1, keepdims=True)
    a_sup = [jnp.where(srow == scol, diag[i * sup:(i + 1) * sup], 0.0) for i in range(n_sup)]
    a = jnp.zeros((cs, cs), F32)
    last = b
    half = 1
    while half < cs:
        blk = 2 * half
        shift = blk.bit_length() - 1
        if half % 8 == 0:
            split = lambda x: x.reshape(cs // blk, 2, half, dk)
            b4, q4, k4 = split(b), split(q), split(k)
            rho = b4[:, 0, half - 1:half]
            zeros = jnp.zeros((cs // blk, half, dk), F32)
            qs = jnp.stack([zeros, q4[:, 1] * jnp.exp(b4[:, 1] - rho)], axis=1).reshape(cs, dk).astype(MXU_DTYPE)
            ks = jnp.stack([k4[:, 0] * jnp.exp(rho - b4[:, 0]), zeros], axis=1).reshape(cs, dk).astype(MXU_DTYPE)
        else:
            bottom = (rowv & (blk - 1)) >= half
            rho = jnp.where(bottom, pltpu.roll(last, half, 0), last)
            f = jnp.exp(jnp.where(bottom, b - rho, rho - b))
            qs = jnp.where(bottom, q * f, 0.0).astype(MXU_DTYPE)
            ks = jnp.where(bottom, 0.0, k * f).astype(MXU_DTYPE)
            last = jnp.where(bottom, last, pltpu.roll(last, cs - half, 0))
        if blk <= sup:
            same_block = (srow >> shift) == (scol >> shift)
            a_sup = [a_i + jnp.where(same_block, _dot_nt(qs[i * sup:(i + 1) * sup], ks[i * sup:(i + 1) * sup]), 0.0)
                     for i, a_i in enumerate(a_sup)]
        else:
            a = a + jnp.where((row >> shift) == (col >> shift), _dot_nt(qs, ks), 0.0)
        half = blk
    zero = jnp.zeros((sup, sup), F32)
    a = a + jnp.concatenate([jnp.concatenate([a_sup[i] if j == i else zero for j in range(n_sup)], axis=1)
                             for i in range(n_sup)], axis=0)
    o = o + _dot(a.astype(MXU_DTYPE), v)

    b_last = b[cs - 1:cs]
    kd = (k * jnp.exp(b_last - b)).astype(MXU_DTYPE)
    st_new = st * jnp.exp(b_last) + _dot_tn(v, kd)
    st_ref[...] = st_new

    o_ref[...] = _gla_out_gate(o, gn_ref[...], r_ref[...]).astype(o_ref.dtype)

    @pl.when(c == nc - 1)
    def _():
        s_ref[...] = st_new.T


def _gla_prompt(qkv, y, lr, wd, bd, gn, s_all, layer, depth, bsz, seq, dk, dv, m_pad):
    cs = min(GLA_CHUNK, seq)
    assert seq % cs == 0 and cs & (cs - 1) == 0 and cs % 8 == 0
    nc = seq // cs
    h_ = GLA_HEADS
    aliased = s_all is not None
    kern = functools.partial(_gla_kernel, nc=nc, q_scale=dk ** -0.5, aliased=aliased)
    row = lambda b, h, c: b * nc + c
    in_specs = [pl.BlockSpec((cs, dk), lambda b, h, c: (row(b, h, c), h)),
                pl.BlockSpec((cs, dk), lambda b, h, c: (row(b, h, c), h_ + h)),
                pl.BlockSpec((cs, dv), lambda b, h, c: (row(b, h, c), 2 * h_ * dk // dv + h)),
                pl.BlockSpec((cs, dv), lambda b, h, c: (row(b, h, c), h)),
                pl.BlockSpec((cs, LANE), lambda b, h, c: (row(b, h, c), 0)),
                pl.BlockSpec((None, LANE, dk), lambda b, h, c: (layer, 0, h)),
                pl.BlockSpec((None, 1, dk), lambda b, h, c: (layer, 0, h)),
                pl.BlockSpec((None, 1, dv), lambda b, h, c: (layer, 0, 0))]
    args = [qkv, qkv, qkv, y, lr, wd, bd, gn]
    if aliased:
        in_specs.append(pl.BlockSpec(memory_space=pl.ANY))
        args.append(s_all)
    return pl.pallas_call(
        kern,
        grid=(bsz, h_, nc),
        in_specs=in_specs,
        out_specs=[pl.BlockSpec((cs, dv), lambda b, h, c: (row(b, h, c), h)),
                   pl.BlockSpec((None, None, None, dk, dv), lambda b, h, c: (layer, b, h, 0, 0))],
        out_shape=[jax.ShapeDtypeStruct((m_pad, h_ * dv), MXU_DTYPE),
                   jax.ShapeDtypeStruct((depth, bsz, h_, dk, dv), F32)],
        scratch_shapes=[pltpu.VMEM((dv, dk), F32)],
        input_output_aliases={8: 1} if aliased else {},
        compiler_params=_params("arbitrary", "arbitrary", "arbitrary"),
        name="gla_prompt",
    )(*args)


def _gla_decode_kernel(q_ref, k_ref, v_ref, r_ref, lr_ref, wd_ref, bd_ref, gn_ref, s0_ref, *rest, nb, q_scale):
    o_ref, s_ref, acc_ref = rest[-3:]
    bi = pl.program_id(1)

    @pl.when(bi == 0)
    def _():
        acc_ref[...] = jnp.zeros_like(acc_ref)

    rows = q_ref.shape[0]
    rid = lax.broadcasted_iota(jnp.int32, (rows, 1), 0)

    def pick(x):
        return jnp.sum(jnp.where(rid == bi, x, 0.0), axis=0, keepdims=True)

    def column(x_row, n):
        eye = lax.broadcasted_iota(jnp.int32, (n, n), 0) == lax.broadcasted_iota(jnp.int32, (n, n), 1)
        return jnp.sum(jnp.where(eye, x_row, 0.0), axis=1, keepdims=True)

    dk = q_ref.shape[1]
    q = pick(q_ref[...]) * q_scale
    k = pick(k_ref[...])
    v = pick(v_ref[...])
    z = _dot(lr_ref[...].astype(MXU_DTYPE), wd_ref[...].astype(MXU_DTYPE)) + bd_ref[...]
    decay = jnp.exp(pick(_log_sigmoid(z) / GLA_TAU))
    s0 = s0_ref[...]
    qd = jnp.broadcast_to(q * decay, (rows, dk)).astype(MXU_DTYPE)
    o = _dot(qd, s0.astype(MXU_DTYPE))[0:1]
    o = o + jnp.sum(q * k, axis=-1, keepdims=True) * v
    s_ref[...] = s0 * column(decay, dk) + column(k, dk) * v

    gated = _gla_out_gate(o, gn_ref[...], pick(r_ref[...]))
    acc = jnp.where(rid == bi, gated, acc_ref[...])
    acc_ref[...] = acc

    @pl.when(bi == nb - 1)
    def _():
        pad = jnp.zeros((o_ref.shape[0] - rows, o_ref.shape[1]), F32)
        o_ref[...] = jnp.concatenate([acc, pad], axis=0).astype(o_ref.dtype)


def _gla_decode(qkv, y, lr, wd, bd, gn, state, og, s_all, layer, nb, dk, dv, m_real):
    h_ = GLA_HEADS
    r8 = m_real // 8
    rdec = m_real // DEC_ROWS
    kern = functools.partial(_gla_decode_kernel, nb=nb, q_scale=dk ** -0.5)
    any_spec = pl.BlockSpec(memory_space=pl.ANY)
    extra = [] if s_all is None else [s_all]
    og_new, s_new = pl.pallas_call(
        kern,
        grid=(h_, nb),
        in_specs=[pl.BlockSpec((8, dk), lambda h, b: (r8, h)),
                  pl.BlockSpec((8, dk), lambda h, b: (r8, h_ + h)),
                  pl.BlockSpec((8, dv), lambda h, b: (r8, 2 * h_ * dk // dv + h)),
                  pl.BlockSpec((8, dv), lambda h, b: (r8, h)),
                  pl.BlockSpec((8, LANE), lambda h, b: (r8, 0)),
                  pl.BlockSpec((None, LANE, dk), lambda h, b: (layer, 0, h)),
                  pl.BlockSpec((None, 1, dk), lambda h, b: (layer, 0, h)),
                  pl.BlockSpec((None, 1, dv), lambda h, b: (layer, 0, 0)),
                  pl.BlockSpec((None, None, None, dk, dv), lambda h, b: (layer, b, h, 0, 0)),
                  any_spec] + [any_spec] * len(extra),
        out_specs=[pl.BlockSpec((DEC_ROWS, dv), lambda h, b: (rdec, h)),
                   pl.BlockSpec((None, None, None, dk, dv), lambda h, b: (layer, b, h, 0, 0))],
        out_shape=[jax.ShapeDtypeStruct(og.shape, og.dtype),
                   jax.ShapeDtypeStruct(state.shape, F32)],
        scratch_shapes=[pltpu.VMEM((8, dv), F32)],
        input_output_aliases={9: 0, 10: 1} if extra else {9: 0},
        compiler_params=_params("arbitrary", "arbitrary"),
        name="gla_decode",
    )(qkv, qkv, qkv, y, lr, wd, bd, gn, state, og, *extra)
    return og_new, s_new


def _att_proj_kernel(h_ref, ss_ref, w_ref, g_ref, o_ref, *, n_norm):
    w = w_ref[0].astype(MXU_DTYPE)
    tm, k = h_ref.shape
    nh, _, hd = o_ref.shape
    normed = pl.program_id(1) < n_norm
    nchunk = 2 if tm % 32 == 0 else 1
    for rows in [pl.ds(c * (tm // nchunk), tm // nchunk) for c in range(nchunk)]:
        r = lax.rsqrt(jnp.sum(ss_ref[rows, :], axis=-1, keepdims=True) / k + EPS)
        acc = _dot_nt(h_ref[rows, :], w) * r
        for hh in range(nh):
            x = acc[:, hh * hd:(hh + 1) * hd]
            ms = jnp.mean(x * x, axis=-1, keepdims=True)
            o_ref[hh, rows, :] = x * jnp.where(normed, lax.rsqrt(ms + EPS) * g_ref[:, hh * hd:(hh + 1) * hd], 1.0)


def _att_proj(hn, w, layer, row0, gains):
    h, ss = hn
    m, k = h.shape
    n = 3 * ATT_W
    tm, h_spec = _lhs_spec(m, k, True)
    tn = _mm_tiles(m, n)[1]
    assert (2 * ATT_W) % tn == 0 and tn % ATT_HEAD_DIM == 0
    nh = tn // ATT_HEAD_DIM
    n_norm = 2 * ATT_W // tn
    return pl.pallas_call(
        functools.partial(_att_proj_kernel, n_norm=n_norm),
        grid=(m // tm, n // tn),
        in_specs=[h_spec, pl.BlockSpec((tm, LANE), lambda i, j: (i, 0)), _w_rows_spec(k, tn, layer, row0),
                  pl.BlockSpec((None, 1, tn), lambda i, j: (layer, 0, jnp.minimum(j, n_norm - 1)))],
        out_specs=pl.BlockSpec((nh, tm, ATT_HEAD_DIM), lambda i, j: (j, i, 0)),
        out_shape=jax.ShapeDtypeStruct((n // ATT_HEAD_DIM, m, ATT_HEAD_DIM), F32),
        compiler_params=_params("arbitrary", "arbitrary"),
        name="in_proj_att",
    )(h, ss, w, gains)


def _softmax_mix(os, ls):
    m = functools.reduce(jnp.maximum, ls)
    ws = [jnp.exp(l - m) for l in ls]
    den = functools.reduce(lambda a, b: a + b, ws)
    return functools.reduce(lambda a, b: a + b, [(w / den) * o for w, o in zip(ws, os)])


def _attn_kernel(slope_ref, q0, q1, q2, k0, k1, k2, v0, v1, v2, *rest):
    oa_ref, c0, c1, c2, o_s, l_s = rest[-6:]
    j = pl.program_id(1)
    seq, hd = q0.shape
    tq = ATT_BLOCK
    row = lax.broadcasted_iota(jnp.int32, (tq, 2 * tq), 0)
    col = lax.broadcasted_iota(jnp.int32, (tq, 2 * tq), 1)
    steps = tq + row - col
    in_window = (steps >= 0) & (steps <= ATT_BLOCK)
    for g, (_, dil) in enumerate(ATT_GROUPS):
        q_ref, k_ref, v_ref = (q0, q1, q2)[g], (k0, k1, k2)[g], (v0, v1, v2)[g]
        nq = seq // (tq * dil)
        bias = slope_ref[g, j] * (steps * dil).astype(F32)

        def block(it, carry, q_ref=q_ref, k_ref=k_ref, v_ref=v_ref, g=g, dil=dil, nq=nq, bias=bias):
            r = it // nq
            i = it - r * nq
            start = r + i * (tq * dil)
            pstart = jnp.where(i > 0, start - tq * dil, start)

            def rows(st):
                return pl.ds(st, tq, stride=dil) if dil > 1 else pl.ds(pl.multiple_of(st, tq), tq)

            qh = q_ref[rows(start), :].astype(MXU_DTYPE)
            kk = jnp.concatenate([k_ref[rows(pstart), :], k_ref[rows(start), :]], axis=0).astype(MXU_DTYPE)
            vv = jnp.concatenate([v_ref[rows(pstart), :], v_ref[rows(start), :]], axis=0).astype(MXU_DTYPE)
            s = _dot_nt(qh, kk) * (hd ** -0.5)
            valid = in_window & ((col >= tq) | (i > 0))
            s = jnp.where(valid, s - bias, NEG)
            m = jnp.max(s, axis=-1, keepdims=True)
            p = jnp.exp(s - m)
            l = jnp.sum(p, axis=-1, keepdims=True)
            o_s[g, rows(start), :] = _dot(p.astype(MXU_DTYPE), vv) / l
            l_s[g, rows(start), :] = jnp.broadcast_to(m + jnp.log(l), (tq, hd))
            return carry

        lax.fori_loop(0, seq // tq, block, 0, unroll=8)
        c_ref = (c0, c1, c2)[g]
        win = c_ref.shape[1]
        c_ref[0] = k_ref[seq - win:, :]
        c_ref[1] = v_ref[seq - win:, :]
    n_g = len(ATT_GROUPS)
    oa_ref[...] = _softmax_mix([o_s[g] for g in range(n_g)], [l_s[g] for g in range(n_g)]).astype(oa_ref.dtype)


def _attn_prompt(att, slopes, bufs, layer, depth, bsz, seq, m_pad):
    for win, dil in ATT_GROUPS:
        assert win // dil == ATT_BLOCK and seq % (dil * ATT_BLOCK) == 0 and win <= seq
    hd = ATT_HEAD_DIM
    n_g = len(ATT_GROUPS)
    specs = [pl.BlockSpec((None, seq, hd), lambda b, j, h0=kind * ATT_HEADS + g * ATT_GROUP_HEADS: (h0 + j, b, 0))
             for kind in range(3) for g in range(n_g)]
    bufs = list(bufs or [])
    outs = pl.pallas_call(
        _attn_kernel,
        grid=(bsz, ATT_GROUP_HEADS),
        in_specs=[pl.BlockSpec(memory_space=pltpu.SMEM)] + specs + [pl.BlockSpec(memory_space=pl.ANY)] * len(bufs),
        out_specs=[pl.BlockSpec((seq, hd), lambda b, j: (b, j))]
        + [pl.BlockSpec((None, 2, None, None, win, hd), lambda b, j: (layer, 0, j, b, 0, 0)) for win, _ in ATT_GROUPS],
        out_shape=[jax.ShapeDtypeStruct((m_pad, ATT_GW), MXU_DTYPE)]
        + [jax.ShapeDtypeStruct((depth, 2, ATT_GROUP_HEADS, bsz, win, hd), F32) for win, _ in ATT_GROUPS],
        scratch_shapes=[pltpu.VMEM((n_g, seq, hd), F32), pltpu.VMEM((n_g, seq, hd), F32)],
        input_output_aliases={10 + g: 1 + g for g in range(len(bufs))},
        compiler_params=_params("arbitrary", "arbitrary"),
        name="attn_prompt",
    )(slopes, *([att] * 9), *bufs)
    return outs[0], outs[1:]


def _attn_decode_kernel(slope_ref, a_ref, c0_ref, c1_ref, c2_ref, o_ref):
    nh = ATT_GROUP_HEADS
    hd = ATT_HEAD_DIM
    nrow = lax.broadcasted_iota(jnp.int32, (ATT_BLOCK, 1, 1), 0)
    os, ls = [], []
    for g, (_, dil) in enumerate(ATT_GROUPS):
        c_ref = (c0_ref, c1_ref, c2_ref)[g]
        q = a_ref[g * nh:(g + 1) * nh]
        kn = a_ref[ATT_HEADS + g * nh:ATT_HEADS + (g + 1) * nh]
        vn = a_ref[2 * ATT_HEADS + g * nh:2 * ATT_HEADS + (g + 1) * nh]
        kb = c_ref[:, 0]
        vb = c_ref[:, 1]
        slope = slope_ref[g][:, 0:1]
        dist = ((ATT_BLOCK - nrow) * dil).astype(F32)
        sb = jnp.sum(kb * q, axis=-1, keepdims=True) * (hd ** -0.5) - slope * dist
        sn = jnp.sum(kn * q, axis=-1, keepdims=True) * (hd ** -0.5)
        m = jnp.maximum(jnp.max(sb, axis=0), sn)
        pb = jnp.exp(sb - m)
        pn = jnp.exp(sn - m)
        l = jnp.sum(pb, axis=0) + pn
        os.append((jnp.sum(pb * vb, axis=0) + pn * vn) / l)
        ls.append(m + jnp.log(l))
    o_ref[...] = _softmax_mix(os, ls)


def _attn_decode(a_dec, slopes_v, caches, layer):
    nb = a_dec.shape[0]
    views, specs = [], []
    for (win, dil), cbuf in zip(ATT_GROUPS, caches):
        depth, nb_, wb = cbuf.shape[:3]
        assert wb == win and nb_ == nb
        views.append(cbuf.reshape(depth, nb, win // dil, dil, 2, ATT_GROUP_HEADS, ATT_HEAD_DIM))
        specs.append(pl.BlockSpec((None, None, win // dil, None, 2, ATT_GROUP_HEADS, ATT_HEAD_DIM),
                                  lambda b: (layer, b, 0, 0, 0, 0, 0)))
    return pl.pallas_call(
        _attn_decode_kernel,
        grid=(nb,),
        in_specs=[pl.BlockSpec(slopes_v.shape, lambda b: (0, 0, 0)),
                  pl.BlockSpec((None,) + a_dec.shape[1:], lambda b: (b, 0, 0))] + specs,
        out_specs=pl.BlockSpec((None, ATT_GROUP_HEADS, ATT_HEAD_DIM), lambda b: (b, 0, 0)),
        out_shape=jax.ShapeDtypeStruct((nb, ATT_GROUP_HEADS, ATT_HEAD_DIM), F32),
        compiler_params=_params("arbitrary"),
        name="attn_decode",
    )(slopes_v, a_dec, *views)


def _place_rows_kernel(x_ref, dst_ref, o_ref):
    del dst_ref
    pad = jnp.zeros((o_ref.shape[0] - x_ref.shape[0], o_ref.shape[1]), F32)
    o_ref[...] = jnp.concatenate([x_ref[...], pad], axis=0).astype(o_ref.dtype)


def _place_sample_rows(x, dst, m_real):
    n = dst.shape[1]
    return pl.pallas_call(
        _place_rows_kernel,
        grid=(1,),
        in_specs=[pl.BlockSpec(x.shape, lambda i: (0, 0)), pl.BlockSpec(memory_space=pl.ANY)],
        out_specs=pl.BlockSpec((DEC_ROWS, n), lambda i: (m_real // DEC_ROWS, 0)),
        out_shape=jax.ShapeDtypeStruct(dst.shape, dst.dtype),
        input_output_aliases={1: 0},
        compiler_params=_params("arbitrary"),
        name="place_sample_rows",
    )(x, dst)


def kernel(x_prompt, x_sample, state_gla, cache_w128, cache_w512, cache_w2048, p_prompt, p_sample, norm_ffn1, ffn1_w_in, ffn1_w_out, norm_mix, w_in, gla_w_decay, gla_b_decay, gla_norm, att_q_norm, att_k_norm, w_gla_out, w_att_out, w_out, norm_ffn2, ffn2_w_in, ffn2_w_out, norm_ple, w_ple_gate, w_ple_proj):
    bsz, seq, d = x_prompt.shape
    nb, dec_seq, _ = x_sample.shape
    depth = norm_ffn1.shape[0]
    assert dec_seq == 1 and nb == 8
    m_real = bsz * seq
    assert m_real % DEC_ROWS == 0
    m_pad = m_real + DEC_ROWS
    gla_qk = gla_w_decay.shape[-1]
    dk = gla_qk // GLA_HEADS
    gla_v = d
    dv = gla_v // GLA_HEADS
    caches = (cache_w128, cache_w512, cache_w2048)
    n_g = len(ATT_GROUPS)
    nh, hd = ATT_GROUP_HEADS, ATT_HEAD_DIM

    lr_off = 2 * gla_qk + gla_v
    r_off = lr_off + GLA_RANK
    q_off = r_off + gla_v
    z_off = q_off + 3 * ATT_W
    assert w_in.shape[-1] == z_off + 2 * d and lr_off % LANE == 0 and r_off % 8 == 0
    w_in_t = jnp.swapaxes(w_in, 1, 2)

    zrow = lambda n, w: jnp.zeros((n, w), F32)
    x = jnp.concatenate([x_prompt.reshape(m_real, d), x_sample.reshape(nb, d), zrow(DEC_ROWS - nb, d)], axis=0)
    pdim = p_prompt.shape[-1]
    pe = jnp.concatenate([p_prompt.reshape(depth, m_real, pdim), p_sample.reshape(depth, nb, pdim),
                          jnp.zeros((depth, DEC_ROWS - nb, pdim), F32)], axis=1)

    wd = jnp.zeros((depth, LANE, gla_qk), F32).at[:, :GLA_RANK].set(gla_w_decay)
    bd = gla_b_decay.reshape(depth, 1, gla_qk)
    gn = gla_norm.reshape(depth, 1, dv)
    att_gain = jnp.concatenate([jnp.tile(att_q_norm, (1, ATT_HEADS)), jnp.tile(att_k_norm, (1, ATT_HEADS))],
                               axis=1).reshape(depth, 1, 2 * ATT_W)
    slopes = jnp.exp2(-8.0 * jnp.arange(1, ATT_HEADS + 1, dtype=F32) / ATT_HEADS)
    slopes_s = slopes.reshape(n_g, nh)
    slopes_v = jnp.broadcast_to(slopes.reshape(n_g, nh, 1), (n_g, nh, LANE))
    g3 = lambda a: a.reshape(depth, 1, d)
    n_f1, n_mix, n_f2, n_ple = g3(norm_ffn1), g3(norm_mix), g3(norm_ffn2), g3(norm_ple)

    _, tn_g = _mm_tiles(m_pad, gla_v + 2 * d)
    n_r, n_att = gla_v // tn_g, 3 * ATT_W // tn_g
    gate_cols = lambda j: jnp.where(j < n_r, j, j + n_att)

    gla_p, gla_s, win_p, new_rows = None, None, None, []
    hn = _norm_operand(x, n_f1, 0)
    for i in range(depth):
        x, h = _ffn_out(_swiglu_in(hn, ffn1_w_in, i), ffn1_w_out, i, x, n_mix, i)
        qkv = _matmul(h, w_in_t, i, 0, lr_off, F32, w_rows=True, big=True, name="in_proj_gla")
        lr = _matmul(h, w_in_t, i, lr_off, LANE, F32, tn=LANE, w_rows=True, name="in_proj_decay")
        gates = _matmul(h, w_in_t, i, r_off, gla_v + 2 * d, F32, col_map=gate_cols, w_rows=True, big=True,
                        name="in_proj_gates")
        att = _att_proj(h, w_in_t, i, q_off, att_gain)

        og, gla_p = _gla_prompt(qkv, gates, lr, wd, bd, gn, gla_p, i, depth, bsz, seq, dk, dv, m_pad)
        og, gla_s = _gla_decode(qkv, gates, lr, wd, bd, gn, state_gla, og, gla_s, i, nb, dk, dv, m_real)

        oa, win_p = _attn_prompt(att, slopes_s, win_p, i, depth, bsz, seq, m_pad)
        a_dec = jnp.transpose(att[:, m_real:m_real + nb], (1, 0, 2))
        o_dec = _attn_decode(a_dec, slopes_v, caches, i)
        oa = _place_sample_rows(o_dec.reshape(nb, ATT_GW), oa, m_real)

        merged = _merge(og, oa, w_gla_out, w_att_out, i, gates, gla_v, gla_v + d)
        x, hn = _matmul_residual(merged, w_out, i, x, n_f2, i)
        x, hn = _ffn_out(_swiglu_in(hn, ffn2_w_in, i), ffn2_w_out, i, x, n_ple, i)
        if i == depth - 1:
            x = _ple(hn, pe[i], w_ple_gate, w_ple_proj, i, x, m_real=m_real)
        else:
            x, hn = _ple(hn, pe[i], w_ple_gate, w_ple_proj, i, x, n_f1, i + 1)

        new_rows.append(a_dec.reshape(nb, 3, n_g, nh, hd)[:, 1:])

    y_prompt = x[0].reshape(bsz, seq, d)
    y_sample = x[1][x[1].shape[0] - DEC_ROWS:][:nb].reshape(nb, 1, d)
    new_rows = jnp.stack(new_rows)
    shift_cfg = [(0, 0, 0), (0, 0, 0), (-1, 1, 0), (0, 0, 0), (0, 0, 0), (0, 0, 0)]
    win_s = [lax.dynamic_update_slice(lax.pad(caches[g], jnp.zeros((), F32), shift_cfg), new_rows[:, :, None, :, g],
                                      (0, 0, caches[g].shape[2] - 1, 0, 0, 0)) for g in range(n_g)]
    win_p = [jnp.transpose(w, (0, 3, 4, 1, 2, 5)) for w in win_p]
    return (y_prompt, y_sample, gla_p, win_p[0], win_p[1], win_p[2], gla_s, win_s[0], win_s[1], win_s[2])
```

```python
import functools

import jax
import jax.numpy as jnp
from jax import lax
from jax.experimental import pallas as pl
from jax.experimental.pallas import tpu as pltpu

F32 = jnp.float32
MXU_DTYPE = jnp.bfloat16

EPS = 1e-6
GLA_HEADS = 4
GLA_RANK = 16
GLA_TAU = 16.0
GLA_CHUNK = 256
GLA_SUPER = 128
ATT_GROUPS = ((128, 1), (512, 4), (2048, 16))
ATT_GROUP_HEADS = 8
ATT_HEAD_DIM = 128
ATT_HEADS = len(ATT_GROUPS) * ATT_GROUP_HEADS
ATT_GW = ATT_GROUP_HEADS * ATT_HEAD_DIM
ATT_W = ATT_HEADS * ATT_HEAD_DIM
ATT_BLOCK = 128
DEC_ROWS = 128
LANE = 128
NEG = -1e30

VMEM_LIMIT = 56 * 1024 * 1024


VMEM_LIMIT_MAX =61 * 1024 * 1024


def _params(*sem, vmem=VMEM_LIMIT):
    return pltpu.CompilerParams(dimension_semantics=sem, vmem_limit_bytes=vmem)


def _pick_tile(n, target, mult):
    best = None
    for t in range(mult, min(n, target) + 1, mult):
        if n % t == 0:
            best = t
    assert best is not None, (n, target, mult)
    return best


_sigmoid = jax.nn.sigmoid


def _log_sigmoid(x):
    return jnp.minimum(x, 0.0) - jnp.log(1.0 + jnp.exp(-jnp.abs(x)))


def _dot(a, b):
    return jnp.dot(a, b, preferred_element_type=F32)


def _dot_nt(a, b):
    return lax.dot_general(a, b, (((1,), (1,)), ((), ())), preferred_element_type=F32)


def _dot_tn(a, b):
    return lax.dot_general(a, b, (((0,), (0,)), ((), ())), preferred_element_type=F32)


def _row_scale(ss_ref, d):
    return lax.rsqrt(jnp.sum(ss_ref[...], axis=-1, keepdims=True) / d + EPS)


def _emit_norm_operand(out, g_ref, xg_ref, ss_ref, j):
    xg_ref[...] = (out * g_ref[...]).astype(xg_ref.dtype)

    @pl.when(j == 0)
    def _():
        ss_ref[...] = jnp.zeros_like(ss_ref)

    lane = lax.broadcasted_iota(jnp.int32, ss_ref.shape, 1)
    ss_ref[...] = jnp.where(lane == j, jnp.sum(out * out, axis=-1, keepdims=True), ss_ref[...])


def _norm_out(m, n, tm, tn):
    assert n // tn <= LANE
    specs = [pl.BlockSpec((tm, tn), lambda i, j, *_: (i, j)), pl.BlockSpec((tm, LANE), lambda i, j, *_: (i, 0))]
    shapes = [jax.ShapeDtypeStruct((m, n), MXU_DTYPE), jax.ShapeDtypeStruct((m, LANE), F32)]
    return specs, shapes


def _gain_spec(tn, layer):
    return pl.BlockSpec((None, 1, tn), lambda i, j, *_: (layer, 0, j))


def _norm_operand_kernel(x_ref, g_ref, xg_ref, ss_ref):
    x = x_ref[...]
    xg_ref[...] = (x * g_ref[...]).astype(xg_ref.dtype)
    lane = lax.broadcasted_iota(jnp.int32, ss_ref.shape, 1)
    ss_ref[...] = jnp.where(lane == 0, jnp.sum(x * x, axis=-1, keepdims=True), 0.0)


def _norm_operand(x, gain, layer):
    m, d = x.shape
    tr = _pick_tile(m, 512, 16)
    return pl.pallas_call(
        _norm_operand_kernel,
        grid=(m // tr,),
        in_specs=[pl.BlockSpec((tr, d), lambda i: (i, 0)),
                  pl.BlockSpec((None, 1, d), lambda i: (layer, 0, 0))],
        out_specs=[pl.BlockSpec((tr, d), lambda i: (i, 0)), pl.BlockSpec((tr, LANE), lambda i: (i, 0))],
        out_shape=[jax.ShapeDtypeStruct((m, d), MXU_DTYPE), jax.ShapeDtypeStruct((m, LANE), F32)],
        compiler_params=_params("arbitrary"),
        name="norm_operand",
    )(x, gain)


def _mm_tiles(m, n):
    return _pick_tile(m, 1100, 16), _pick_tile(n, 512, LANE)


def _lhs_spec(m, k, big):
    if big:
        tm = _pick_tile(m, 2200, 16)
        return tm, pl.BlockSpec((tm, k), lambda i, j: (i, 0), pipeline_mode=pl.Buffered(1))
    tm = _pick_tile(m, 1100, 16)
    return tm, pl.BlockSpec((tm, k), lambda i, j: (i, 0))


def _w_spec(k, tn, layer, col_blk0):
    return pl.BlockSpec((None, k, tn), lambda i, j: (layer, 0, col_blk0 + j))


def _mm_kernel(h_ref, ss_ref, w_ref, o_ref, *, w_rows):
    w = (w_ref[0] if w_rows else w_ref[...]).astype(MXU_DTYPE)
    tm, k = h_ref.shape
    nchunk = 2 if tm % 32 == 0 else 1
    for rows in [pl.ds(c * (tm // nchunk), tm // nchunk) for c in range(nchunk)]:
        h = h_ref[rows, :]
        acc = _dot_nt(h, w) if w_rows else _dot(h, w)
        r = lax.rsqrt(jnp.sum(ss_ref[rows, :], axis=-1, keepdims=True) / k + EPS)
        o_ref[rows, :] = (acc * r).astype(o_ref.dtype)


def _w_rows_spec(k, tn, layer, row0, col_map=None):
    col_map = col_map or (lambda j: j)
    return pl.BlockSpec((pl.Element(1), pl.Element(tn), pl.Element(k)),
                        lambda i, j: (layer, pl.multiple_of(row0 + col_map(j) * tn, 8), 0))


def _matmul(hn, w, layer, col0, n, out_dtype, tn=None, col_map=None, w_rows=False, big=False, name="matmul"):
    h, ss = hn
    m, k = h.shape
    tm, h_spec = _lhs_spec(m, k, big)
    tn = tn or _mm_tiles(m, n)[1]
    assert n % tn == 0
    if w_rows:
        w_spec = _w_rows_spec(k, tn, layer, col0, col_map)
    else:
        assert col0 % tn == 0
        col_map = col_map or (lambda j: j)
        w_spec = pl.BlockSpec((None, k, tn), lambda i, j: (layer, 0, col0 // tn + col_map(j)))
    return pl.pallas_call(
        functools.partial(_mm_kernel, w_rows=w_rows),
        grid=(m // tm, n // tn),
        in_specs=[h_spec, pl.BlockSpec((tm, LANE), lambda i, j: (i, 0)), w_spec],
        out_specs=pl.BlockSpec((tm, tn), lambda i, j: (i, j)),
        out_shape=jax.ShapeDtypeStruct((m, n), out_dtype),
        compiler_params=_params("arbitrary", "arbitrary"),
        name=name,
    )(h, ss, w)


def _mm_res_kernel(h_ref, w_ref, r_ref, g_ref, o_ref, xg_ref, ss_ref, *, scale):
    acc = _dot(h_ref[...], w_ref[...].astype(MXU_DTYPE))
    out = r_ref[...] + (acc if scale is None else scale * acc)
    o_ref[...] = out
    _emit_norm_operand(out, g_ref, xg_ref, ss_ref, pl.program_id(1))


def _matmul_residual(h, w, layer, res, gain, glayer, scale=None, tn=None, lhs_once=False, vmem=VMEM_LIMIT,
                     name="matmul_residual"):
    m, k = h.shape
    n = w.shape[-1]
    tm, tn_ = _mm_tiles(m, n)
    tn = tn or tn_
    nspecs, nshapes = _norm_out(m, n, tm, tn)
    h_spec = pl.BlockSpec((tm, k), lambda i, j: (i, 0), **({"pipeline_mode": pl.Buffered(1)} if lhs_once else {}))
    x, xg, ss = pl.pallas_call(
        functools.partial(_mm_res_kernel, scale=scale),
        grid=(m // tm, n // tn),
        in_specs=[h_spec, _w_spec(k, tn, layer, 0),
                  pl.BlockSpec((tm, tn), lambda i, j: (i, j)), _gain_spec(tn, glayer)],
        out_specs=[pl.BlockSpec((tm, tn), lambda i, j: (i, j))] + nspecs,
        out_shape=[jax.ShapeDtypeStruct((m, n), F32)] + nshapes,
        compiler_params=_params("arbitrary", "arbitrary", vmem=vmem),
        name=name,
    )(h, w, res, gain)
    return x, (xg, ss)


def _swiglu_kernel(h_ref, ss_ref, wg_ref, wu_ref, o_ref):
    wg = wg_ref[...].astype(MXU_DTYPE)
    wu = wu_ref[...].astype(MXU_DTYPE)
    tm, k = h_ref.shape
    nchunk = 5 if tm % 80 == 0 else (2 if tm % 32 == 0 else 1)
    for rows in [pl.ds(c * (tm // nchunk), tm // nchunk) for c in range(nchunk)]:
        h = h_ref[rows, :]
        r = lax.rsqrt(jnp.sum(ss_ref[rows, :], axis=-1, keepdims=True) / k + EPS)
        g = _dot(h, wg) * r
        u = _dot(h, wu) * r
        o_ref[rows, :] = (g * _sigmoid(g) * u).astype(o_ref.dtype)


def _swiglu_in(hn, w, layer):
    h, ss = hn
    m, k = h.shape
    f = w.shape[-1] // 2
    tm, h_spec = _lhs_spec(m, k, True)
    tn = _pick_tile(f, 256, LANE)
    nf = f // tn
    return pl.pallas_call(
        _swiglu_kernel,
        grid=(m // tm, nf),
        in_specs=[h_spec, pl.BlockSpec((tm, LANE), lambda i, j: (i, 0)),
                  _w_spec(k, tn, layer, 0), _w_spec(k, tn, layer, nf)],
        out_specs=pl.BlockSpec((tm, tn), lambda i, j: (i, j)),
        out_shape=jax.ShapeDtypeStruct((m, f), MXU_DTYPE),
        compiler_params=_params("arbitrary", "arbitrary"),
        name="swiglu_in",
    )(h, ss, w, w)


def _ffn_out(a, w, layer, res, gain, glayer):
    return _matmul_residual(a, w, layer, res, gain, glayer, scale=0.5, tn=_pick_tile(w.shape[-1], 256, LANE),
                            lhs_once=True, vmem=VMEM_LIMIT_MAX, name="ffn_out")


def _merge_kernel(og_ref, oa_ref, wg_ref, wa_ref, zg_ref, za_ref, o_ref):
    bg = _dot(og_ref[...], wg_ref[...].astype(MXU_DTYPE))
    ba = _dot(oa_ref[...], wa_ref[...].astype(MXU_DTYPE))
    o_ref[...] = (_sigmoid(zg_ref[...]) * bg + _sigmoid(za_ref[...]) * ba).astype(o_ref.dtype)


def _merge(og, oa, w_gla_out, w_att_out, layer, y, zg_off, za_off):
    m, kg = og.shape
    ka = oa.shape[1]
    n = w_gla_out.shape[-1]
    tm, tn = _mm_tiles(m, n)
    assert zg_off % tn == 0 and za_off % tn == 0
    once = pl.Buffered(1)
    return pl.pallas_call(
        _merge_kernel,
        grid=(m // tm, n // tn),
        in_specs=[pl.BlockSpec((tm, kg), lambda i, j: (i, 0), pipeline_mode=once),
                  pl.BlockSpec((tm, ka), lambda i, j: (i, 0), pipeline_mode=once),
                  _w_spec(kg, tn, layer, 0), _w_spec(ka, tn, layer, 0),
                  pl.BlockSpec((tm, tn), lambda i, j: (i, zg_off // tn + j)),
                  pl.BlockSpec((tm, tn), lambda i, j: (i, za_off // tn + j))],
        out_specs=pl.BlockSpec((tm, tn), lambda i, j: (i, j)),
        out_shape=jax.ShapeDtypeStruct((m, n), MXU_DTYPE),
        compiler_params=_params("arbitrary", "arbitrary"),
        name="merge",
    )(og, oa, w_gla_out, w_att_out, y, y)


def _ple_kernel(h_ref, ss_ref, pe_ref, wg_ref, wp_ref, r_ref, *rest, final):
    gate = _dot(h_ref[...], wg_ref[...].astype(MXU_DTYPE)) * _row_scale(ss_ref, h_ref.shape[1])
    proj = _dot(pe_ref[...].astype(MXU_DTYPE), wp_ref[...].astype(MXU_DTYPE))
    out = r_ref[...] + _sigmoid(gate) * proj
    if final:
        o_ref, tail_ref = rest
        o_ref[...] = out
        tail_ref[...] = out[out.shape[0] - DEC_ROWS:]
    else:
        g_ref, o_ref, xg_ref, ss_out_ref = rest
        o_ref[...] = out
        _emit_norm_operand(out, g_ref, xg_ref, ss_out_ref, pl.program_id(1))


def _ple(hn, pe, w_gate, w_proj, layer, res, gain=None, glayer=None, m_real=None):
    h, ss = hn
    m, k = h.shape
    kp = pe.shape[1]
    n = w_gate.shape[-1]
    tm, tn = _mm_tiles(m, n)
    final = m_real is not None
    assert not final or (m - m_real == DEC_ROWS and tm >= DEC_ROWS)
    main = pl.BlockSpec((tm, tn), lambda i, j: (i, j))
    in_specs = [pl.BlockSpec((tm, k), lambda i, j: (i, 0)), pl.BlockSpec((tm, LANE), lambda i, j: (i, 0)),
                pl.BlockSpec((tm, kp), lambda i, j: (i, 0)),
                _w_spec(k, tn, layer, 0), _w_spec(kp, tn, layer, 0),
                pl.BlockSpec((tm, tn), lambda i, j: (i, j))]
    args = [h, ss, pe, w_gate, w_proj, res]
    if final:
        out_specs = [main, pl.BlockSpec((DEC_ROWS, tn), lambda i, j: (i, j))]
        out_shape = [jax.ShapeDtypeStruct((m_real, n), F32), jax.ShapeDtypeStruct((m // tm * DEC_ROWS, n), F32)]
    else:
        nspecs, nshapes = _norm_out(m, n, tm, tn)
        in_specs.append(_gain_spec(tn, glayer))
        args.append(gain)
        out_specs = [main] + nspecs
        out_shape = [jax.ShapeDtypeStruct((m, n), F32)] + nshapes
    outs = pl.pallas_call(
        functools.partial(_ple_kernel, final=final),
        grid=(m // tm, n // tn),
        in_specs=in_specs,
        out_specs=out_specs,
        out_shape=out_shape,
        compiler_params=_params("arbitrary", "arbitrary"),
        name="ple",
    )(*args)
    return outs if final else (outs[0], (outs[1], outs[2]))


def _split_cumsum(tri, x):
    hi = x.astype(MXU_DTYPE)
    r1 = x - hi.astype(F32)
    mid = r1.astype(MXU_DTYPE)
    lo = (r1 - mid.astype(F32)).astype(MXU_DTYPE)
    return _dot(tri, hi) + _dot(tri, mid) + _dot(tri, lo)


def _gla_out_gate(o, gn, r):
    ms = jnp.mean(o * o, axis=-1, keepdims=True)
    return o * lax.rsqrt(ms + EPS) * gn * (r * _sigmoid(r))


def _gla_kernel(q_ref, k_ref, v_ref, r_ref, lr_ref, wd_ref, bd_ref, gn_ref, *rest, nc, q_scale, aliased):
    o_ref, s_ref, st_ref = rest[1:] if aliased else rest
    c = pl.program_id(2)

    @pl.when(c == 0)
    def _():
        st_ref[...] = jnp.zeros_like(st_ref)

    cs, dk = q_ref.shape
    q = q_ref[...] * q_scale
    k = k_ref[...]
    v = v_ref[...].astype(MXU_DTYPE)
    z = _dot(lr_ref[...].astype(MXU_DTYPE), wd_ref[...].astype(MXU_DTYPE)) + bd_ref[...]
    log_a = _log_sigmoid(z) / GLA_TAU
    row = lax.broadcasted_iota(jnp.int32, (cs, cs), 0)
    col = lax.broadcasted_iota(jnp.int32, (cs, cs), 1)
    b = _split_cumsum(jnp.where(row >= col, 1.0, 0.0).astype(MXU_DTYPE), log_a)

    st = st_ref[...]
    o = _dot_nt((q * jnp.exp(b)).astype(MXU_DTYPE), st.astype(MXU_DTYPE))

    sup = min(GLA_SUPER, cs)
    n_sup = cs // sup
    rowv = lax.broadcasted_iota(jnp.int32, (cs, 1), 0)
    srow = lax.broadcasted_iota(jnp.int32, (sup, sup), 0)
    scol = lax.broadcasted_iota(jnp.int32, (sup, sup), 1)
    diag = jnp.sum(q * k, axis=-1, keepdims=True)
    a_sup = [jnp.where(srow == scol, diag[i * sup:(i + 1) * sup], 0.0) for i in range(n_sup)]
    a = jnp.zeros((cs, cs), F32)
    last = b
    half = 1
    while half < cs:
        blk = 2 * half
        shift = blk.bit_length() - 1
        if half % 8 == 0:
            split = lambda x: x.reshape(cs // blk, 2, half, dk)
            b4, q4, k4 = split(b), split(q), split(k)
            rho = b4[:, 0, half - 1:half]
            zeros = jnp.zeros((cs // blk, half, dk), F32)
            qs = jnp.stack([zeros, q4[:, 1] * jnp.exp(b4[:, 1] - rho)], axis=1).reshape(cs, dk).astype(MXU_DTYPE)
            ks = jnp.stack([k4[:, 0] * jnp.exp(rho - b4[:, 0]), zeros], axis=1).reshape(cs, dk).astype(MXU_DTYPE)
        else:
            bottom = (rowv & (blk - 1)) >= half
            rho = jnp.where(bottom, pltpu.roll(last, half, 0), last)
            f = jnp.exp(jnp.where(bottom, b - rho, rho - b))
            qs = jnp.where(bottom, q * f, 0.0).astype(MXU_DTYPE)
            ks = jnp.where(bottom, 0.0, k * f).astype(MXU_DTYPE)
            last = jnp.where(bottom, last, pltpu.roll(last, cs - half, 0))
        if blk <= sup:
            same_block = (srow >> shift) == (scol >> shift)
            a_sup = [a_i + jnp.where(same_block, _dot_nt(qs[i * sup:(i + 1) * sup], ks[i * sup:(i + 1) * sup]), 0.0)
                     for i, a_i in enumerate(a_sup)]
        else:
            a = a + jnp.where((row >> shift) == (col >> shift), _dot_nt(qs, ks), 0.0)
        half = blk
    zero = jnp.zeros((sup, sup), F32)
    a = a + jnp.concatenate([jnp.concatenate([a_sup[i] if j == i else zero for j in range(n_sup)], axis=1)
                             for i in range(n_sup)], axis=0)
    o = o + _dot(a.astype(MXU_DTYPE), v)

    b_last = b[cs - 1:cs]
    kd = (k * jnp.exp(b_last - b)).astype(MXU_DTYPE)
    st_new = st * jnp.exp(b_last) + _dot_tn(v, kd)
    st_ref[...] = st_new

    o_ref[...] = _gla_out_gate(o, gn_ref[...], r_ref[...]).astype(o_ref.dtype)

    @pl.when(c == nc - 1)
    def _():
        s_ref[...] = st_new.T


def _gla_prompt(qkv, y, lr, wd, bd, gn, s_all, layer, depth, bsz, seq, dk, dv, m_pad):
    cs = min(GLA_CHUNK, seq)
    assert seq % cs == 0 and cs & (cs - 1) == 0 and cs % 8 == 0
    nc = seq // cs
    h_ = GLA_HEADS
    aliased = s_all is not None
    kern = functools.partial(_gla_kernel, nc=nc, q_scale=dk ** -0.5, aliased=aliased)
    row = lambda b, h, c: b * nc + c
    in_specs = [pl.BlockSpec((cs, dk), lambda b, h, c: (row(b, h, c), h)),
                pl.BlockSpec((cs, dk), lambda b, h, c: (row(b, h, c), h_ + h)),
                pl.BlockSpec((cs, dv), lambda b, h, c: (row(b, h, c), 2 * h_ * dk // dv + h)),
                pl.BlockSpec((cs, dv), lambda b, h, c: (row(b, h, c), h)),
                pl.BlockSpec((cs, LANE), lambda b, h, c: (row(b, h, c), 0)),
                pl.BlockSpec((None, LANE, dk), lambda b, h, c: (layer, 0, h)),
                pl.BlockSpec((None, 1, dk), lambda b, h, c: (layer, 0, h)),
                pl.BlockSpec((None, 1, dv), lambda b, h, c: (layer, 0, 0))]
    args = [qkv, qkv, qkv, y, lr, wd, bd, gn]
    if aliased:
        in_specs.append(pl.BlockSpec(memory_space=pl.ANY))
        args.append(s_all)
    return pl.pallas_call(
        kern,
        grid=(bsz, h_, nc),
        in_specs=in_specs,
        out_specs=[pl.BlockSpec((cs, dv), lambda b, h, c: (row(b, h, c), h)),
                   pl.BlockSpec((None, None, None, dk, dv), lambda b, h, c: (layer, b, h, 0, 0))],
        out_shape=[jax.ShapeDtypeStruct((m_pad, h_ * dv), MXU_DTYPE),
                   jax.ShapeDtypeStruct((depth, bsz, h_, dk, dv), F32)],
        scratch_shapes=[pltpu.VMEM((dv, dk), F32)],
        input_output_aliases={8: 1} if aliased else {},
        compiler_params=_params("arbitrary", "arbitrary", "arbitrary"),
        name="gla_prompt",
    )(*args)


def _gla_decode_kernel(q_ref, k_ref, v_ref, r_ref, lr_ref, wd_ref, bd_ref, gn_ref, s0_ref, *rest, nb, q_scale):
    o_ref, s_ref, acc_ref = rest[-3:]
    bi = pl.program_id(1)

    @pl.when(bi == 0)
    def _():
        acc_ref[...] = jnp.zeros_like(acc_ref)

    rows = q_ref.shape[0]
    rid = lax.broadcasted_iota(jnp.int32, (rows, 1), 0)

    def pick(x):
        return jnp.sum(jnp.where(rid == bi, x, 0.0), axis=0, keepdims=True)

    def column(x_row, n):
        eye = lax.broadcasted_iota(jnp.int32, (n, n), 0) == lax.broadcasted_iota(jnp.int32, (n, n), 1)
        return jnp.sum(jnp.where(eye, x_row, 0.0), axis=1, keepdims=True)

    dk = q_ref.shape[1]
    q = pick(q_ref[...]) * q_scale
    k = pick(k_ref[...])
    v = pick(v_ref[...])
    z = _dot(lr_ref[...].astype(MXU_DTYPE), wd_ref[...].astype(MXU_DTYPE)) + bd_ref[...]
    decay = jnp.exp(pick(_log_sigmoid(z) / GLA_TAU))
    s0 = s0_ref[...]
    qd = jnp.broadcast_to(q * decay, (rows, dk)).astype(MXU_DTYPE)
    o = _dot(qd, s0.astype(MXU_DTYPE))[0:1]
    o = o + jnp.sum(q * k, axis=-1, keepdims=True) * v
    s_ref[...] = s0 * column(decay, dk) + column(k, dk) * v

    gated = _gla_out_gate(o, gn_ref[...], pick(r_ref[...]))
    acc = jnp.where(rid == bi, gated, acc_ref[...])
    acc_ref[...] = acc

    @pl.when(bi == nb - 1)
    def _():
        pad = jnp.zeros((o_ref.shape[0] - rows, o_ref.shape[1]), F32)
        o_ref[...] = jnp.concatenate([acc, pad], axis=0).astype(o_ref.dtype)


def _gla_decode(qkv, y, lr, wd, bd, gn, state, og, s_all, layer, nb, dk, dv, m_real):
    h_ = GLA_HEADS
    r8 = m_real // 8
    rdec = m_real // DEC_ROWS
    kern = functools.partial(_gla_decode_kernel, nb=nb, q_scale=dk ** -0.5)
    any_spec = pl.BlockSpec(memory_space=pl.ANY)
    extra = [] if s_all is None else [s_all]
    og_new, s_new = pl.pallas_call(
        kern,
        grid=(h_, nb),
        in_specs=[pl.BlockSpec((8, dk), lambda h, b: (r8, h)),
                  pl.BlockSpec((8, dk), lambda h, b: (r8, h_ + h)),
                  pl.BlockSpec((8, dv), lambda h, b: (r8, 2 * h_ * dk // dv + h)),
                  pl.BlockSpec((8, dv), lambda h, b: (r8, h)),
                  pl.BlockSpec((8, LANE), lambda h, b: (r8, 0)),
                  pl.BlockSpec((None, LANE, dk), lambda h, b: (layer, 0, h)),
                  pl.BlockSpec((None, 1, dk), lambda h, b: (layer, 0, h)),
                  pl.BlockSpec((None, 1, dv), lambda h, b: (layer, 0, 0)),
                  pl.BlockSpec((None, None, None, dk, dv), lambda h, b: (layer, b, h, 0, 0)),
                  any_spec] + [any_spec] * len(extra),
        out_specs=[pl.BlockSpec((DEC_ROWS, dv), lambda h, b: (rdec, h)),
                   pl.BlockSpec((None, None, None, dk, dv), lambda h, b: (layer, b, h, 0, 0))],
        out_shape=[jax.ShapeDtypeStruct(og.shape, og.dtype),
                   jax.ShapeDtypeStruct(state.shape, F32)],
        scratch_shapes=[pltpu.VMEM((8, dv), F32)],
        input_output_aliases={9: 0, 10: 1} if extra else {9: 0},
        compiler_params=_params("arbitrary", "arbitrary"),
        name="gla_decode",
    )(qkv, qkv, qkv, y, lr, wd, bd, gn, state, og, *extra)
    return og_new, s_new


def _att_proj_kernel(h_ref, ss_ref, w_ref, g_ref, o_ref, *, n_norm):
    w = w_ref[0].astype(MXU_DTYPE)
    tm, k = h_ref.shape
    nh, _, hd = o_ref.shape
    normed = pl.program_id(1) < n_norm
    nchunk = 5 if tm % 80 == 0 else (2 if tm % 32 == 0 else 1)
    for rows in [pl.ds(c * (tm // nchunk), tm // nchunk) for c in range(nchunk)]:
        r = lax.rsqrt(jnp.sum(ss_ref[rows, :], axis=-1, keepdims=True) / k + EPS)
        acc = _dot_nt(h_ref[rows, :], w) * r
        for hh in range(nh):
            x = acc[:, hh * hd:(hh + 1) * hd]
            ms = jnp.mean(x * x, axis=-1, keepdims=True)
            o_ref[hh, rows, :] = x * jnp.where(normed, lax.rsqrt(ms + EPS) * g_ref[:, hh * hd:(hh + 1) * hd], 1.0)


def _att_proj(hn, w, layer, row0, gains):
    h, ss = hn
    m, k = h.shape
    n = 3 * ATT_W
    tm, h_spec = _lhs_spec(m, k, True)
    tn = _mm_tiles(m, n)[1]
    assert (2 * ATT_W) % tn == 0 and tn % ATT_HEAD_DIM == 0
    nh = tn // ATT_HEAD_DIM
    n_norm = 2 * ATT_W // tn
    return pl.pallas_call(
        functools.partial(_att_proj_kernel, n_norm=n_norm),
        grid=(m // tm, n // tn),
        in_specs=[h_spec, pl.BlockSpec((tm, LANE), lambda i, j: (i, 0)), _w_rows_spec(k, tn, layer, row0),
                  pl.BlockSpec((None, 1, tn), lambda i, j: (layer, 0, jnp.minimum(j, n_norm - 1)))],
        out_specs=pl.BlockSpec((nh, tm, ATT_HEAD_DIM), lambda i, j: (j, i, 0)),
        out_shape=jax.ShapeDtypeStruct((n // ATT_HEAD_DIM, m, ATT_HEAD_DIM), F32),
        compiler_params=_params("arbitrary", "arbitrary"),
        name="in_proj_att",
    )(h, ss, w, gains)


def _softmax_mix(os, ls):
    m = functools.reduce(jnp.maximum, ls)
    ws = [jnp.exp(l - m) for l in ls]
    den = functools.reduce(lambda a, b: a + b, ws)
    return functools.reduce(lambda a, b: a + b, [(w / den) * o for w, o in zip(ws, os)])


def _attn_kernel(slope_ref, q0, q1, q2, k0, k1, k2, v0, v1, v2, *rest):
    oa_ref, c0, c1, c2, o_s, l_s = rest[-6:]
    j = pl.program_id(1)
    seq, hd = q0.shape
    tq = ATT_BLOCK
    row = lax.broadcasted_iota(jnp.int32, (tq, 2 * tq), 0)
    col = lax.broadcasted_iota(jnp.int32, (tq, 2 * tq), 1)
    steps = tq + row - col
    in_window = (steps >= 0) & (steps <= ATT_BLOCK)
    for g, (_, dil) in enumerate(ATT_GROUPS):
        q_ref, k_ref, v_ref = (q0, q1, q2)[g], (k0, k1, k2)[g], (v0, v1, v2)[g]
        nq = seq // (tq * dil)
        bias = slope_ref[g, j] * (steps * dil).astype(F32)

        def block(it, carry, q_ref=q_ref, k_ref=k_ref, v_ref=v_ref, g=g, dil=dil, nq=nq, bias=bias):
            r = it // nq
            i = it - r * nq
            start = r + i * (tq * dil)
            pstart = jnp.where(i > 0, start - tq * dil, start)

            def rows(st):
                return pl.ds(st, tq, stride=dil) if dil > 1 else pl.ds(pl.multiple_of(st, tq), tq)

            qh = q_ref[rows(start), :].astype(MXU_DTYPE)
            kk = jnp.concatenate([k_ref[rows(pstart), :], k_ref[rows(start), :]], axis=0).astype(MXU_DTYPE)
            vv = jnp.concatenate([v_ref[rows(pstart), :], v_ref[rows(start), :]], axis=0).astype(MXU_DTYPE)
            s = _dot_nt(qh, kk) * (hd ** -0.5)
            valid = in_window & ((col >= tq) | (i > 0))
            s = jnp.where(valid, s - bias, NEG)
            m = jnp.max(s, axis=-1, keepdims=True)
            p = jnp.exp(s - m)
            l = jnp.sum(p, axis=-1, keepdims=True)
            o_s[g, rows(start), :] = _dot(p.astype(MXU_DTYPE), vv) / l
            l_s[g, rows(start), :] = jnp.broadcast_to(m + jnp.log(l), (tq, hd))
            return carry

        lax.fori_loop(0, seq // tq, block, 0, unroll=8)
        c_ref = (c0, c1, c2)[g]
        win = c_ref.shape[1]
        c_ref[0] = k_ref[seq - win:, :]
        c_ref[1] = v_ref[seq - win:, :]
    n_g = len(ATT_GROUPS)
    oa_ref[...] = _softmax_mix([o_s[g] for g in range(n_g)], [l_s[g] for g in range(n_g)]).astype(oa_ref.dtype)


def _attn_prompt(att, slopes, bufs, layer, depth, bsz, seq, m_pad):
    for win, dil in ATT_GROUPS:
        assert win // dil == ATT_BLOCK and seq % (dil * ATT_BLOCK) == 0 and win <= seq
    hd = ATT_HEAD_DIM
    n_g = len(ATT_GROUPS)
    specs = [pl.BlockSpec((None, seq, hd), lambda b, j, h0=kind * ATT_HEADS + g * ATT_GROUP_HEADS: (h0 + j, b, 0))
             for kind in range(3) for g in range(n_g)]
    bufs = list(bufs or [])
    outs = pl.pallas_call(
        _attn_kernel,
        grid=(bsz, ATT_GROUP_HEADS),
        in_specs=[pl.BlockSpec(memory_space=pltpu.SMEM)] + specs + [pl.BlockSpec(memory_space=pl.ANY)] * len(bufs),
        out_specs=[pl.BlockSpec((seq, hd), lambda b, j: (b, j))]
        + [pl.BlockSpec((None, 2, None, None, win, hd), lambda b, j: (layer, 0, j, b, 0, 0)) for win, _ in ATT_GROUPS],
        out_shape=[jax.ShapeDtypeStruct((m_pad, ATT_GW), MXU_DTYPE)]
        + [jax.ShapeDtypeStruct((depth, 2, ATT_GROUP_HEADS, bsz, win, hd), F32) for win, _ in ATT_GROUPS],
        scratch_shapes=[pltpu.VMEM((n_g, seq, hd), F32), pltpu.VMEM((n_g, seq, hd), F32)],
        input_output_aliases={10 + g: 1 + g for g in range(len(bufs))},
        compiler_params=_params("arbitrary", "arbitrary"),
        name="attn_prompt",
    )(slopes, *([att] * 9), *bufs)
    return outs[0], outs[1:]


def _attn_decode_kernel(slope_ref, a_ref, c0_ref, c1_ref, c2_ref, o_ref):
    nh = ATT_GROUP_HEADS
    hd = ATT_HEAD_DIM
    nrow = lax.broadcasted_iota(jnp.int32, (ATT_BLOCK, 1, 1), 0)
    os, ls = [], []
    for g, (_, dil) in enumerate(ATT_GROUPS):
        c_ref = (c0_ref, c1_ref, c2_ref)[g]
        q = a_ref[g * nh:(g + 1) * nh]
        kn = a_ref[ATT_HEADS + g * nh:ATT_HEADS + (g + 1) * nh]
        vn = a_ref[2 * ATT_HEADS + g * nh:2 * ATT_HEADS + (g + 1) * nh]
        kb = c_ref[:, 0]
        vb = c_ref[:, 1]
        slope = slope_ref[g][:, 0:1]
        dist = ((ATT_BLOCK - nrow) * dil).astype(F32)
        sb = jnp.sum(kb * q, axis=-1, keepdims=True) * (hd ** -0.5) - slope * dist
        sn = jnp.sum(kn * q, axis=-1, keepdims=True) * (hd ** -0.5)
        m = jnp.maximum(jnp.max(sb, axis=0), sn)
        pb = jnp.exp(sb - m)
        pn = jnp.exp(sn - m)
        l = jnp.sum(pb, axis=0) + pn
        os.append((jnp.sum(pb * vb, axis=0) + pn * vn) / l)
        ls.append(m + jnp.log(l))
    o_ref[...] = _softmax_mix(os, ls)


def _attn_decode(a_dec, slopes_v, caches, layer):
    nb = a_dec.shape[0]
    views, specs = [], []
    for (win, dil), cbuf in zip(ATT_GROUPS, caches):
        depth, nb_, wb = cbuf.shape[:3]
        assert wb == win and nb_ == nb
        views.append(cbuf.reshape(depth, nb, win // dil, dil, 2, ATT_GROUP_HEADS, ATT_HEAD_DIM))
        specs.append(pl.BlockSpec((None, None, win // dil, None, 2, ATT_GROUP_HEADS, ATT_HEAD_DIM),
                                  lambda b: (layer, b, 0, 0, 0, 0, 0)))
    return pl.pallas_call(
        _attn_decode_kernel,
        grid=(nb,),
        in_specs=[pl.BlockSpec(slopes_v.shape, lambda b: (0, 0, 0)),
                  pl.BlockSpec((None,) + a_dec.shape[1:], lambda b: (b, 0, 0))] + specs,
        out_specs=pl.BlockSpec((None, ATT_GROUP_HEADS, ATT_HEAD_DIM), lambda b: (b, 0, 0)),
        out_shape=jax.ShapeDtypeStruct((nb, ATT_GROUP_HEADS, ATT_HEAD_DIM), F32),
        compiler_params=_params("arbitrary"),
        name="attn_decode",
    )(slopes_v, a_dec, *views)


def _place_rows_kernel(x_ref, dst_ref, o_ref):
    del dst_ref
    pad = jnp.zeros((o_ref.shape[0] - x_ref.shape[0], o_ref.shape[1]), F32)
    o_ref[...] = jnp.concatenate([x_ref[...], pad], axis=0).astype(o_ref.dtype)


def _place_sample_rows(x, dst, m_real):
    n = dst.shape[1]
    return pl.pallas_call(
        _place_rows_kernel,
        grid=(1,),
        in_specs=[pl.BlockSpec(x.shape, lambda i: (0, 0)), pl.BlockSpec(memory_space=pl.ANY)],
        out_specs=pl.BlockSpec((DEC_ROWS, n), lambda i: (m_real // DEC_ROWS, 0)),
        out_shape=jax.ShapeDtypeStruct(dst.shape, dst.dtype),
        input_output_aliases={1: 0},
        compiler_params=_params("arbitrary"),
        name="place_sample_rows",
    )(x, dst)


def kernel(x_prompt, x_sample, state_gla, cache_w128, cache_w512, cache_w2048, p_prompt, p_sample, norm_ffn1, ffn1_w_in, ffn1_w_out, norm_mix, w_in, gla_w_decay, gla_b_decay, gla_norm, att_q_norm, att_k_norm, w_gla_out, w_att_out, w_out, norm_ffn2, ffn2_w_in, ffn2_w_out, norm_ple, w_ple_gate, w_ple_proj):
    bsz, seq, d = x_prompt.shape
    nb, dec_seq, _ = x_sample.shape
    depth = norm_ffn1.shape[0]
    assert dec_seq == 1 and nb == 8
    m_real = bsz * seq
    assert m_real % DEC_ROWS == 0
    m_pad = m_real + DEC_ROWS
    gla_qk = gla_w_decay.shape[-1]
    dk = gla_qk // GLA_HEADS
    gla_v = d
    dv = gla_v // GLA_HEADS
    caches = (cache_w128, cache_w512, cache_w2048)
    n_g = len(ATT_GROUPS)
    nh, hd = ATT_GROUP_HEADS, ATT_HEAD_DIM

    lr_off = 2 * gla_qk + gla_v
    r_off = lr_off + GLA_RANK
    q_off = r_off + gla_v
    z_off = q_off + 3 * ATT_W
    assert w_in.shape[-1] == z_off + 2 * d and lr_off % LANE == 0 and r_off % 8 == 0
    w_in_t = jnp.swapaxes(w_in, 1, 2)

    zrow = lambda n, w: jnp.zeros((n, w), F32)
    x = jnp.concatenate([x_prompt.reshape(m_real, d), x_sample.reshape(nb, d), zrow(DEC_ROWS - nb, d)], axis=0)
    pdim = p_prompt.shape[-1]
    pe = jnp.concatenate([p_prompt.reshape(depth, m_real, pdim), p_sample.reshape(depth, nb, pdim),
                          jnp.zeros((depth, DEC_ROWS - nb, pdim), F32)], axis=1)

    wd = jnp.zeros((depth, LANE, gla_qk), F32).at[:, :GLA_RANK].set(gla_w_decay)
    bd = gla_b_decay.reshape(depth, 1, gla_qk)
    gn = gla_norm.reshape(depth, 1, dv)
    att_gain = jnp.concatenate([jnp.tile(att_q_norm, (1, ATT_HEADS)), jnp.tile(att_k_norm, (1, ATT_HEADS))],
                               axis=1).reshape(depth, 1, 2 * ATT_W)
    slopes = jnp.exp2(-8.0 * jnp.arange(1, ATT_HEADS + 1, dtype=F32) / ATT_HEADS)
    slopes_s = slopes.reshape(n_g, nh)
    slopes_v = jnp.broadcast_to(slopes.reshape(n_g, nh, 1), (n_g, nh, LANE))
    g3 = lambda a: a.reshape(depth, 1, d)
    n_f1, n_mix, n_f2, n_ple = g3(norm_ffn1), g3(norm_mix), g3(norm_ffn2), g3(norm_ple)

    _, tn_g = _mm_tiles(m_pad, gla_v + 2 * d)
    n_r, n_att = gla_v // tn_g, 3 * ATT_W // tn_g
    gate_cols = lambda j: jnp.where(j < n_r, j, j + n_att)

    gla_p, gla_s, win_p, new_rows = None, None, None, []
    hn = _norm_operand(x, n_f1, 0)
    for i in range(depth):
        x, h = _ffn_out(_swiglu_in(hn, ffn1_w_in, i), ffn1_w_out, i, x, n_mix, i)
        qkv = _matmul(h, w_in_t, i, 0, lr_off, F32, w_rows=True, big=True, name="in_proj_gla")
        lr = _matmul(h, w_in_t, i, lr_off, LANE, F32, tn=LANE, w_rows=True, name="in_proj_decay")
        gates = _matmul(h, w_in_t, i, r_off, gla_v + 2 * d, F32, col_map=gate_cols, w_rows=True, big=True,
                        name="in_proj_gates")
        att = _att_proj(h, w_in_t, i, q_off, att_gain)

        og, gla_p = _gla_prompt(qkv, gates, lr, wd, bd, gn, gla_p, i, depth, bsz, seq, dk, dv, m_pad)
        og, gla_s = _gla_decode(qkv, gates, lr, wd, bd, gn, state_gla, og, gla_s, i, nb, dk, dv, m_real)

        oa, win_p = _attn_prompt(att, slopes_s, win_p, i, depth, bsz, seq, m_pad)
        a_dec = jnp.transpose(att[:, m_real:m_real + nb], (1, 0, 2))
        o_dec = _attn_decode(a_dec, slopes_v, caches, i)
        oa = _place_sample_rows(o_dec.reshape(nb, ATT_GW), oa, m_real)

        merged = _merge(og, oa, w_gla_out, w_att_out, i, gates, gla_v, gla_v + d)
        x, hn = _matmul_residual(merged, w_out, i, x, n_f2, i)
        x, hn = _ffn_out(_swiglu_in(hn, ffn2_w_in, i), ffn2_w_out, i, x, n_ple, i)
        if i == depth - 1:
            x = _ple(hn, pe[i], w_ple_gate, w_ple_proj, i, x, m_real=m_real)
        else:
            x, hn = _ple(hn, pe[i], w_ple_gate, w_ple_proj, i, x, n_f1, i + 1)

        new_rows.append(a_dec.reshape(nb, 3, n_g, nh, hd)[:, 1:])

    y_prompt = x[0].reshape(bsz, seq, d)
    y_sample = x[1][x[1].shape[0] - DEC_ROWS:][:nb].reshape(nb, 1, d)
    new_rows = jnp.stack(new_rows)
    shift_cfg = [(0, 0, 0), (0, 0, 0), (-1, 1, 0), (0, 0, 0), (0, 0, 0), (0, 0, 0)]
    win_s = [lax.dynamic_update_slice(lax.pad(caches[g], jnp.zeros((), F32), shift_cfg), new_rows[:, :, None, :, g],
                                      (0, 0, caches[g].shape[2] - 1, 0, 0, 0)) for g in range(n_g)]
    win_p = [jnp.transpose(w, (0, 3, 4, 1, 2, 5)) for w in win_p]
    return (y_prompt, y_sample, gla_p, win_p[0], win_p[1], win_p[2], gla_s, win_s[0], win_s[1], win_s[2])
```

```python
import functools

import jax
import jax.numpy as jnp
from jax import lax
from jax.experimental import pallas as pl
from jax.experimental.pallas import tpu as pltpu

F32 = jnp.float32
MXU_DTYPE = jnp.bfloat16

EPS = 1e-6
GLA_HEADS = 4
GLA_RANK = 16
GLA_TAU = 16.0
GLA_CHUNK = 256
GLA_SUPER = 128
ATT_GROUPS = ((128, 1), (512, 4), (2048, 16))
ATT_GROUP_HEADS = 8
ATT_HEAD_DIM = 128
ATT_HEADS = len(ATT_GROUPS) * ATT_GROUP_HEADS
ATT_GW = ATT_GROUP_HEADS * ATT_HEAD_DIM
ATT_W = ATT_HEADS * ATT_HEAD_DIM
ATT_BLOCK = 128
DEC_ROWS = 128
LANE = 128
NEG = -1e30

VMEM_LIMIT = 56 * 1024 * 1024


VMEM_LIMIT_MAX =61 * 1024 * 1024


def _params(*sem, vmem=VMEM_LIMIT):
    return pltpu.CompilerParams(dimension_semantics=sem, vmem_limit_bytes=vmem)


def _pick_tile(n, target, mult):
    best = None
    for t in range(mult, min(n, target) + 1, mult):
        if n % t == 0:
            best = t
    assert best is not None, (n, target, mult)
    return best


_sigmoid = jax.nn.sigmoid


def _log_sigmoid(x):
    return jnp.minimum(x, 0.0) - jnp.log(1.0 + jnp.exp(-jnp.abs(x)))


def _dot(a, b):
    return jnp.dot(a, b, preferred_element_type=F32)


def _dot_nt(a, b):
    return lax.dot_general(a, b, (((1,), (1,)), ((), ())), preferred_element_type=F32)


def _dot_tn(a, b):
    return lax.dot_general(a, b, (((0,), (0,)), ((), ())), preferred_element_type=F32)


def _row_scale(ss_ref, d):
    return lax.rsqrt(jnp.sum(ss_ref[...], axis=-1, keepdims=True) / d + EPS)


def _emit_norm_operand(out, g_ref, xg_ref, ss_ref, j):
    xg_ref[...] = (out * g_ref[...]).astype(xg_ref.dtype)

    @pl.when(j == 0)
    def _():
        ss_ref[...] = jnp.zeros_like(ss_ref)

    lane = lax.broadcasted_iota(jnp.int32, ss_ref.shape, 1)
    ss_ref[...] = jnp.where(lane == j, jnp.sum(out * out, axis=-1, keepdims=True), ss_ref[...])


def _norm_out(m, n, tm, tn):
    assert n // tn <= LANE
    specs = [pl.BlockSpec((tm, tn), lambda i, j, *_: (i, j)), pl.BlockSpec((tm, LANE), lambda i, j, *_: (i, 0))]
    shapes = [jax.ShapeDtypeStruct((m, n), MXU_DTYPE), jax.ShapeDtypeStruct((m, LANE), F32)]
    return specs, shapes


def _gain_spec(tn, layer):
    return pl.BlockSpec((None, 1, tn), lambda i, j, *_: (layer, 0, j))


def _norm_operand_kernel(x_ref, g_ref, xg_ref, ss_ref):
    x = x_ref[...]
    xg_ref[...] = (x * g_ref[...]).astype(xg_ref.dtype)
    lane = lax.broadcasted_iota(jnp.int32, ss_ref.shape, 1)
    ss_ref[...] = jnp.where(lane == 0, jnp.sum(x * x, axis=-1, keepdims=True), 0.0)


def _norm_operand(x, gain, layer):
    m, d = x.shape
    tr = _pick_tile(m, 512, 16)
    return pl.pallas_call(
        _norm_operand_kernel,
        grid=(m // tr,),
        in_specs=[pl.BlockSpec((tr, d), lambda i: (i, 0)),
                  pl.BlockSpec((None, 1, d), lambda i: (layer, 0, 0))],
        out_specs=[pl.BlockSpec((tr, d), lambda i: (i, 0)), pl.BlockSpec((tr, LANE), lambda i: (i, 0))],
        out_shape=[jax.ShapeDtypeStruct((m, d), MXU_DTYPE), jax.ShapeDtypeStruct((m, LANE), F32)],
        compiler_params=_params("arbitrary"),
        name="norm_operand",
    )(x, gain)


def _mm_tiles(m, n):
    return _pick_tile(m, 1100, 16), _pick_tile(n, 512, LANE)


def _lhs_spec(m, k, big):
    if big:
        tm = _pick_tile(m, 2200, 16)
        return tm, pl.BlockSpec((tm, k), lambda i, j: (i, 0), pipeline_mode=pl.Buffered(1))
    tm = _pick_tile(m, 1100, 16)
    return tm, pl.BlockSpec((tm, k), lambda i, j: (i, 0))


def _w_spec(k, tn, layer, col_blk0):
    return pl.BlockSpec((None, k, tn), lambda i, j: (layer, 0, col_blk0 + j))


def _mm_kernel(h_ref, ss_ref, w_ref, o_ref, *, w_rows):
    w = (w_ref[0] if w_rows else w_ref[...]).astype(MXU_DTYPE)
    tm, k = h_ref.shape
    nchunk = 2 if tm % 32 == 0 else 1
    for rows in [pl.ds(c * (tm // nchunk), tm // nchunk) for c in range(nchunk)]:
        h = h_ref[rows, :]
        acc = _dot_nt(h, w) if w_rows else _dot(h, w)
        r = lax.rsqrt(jnp.sum(ss_ref[rows, :], axis=-1, keepdims=True) / k + EPS)
        o_ref[rows, :] = (acc * r).astype(o_ref.dtype)


def _w_rows_spec(k, tn, layer, row0, col_map=None):
    col_map = col_map or (lambda j: j)
    return pl.BlockSpec((pl.Element(1), pl.Element(tn), pl.Element(k)),
                        lambda i, j: (layer, pl.multiple_of(row0 + col_map(j) * tn, 8), 0))


def _matmul(hn, w, layer, col0, n, out_dtype, tn=None, col_map=None, w_rows=False, big=False, name="matmul"):
    h, ss = hn
    m, k = h.shape
    tm, h_spec = _lhs_spec(m, k, big)
    tn = tn or _mm_tiles(m, n)[1]
    assert n % tn == 0
    if w_rows:
        w_spec = _w_rows_spec(k, tn, layer, col0, col_map)
    else:
        assert col0 % tn == 0
        col_map = col_map or (lambda j: j)
        w_spec = pl.BlockSpec((None, k, tn), lambda i, j: (layer, 0, col0 // tn + col_map(j)))
    return pl.pallas_call(
        functools.partial(_mm_kernel, w_rows=w_rows),
        grid=(m // tm, n // tn),
        in_specs=[h_spec, pl.BlockSpec((tm, LANE), lambda i, j: (i, 0)), w_spec],
        out_specs=pl.BlockSpec((tm, tn), lambda i, j: (i, j)),
        out_shape=jax.ShapeDtypeStruct((m, n), out_dtype),
        compiler_params=_params("arbitrary", "arbitrary"),
        name=name,
    )(h, ss, w)


def _mm_res_kernel(h_ref, w_ref, r_ref, g_ref, o_ref, xg_ref, ss_ref, *, scale):
    acc = _dot(h_ref[...], w_ref[...].astype(MXU_DTYPE))
    out = r_ref[...] + (acc if scale is None else scale * acc)
    o_ref[...] = out
    _emit_norm_operand(out, g_ref, xg_ref, ss_ref, pl.program_id(1))


def _matmul_residual(h, w, layer, res, gain, glayer, scale=None, tn=None, lhs_once=False, vmem=VMEM_LIMIT,
                     name="matmul_residual"):
    m, k = h.shape
    n = w.shape[-1]
    tm, tn_ = _mm_tiles(m, n)
    tn = tn or tn_
    nspecs, nshapes = _norm_out(m, n, tm, tn)
    h_spec = pl.BlockSpec((tm, k), lambda i, j: (i, 0), **({"pipeline_mode": pl.Buffered(1)} if lhs_once else {}))
    x, xg, ss = pl.pallas_call(
        functools.partial(_mm_res_kernel, scale=scale),
        grid=(m // tm, n // tn),
        in_specs=[h_spec, _w_spec(k, tn, layer, 0),
                  pl.BlockSpec((tm, tn), lambda i, j: (i, j)), _gain_spec(tn, glayer)],
        out_specs=[pl.BlockSpec((tm, tn), lambda i, j: (i, j))] + nspecs,
        out_shape=[jax.ShapeDtypeStruct((m, n), F32)] + nshapes,
        compiler_params=_params("arbitrary", "arbitrary", vmem=vmem),
        name=name,
    )(h, w, res, gain)
    return x, (xg, ss)


def _swiglu_kernel(h_ref, ss_ref, wg_ref, wu_ref, o_ref):
    wg = wg_ref[...].astype(MXU_DTYPE)
    wu = wu_ref[...].astype(MXU_DTYPE)
    tm, k = h_ref.shape
    nchunk = 10 if tm % 160 == 0 else (2 if tm % 32 == 0 else 1)
    for rows in [pl.ds(c * (tm // nchunk), tm // nchunk) for c in range(nchunk)]:
        h = h_ref[rows, :]
        r = lax.rsqrt(jnp.sum(ss_ref[rows, :], axis=-1, keepdims=True) / k + EPS)
        g = _dot(h, wg) * r
        u = _dot(h, wu) * r
        o_ref[rows, :] = (g * _sigmoid(g) * u).astype(o_ref.dtype)


def _swiglu_in(hn, w, layer):
    h, ss = hn
    m, k = h.shape
    f = w.shape[-1] // 2
    tm, h_spec = _lhs_spec(m, k, True)
    tn = _pick_tile(f, 256, LANE)
    nf = f // tn
    return pl.pallas_call(
        _swiglu_kernel,
        grid=(m // tm, nf),
        in_specs=[h_spec, pl.BlockSpec((tm, LANE), lambda i, j: (i, 0)),
                  _w_spec(k, tn, layer, 0), _w_spec(k, tn, layer, nf)],
        out_specs=pl.BlockSpec((tm, tn), lambda i, j: (i, j)),
        out_shape=jax.ShapeDtypeStruct((m, f), MXU_DTYPE),
        compiler_params=_params("arbitrary", "arbitrary"),
        name="swiglu_in",
    )(h, ss, w, w)


def _ffn_out(a, w, layer, res, gain, glayer):
    return _matmul_residual(a, w, layer, res, gain, glayer, scale=0.5, tn=_pick_tile(w.shape[-1], 256, LANE),
                            lhs_once=True, vmem=VMEM_LIMIT_MAX, name="ffn_out")


def _merge_kernel(og_ref, oa_ref, wg_ref, wa_ref, zg_ref, za_ref, o_ref):
    bg = _dot(og_ref[...], wg_ref[...].astype(MXU_DTYPE))
    ba = _dot(oa_ref[...], wa_ref[...].astype(MXU_DTYPE))
    o_ref[...] = (_sigmoid(zg_ref[...]) * bg + _sigmoid(za_ref[...]) * ba).astype(o_ref.dtype)


def _merge(og, oa, w_gla_out, w_att_out, layer, y, zg_off, za_off):
    m, kg = og.shape
    ka = oa.shape[1]
    n = w_gla_out.shape[-1]
    tm, tn = _mm_tiles(m, n)
    assert zg_off % tn == 0 and za_off % tn == 0
    once = pl.Buffered(1)
    return pl.pallas_call(
        _merge_kernel,
        grid=(m // tm, n // tn),
        in_specs=[pl.BlockSpec((tm, kg), lambda i, j: (i, 0), pipeline_mode=once),
                  pl.BlockSpec((tm, ka), lambda i, j: (i, 0), pipeline_mode=once),
                  _w_spec(kg, tn, layer, 0), _w_spec(ka, tn, layer, 0),
                  pl.BlockSpec((tm, tn), lambda i, j: (i, zg_off // tn + j)),
                  pl.BlockSpec((tm, tn), lambda i, j: (i, za_off // tn + j))],
        out_specs=pl.BlockSpec((tm, tn), lambda i, j: (i, j)),
        out_shape=jax.ShapeDtypeStruct((m, n), MXU_DTYPE),
        compiler_params=_params("arbitrary", "arbitrary"),
        name="merge",
    )(og, oa, w_gla_out, w_att_out, y, y)


def _ple_kernel(h_ref, ss_ref, pe_ref, wg_ref, wp_ref, r_ref, *rest, final):
    gate = _dot(h_ref[...], wg_ref[...].astype(MXU_DTYPE)) * _row_scale(ss_ref, h_ref.shape[1])
    proj = _dot(pe_ref[...].astype(MXU_DTYPE), wp_ref[...].astype(MXU_DTYPE))
    out = r_ref[...] + _sigmoid(gate) * proj
    if final:
        o_ref, tail_ref = rest
        o_ref[...] = out
        tail_ref[...] = out[out.shape[0] - DEC_ROWS:]
    else:
        g_ref, o_ref, xg_ref, ss_out_ref = rest
        o_ref[...] = out
        _emit_norm_operand(out, g_ref, xg_ref, ss_out_ref, pl.program_id(1))


def _ple(hn, pe, w_gate, w_proj, layer, res, gain=None, glayer=None, m_real=None):
    h, ss = hn
    m, k = h.shape
    kp = pe.shape[1]
    n = w_gate.shape[-1]
    tm, tn = _mm_tiles(m, n)
    final = m_real is not None
    assert not final or (m - m_real == DEC_ROWS and tm >= DEC_ROWS)
    main = pl.BlockSpec((tm, tn), lambda i, j: (i, j))
    in_specs = [pl.BlockSpec((tm, k), lambda i, j: (i, 0)), pl.BlockSpec((tm, LANE), lambda i, j: (i, 0)),
                pl.BlockSpec((tm, kp), lambda i, j: (i, 0)),
                _w_spec(k, tn, layer, 0), _w_spec(kp, tn, layer, 0),
                pl.BlockSpec((tm, tn), lambda i, j: (i, j))]
    args = [h, ss, pe, w_gate, w_proj, res]
    if final:
        out_specs = [main, pl.BlockSpec((DEC_ROWS, tn), lambda i, j: (i, j))]
        out_shape = [jax.ShapeDtypeStruct((m_real, n), F32), jax.ShapeDtypeStruct((m // tm * DEC_ROWS, n), F32)]
    else:
        nspecs, nshapes = _norm_out(m, n, tm, tn)
        in_specs.append(_gain_spec(tn, glayer))
        args.append(gain)
        out_specs = [main] + nspecs
        out_shape = [jax.ShapeDtypeStruct((m, n), F32)] + nshapes
    outs = pl.pallas_call(
        functools.partial(_ple_kernel, final=final),
        grid=(m // tm, n // tn),
        in_specs=in_specs,
        out_specs=out_specs,
        out_shape=out_shape,
        compiler_params=_params("arbitrary", "arbitrary"),
        name="ple",
    )(*args)
    return outs if final else (outs[0], (outs[1], outs[2]))


def _split_cumsum(tri, x):
    hi = x.astype(MXU_DTYPE)
    r1 = x - hi.astype(F32)
    mid = r1.astype(MXU_DTYPE)
    lo = (r1 - mid.astype(F32)).astype(MXU_DTYPE)
    return _dot(tri, hi) + _dot(tri, mid) + _dot(tri, lo)


def _gla_out_gate(o, gn, r):
    ms = jnp.mean(o * o, axis=-1, keepdims=True)
    return o * lax.rsqrt(ms + EPS) * gn * (r * _sigmoid(r))


def _gla_kernel(q_ref, k_ref, v_ref, r_ref, lr_ref, wd_ref, bd_ref, gn_ref, *rest, nc, q_scale, aliased):
    o_ref, s_ref, st_ref = rest[1:] if aliased else rest
    c = pl.program_id(2)

    @pl.when(c == 0)
    def _():
        st_ref[...] = jnp.zeros_like(st_ref)

    cs, dk = q_ref.shape
    q = q_ref[...] * q_scale
    k = k_ref[...]
    v = v_ref[...].astype(MXU_DTYPE)
    z = _dot(lr_ref[...].astype(MXU_DTYPE), wd_ref[...].astype(MXU_DTYPE)) + bd_ref[...]
    log_a = _log_sigmoid(z) / GLA_TAU
    row = lax.broadcasted_iota(jnp.int32, (cs, cs), 0)
    col = lax.broadcasted_iota(jnp.int32, (cs, cs), 1)
    b = _split_cumsum(jnp.where(row >= col, 1.0, 0.0).astype(MXU_DTYPE), log_a)

    st = st_ref[...]
    o = _dot_nt((q * jnp.exp(b)).astype(MXU_DTYPE), st.astype(MXU_DTYPE))

    sup = min(GLA_SUPER, cs)
    n_sup = cs // sup
    rowv = lax.broadcasted_iota(jnp.int32, (cs, 1), 0)
    srow = lax.broadcasted_iota(jnp.int32, (sup, sup), 0)
    scol = lax.broadcasted_iota(jnp.int32, (sup, sup), 1)
    diag = jnp.sum(q * k, axis=-1, keepdims=True)
    a_sup = [jnp.where(srow == scol, diag[i * sup:(i + 1) * sup], 0.0) for i in range(n_sup)]
    a = jnp.zeros((cs, cs), F32)
    last = b
    half = 1
    while half < cs:
        blk = 2 * half
        shift = blk.bit_length() - 1
        if half % 8 == 0:
            split = lambda x: x.reshape(cs // blk, 2, half, dk)
            b4, q4, k4 = split(b), split(q), split(k)
            rho = b4[:, 0, half - 1:half]
            zeros = jnp.zeros((cs // blk, half, dk), F32)
            qs = jnp.stack([zeros, q4[:, 1] * jnp.exp(b4[:, 1] - rho)], axis=1).reshape(cs, dk).astype(MXU_DTYPE)
            ks = jnp.stack([k4[:, 0] * jnp.exp(rho - b4[:, 0]), zeros], axis=1).reshape(cs, dk).astype(MXU_DTYPE)
        else:
            bottom = (rowv & (blk - 1)) >= half
            rho = jnp.where(bottom, pltpu.roll(last, half, 0), last)
            f = jnp.exp(jnp.where(bottom, b - rho, rho - b))
            qs = jnp.where(bottom, q * f, 0.0).astype(MXU_DTYPE)
            ks = jnp.where(bottom, 0.0, k * f).astype(MXU_DTYPE)
            last = jnp.where(bottom, last, pltpu.roll(last, cs - half, 0))
        if blk <= sup:
            same_block = (srow >> shift) == (scol >> shift)
            a_sup = [a_i + jnp.where(same_block, _dot_nt(qs[i * sup:(i + 1) * sup], ks[i * sup:(i + 1) * sup]), 0.0)
                     for i, a_i in enumerate(a_sup)]
        else:
            a = a + jnp.where((row >> shift) == (col >> shift), _dot_nt(qs, ks), 0.0)
        half = blk
    zero = jnp.zeros((sup, sup), F32)
    a = a + jnp.concatenate([jnp.concatenate([a_sup[i] if j == i else zero for j in range(n_sup)], axis=1)
                             for i in range(n_sup)], axis=0)
    o = o + _dot(a.astype(MXU_DTYPE), v)

    b_last = b[cs - 1:cs]
    kd = (k * jnp.exp(b_last - b)).astype(MXU_DTYPE)
    st_new = st * jnp.exp(b_last) + _dot_tn(v, kd)
    st_ref[...] = st_new

    o_ref[...] = _gla_out_gate(o, gn_ref[...], r_ref[...]).astype(o_ref.dtype)

    @pl.when(c == nc - 1)
    def _():
        s_ref[...] = st_new.T


def _gla_prompt(qkv, y, lr, wd, bd, gn, s_all, layer, depth, bsz, seq, dk, dv, m_pad):
    cs = min(GLA_CHUNK, seq)
    assert seq % cs == 0 and cs & (cs - 1) == 0 and cs % 8 == 0
    nc = seq // cs
    h_ = GLA_HEADS
    aliased = s_all is not None
    kern = functools.partial(_gla_kernel, nc=nc, q_scale=dk ** -0.5, aliased=aliased)
    row = lambda b, h, c: b * nc + c
    in_specs = [pl.BlockSpec((cs, dk), lambda b, h, c: (row(b, h, c), h)),
                pl.BlockSpec((cs, dk), lambda b, h, c: (row(b, h, c), h_ + h)),
                pl.BlockSpec((cs, dv), lambda b, h, c: (row(b, h, c), 2 * h_ * dk // dv + h)),
                pl.BlockSpec((cs, dv), lambda b, h, c: (row(b, h, c), h)),
                pl.BlockSpec((cs, LANE), lambda b, h, c: (row(b, h, c), 0)),
                pl.BlockSpec((None, LANE, dk), lambda b, h, c: (layer, 0, h)),
                pl.BlockSpec((None, 1, dk), lambda b, h, c: (layer, 0, h)),
                pl.BlockSpec((None, 1, dv), lambda b, h, c: (layer, 0, 0))]
    args = [qkv, qkv, qkv, y, lr, wd, bd, gn]
    if aliased:
        in_specs.append(pl.BlockSpec(memory_space=pl.ANY))
        args.append(s_all)
    return pl.pallas_call(
        kern,
        grid=(bsz, h_, nc),
        in_specs=in_specs,
        out_specs=[pl.BlockSpec((cs, dv), lambda b, h, c: (row(b, h, c), h)),
                   pl.BlockSpec((None, None, None, dk, dv), lambda b, h, c: (layer, b, h, 0, 0))],
        out_shape=[jax.ShapeDtypeStruct((m_pad, h_ * dv), MXU_DTYPE),
                   jax.ShapeDtypeStruct((depth, bsz, h_, dk, dv), F32)],
        scratch_shapes=[pltpu.VMEM((dv, dk), F32)],
        input_output_aliases={8: 1} if aliased else {},
        compiler_params=_params("arbitrary", "arbitrary", "arbitrary"),
        name="gla_prompt",
    )(*args)


def _gla_decode_kernel(q_ref, k_ref, v_ref, r_ref, lr_ref, wd_ref, bd_ref, gn_ref, s0_ref, *rest, nb, q_scale):
    o_ref, s_ref, acc_ref = rest[-3:]
    bi = pl.program_id(1)

    @pl.when(bi == 0)
    def _():
        acc_ref[...] = jnp.zeros_like(acc_ref)

    rows = q_ref.shape[0]
    rid = lax.broadcasted_iota(jnp.int32, (rows, 1), 0)

    def pick(x):
        return jnp.sum(jnp.where(rid == bi, x, 0.0), axis=0, keepdims=True)

    def column(x_row, n):
        eye = lax.broadcasted_iota(jnp.int32, (n, n), 0) == lax.broadcasted_iota(jnp.int32, (n, n), 1)
        return jnp.sum(jnp.where(eye, x_row, 0.0), axis=1, keepdims=True)

    dk = q_ref.shape[1]
    q = pick(q_ref[...]) * q_scale
    k = pick(k_ref[...])
    v = pick(v_ref[...])
    z = _dot(lr_ref[...].astype(MXU_DTYPE), wd_ref[...].astype(MXU_DTYPE)) + bd_ref[...]
    decay = jnp.exp(pick(_log_sigmoid(z) / GLA_TAU))
    s0 = s0_ref[...]
    qd = jnp.broadcast_to(q * decay, (rows, dk)).astype(MXU_DTYPE)
    o = _dot(qd, s0.astype(MXU_DTYPE))[0:1]
    o = o + jnp.sum(q * k, axis=-1, keepdims=True) * v
    s_ref[...] = s0 * column(decay, dk) + column(k, dk) * v

    gated = _gla_out_gate(o, gn_ref[...], pick(r_ref[...]))
    acc = jnp.where(rid == bi, gated, acc_ref[...])
    acc_ref[...] = acc

    @pl.when(bi == nb - 1)
    def _():
        pad = jnp.zeros((o_ref.shape[0] - rows, o_ref.shape[1]), F32)
        o_ref[...] = jnp.concatenate([acc, pad], axis=0).astype(o_ref.dtype)


def _gla_decode(qkv, y, lr, wd, bd, gn, state, og, s_all, layer, nb, dk, dv, m_real):
    h_ = GLA_HEADS
    r8 = m_real // 8
    rdec = m_real // DEC_ROWS
    kern = functools.partial(_gla_decode_kernel, nb=nb, q_scale=dk ** -0.5)
    any_spec = pl.BlockSpec(memory_space=pl.ANY)
    extra = [] if s_all is None else [s_all]
    og_new, s_new = pl.pallas_call(
        kern,
        grid=(h_, nb),
        in_specs=[pl.BlockSpec((8, dk), lambda h, b: (r8, h)),
                  pl.BlockSpec((8, dk), lambda h, b: (r8, h_ + h)),
                  pl.BlockSpec((8, dv), lambda h, b: (r8, 2 * h_ * dk // dv + h)),
                  pl.BlockSpec((8, dv), lambda h, b: (r8, h)),
                  pl.BlockSpec((8, LANE), lambda h, b: (r8, 0)),
                  pl.BlockSpec((None, LANE, dk), lambda h, b: (layer, 0, h)),
                  pl.BlockSpec((None, 1, dk), lambda h, b: (layer, 0, h)),
                  pl.BlockSpec((None, 1, dv), lambda h, b: (layer, 0, 0)),
                  pl.BlockSpec((None, None, None, dk, dv), lambda h, b: (layer, b, h, 0, 0)),
                  any_spec] + [any_spec] * len(extra),
        out_specs=[pl.BlockSpec((DEC_ROWS, dv), lambda h, b: (rdec, h)),
                   pl.BlockSpec((None, None, None, dk, dv), lambda h, b: (layer, b, h, 0, 0))],
        out_shape=[jax.ShapeDtypeStruct(og.shape, og.dtype),
                   jax.ShapeDtypeStruct(state.shape, F32)],
        scratch_shapes=[pltpu.VMEM((8, dv), F32)],
        input_output_aliases={9: 0, 10: 1} if extra else {9: 0},
        compiler_params=_params("arbitrary", "arbitrary"),
        name="gla_decode",
    )(qkv, qkv, qkv, y, lr, wd, bd, gn, state, og, *extra)
    return og_new, s_new


def _att_proj_kernel(h_ref, ss_ref, w_ref, g_ref, o_ref, *, n_norm):
    w = w_ref[0].astype(MXU_DTYPE)
    tm, k = h_ref.shape
    nh, _, hd = o_ref.shape
    normed = pl.program_id(1) < n_norm
    nchunk = 5 if tm % 80 == 0 else (2 if tm % 32 == 0 else 1)
    for rows in [pl.ds(c * (tm // nchunk), tm // nchunk) for c in range(nchunk)]:
        r = lax.rsqrt(jnp.sum(ss_ref[rows, :], axis=-1, keepdims=True) / k + EPS)
        acc = _dot_nt(h_ref[rows, :], w) * r
        for hh in range(nh):
            x = acc[:, hh * hd:(hh + 1) * hd]
            ms = jnp.mean(x * x, axis=-1, keepdims=True)
            o_ref[hh, rows, :] = x * jnp.where(normed, lax.rsqrt(ms + EPS) * g_ref[:, hh * hd:(hh + 1) * hd], 1.0)


def _att_proj(hn, w, layer, row0, gains):
    h, ss = hn
    m, k = h.shape
    n = 3 * ATT_W
    tm, h_spec = _lhs_spec(m, k, True)
    tn = _mm_tiles(m, n)[1]
    assert (2 * ATT_W) % tn == 0 and tn % ATT_HEAD_DIM == 0
    nh = tn // ATT_HEAD_DIM
    n_norm = 2 * ATT_W // tn
    return pl.pallas_call(
        functools.partial(_att_proj_kernel, n_norm=n_norm),
        grid=(m // tm, n // tn),
        in_specs=[h_spec, pl.BlockSpec((tm, LANE), lambda i, j: (i, 0)), _w_rows_spec(k, tn, layer, row0),
                  pl.BlockSpec((None, 1, tn), lambda i, j: (layer, 0, jnp.minimum(j, n_norm - 1)))],
        out_specs=pl.BlockSpec((nh, tm, ATT_HEAD_DIM), lambda i, j: (j, i, 0)),
        out_shape=jax.ShapeDtypeStruct((n // ATT_HEAD_DIM, m, ATT_HEAD_DIM), F32),
        compiler_params=_params("arbitrary", "arbitrary"),
        name="in_proj_att",
    )(h, ss, w, gains)


def _softmax_mix(os, ls):
    m = functools.reduce(jnp.maximum, ls)
    ws = [jnp.exp(l - m) for l in ls]
    den = functools.reduce(lambda a, b: a + b, ws)
    return functools.reduce(lambda a, b: a + b, [(w / den) * o for w, o in zip(ws, os)])


def _attn_kernel(slope_ref, q0, q1, q2, k0, k1, k2, v0, v1, v2, *rest):
    oa_ref, c0, c1, c2, o_s, l_s = rest[-6:]
    j = pl.program_id(1)
    seq, hd = q0.shape
    tq = ATT_BLOCK
    row = lax.broadcasted_iota(jnp.int32, (tq, 2 * tq), 0)
    col = lax.broadcasted_iota(jnp.int32, (tq, 2 * tq), 1)
    steps = tq + row - col
    in_window = (steps >= 0) & (steps <= ATT_BLOCK)
    for g, (_, dil) in enumerate(ATT_GROUPS):
        q_ref, k_ref, v_ref = (q0, q1, q2)[g], (k0, k1, k2)[g], (v0, v1, v2)[g]
        nq = seq // (tq * dil)
        bias = slope_ref[g, j] * (steps * dil).astype(F32)

        def block(it, carry, q_ref=q_ref, k_ref=k_ref, v_ref=v_ref, g=g, dil=dil, nq=nq, bias=bias):
            r = it // nq
            i = it - r * nq
            start = r + i * (tq * dil)
            pstart = jnp.where(i > 0, start - tq * dil, start)

            def rows(st):
                return pl.ds(st, tq, stride=dil) if dil > 1 else pl.ds(pl.multiple_of(st, tq), tq)

            qh = q_ref[rows(start), :].astype(MXU_DTYPE)
            kk = jnp.concatenate([k_ref[rows(pstart), :], k_ref[rows(start), :]], axis=0).astype(MXU_DTYPE)
            vv = jnp.concatenate([v_ref[rows(pstart), :], v_ref[rows(start), :]], axis=0).astype(MXU_DTYPE)
            s = _dot_nt(qh, kk) * (hd ** -0.5)
            valid = in_window & ((col >= tq) | (i > 0))
            s = jnp.where(valid, s - bias, NEG)
            m = jnp.max(s, axis=-1, keepdims=True)
            p = jnp.exp(s - m)
            l = jnp.sum(p, axis=-1, keepdims=True)
            o_s[g, rows(start), :] = _dot(p.astype(MXU_DTYPE), vv) / l
            l_s[g, rows(start), :] = jnp.broadcast_to(m + jnp.log(l), (tq, hd))
            return carry

        lax.fori_loop(0, seq // tq, block, 0, unroll=8)
        c_ref = (c0, c1, c2)[g]
        win = c_ref.shape[1]
        c_ref[0] = k_ref[seq - win:, :]
        c_ref[1] = v_ref[seq - win:, :]
    n_g = len(ATT_GROUPS)
    oa_ref[...] = _softmax_mix([o_s[g] for g in range(n_g)], [l_s[g] for g in range(n_g)]).astype(oa_ref.dtype)


def _attn_prompt(att, slopes, bufs, layer, depth, bsz, seq, m_pad):
    for win, dil in ATT_GROUPS:
        assert win // dil == ATT_BLOCK and seq % (dil * ATT_BLOCK) == 0 and win <= seq
    hd = ATT_HEAD_DIM
    n_g = len(ATT_GROUPS)
    specs = [pl.BlockSpec((None, seq, hd), lambda b, j, h0=kind * ATT_HEADS + g * ATT_GROUP_HEADS: (h0 + j, b, 0))
             for kind in range(3) for g in range(n_g)]
    bufs = list(bufs or [])
    outs = pl.pallas_call(
        _attn_kernel,
        grid=(bsz, ATT_GROUP_HEADS),
        in_specs=[pl.BlockSpec(memory_space=pltpu.SMEM)] + specs + [pl.BlockSpec(memory_space=pl.ANY)] * len(bufs),
        out_specs=[pl.BlockSpec((seq, hd), lambda b, j: (b, j))]
        + [pl.BlockSpec((None, 2, None, None, win, hd), lambda b, j: (layer, 0, j, b, 0, 0)) for win, _ in ATT_GROUPS],
        out_shape=[jax.ShapeDtypeStruct((m_pad, ATT_GW), MXU_DTYPE)]
        + [jax.ShapeDtypeStruct((depth, 2, ATT_GROUP_HEADS, bsz, win, hd), F32) for win, _ in ATT_GROUPS],
        scratch_shapes=[pltpu.VMEM((n_g, seq, hd), F32), pltpu.VMEM((n_g, seq, hd), F32)],
        input_output_aliases={10 + g: 1 + g for g in range(len(bufs))},
        compiler_params=_params("arbitrary", "arbitrary"),
        name="attn_prompt",
    )(slopes, *([att] * 9), *bufs)
    return outs[0], outs[1:]


def _attn_decode_kernel(slope_ref, a_ref, c0_ref, c1_ref, c2_ref, o_ref):
    nh = ATT_GROUP_HEADS
    hd = ATT_HEAD_DIM
    nrow = lax.broadcasted_iota(jnp.int32, (ATT_BLOCK, 1, 1), 0)
    os, ls = [], []
    for g, (_, dil) in enumerate(ATT_GROUPS):
        c_ref = (c0_ref, c1_ref, c2_ref)[g]
        q = a_ref[g * nh:(g + 1) * nh]
        kn = a_ref[ATT_HEADS + g * nh:ATT_HEADS + (g + 1) * nh]
        vn = a_ref[2 * ATT_HEADS + g * nh:2 * ATT_HEADS + (g + 1) * nh]
        kb = c_ref[:, 0]
        vb = c_ref[:, 1]
        slope = slope_ref[g][:, 0:1]
        dist = ((ATT_BLOCK - nrow) * dil).astype(F32)
        sb = jnp.sum(kb * q, axis=-1, keepdims=True) * (hd ** -0.5) - slope * dist
        sn = jnp.sum(kn * q, axis=-1, keepdims=True) * (hd ** -0.5)
        m = jnp.maximum(jnp.max(sb, axis=0), sn)
        pb = jnp.exp(sb - m)
        pn = jnp.exp(sn - m)
        l = jnp.sum(pb, axis=0) + pn
        os.append((jnp.sum(pb * vb, axis=0) + pn * vn) / l)
        ls.append(m + jnp.log(l))
    o_ref[...] = _softmax_mix(os, ls)


def _attn_decode(a_dec, slopes_v, caches, layer):
    nb = a_dec.shape[0]
    views, specs = [], []
    for (win, dil), cbuf in zip(ATT_GROUPS, caches):
        depth, nb_, wb = cbuf.shape[:3]
        assert wb == win and nb_ == nb
        views.append(cbuf.reshape(depth, nb, win // dil, dil, 2, ATT_GROUP_HEADS, ATT_HEAD_DIM))
        specs.append(pl.BlockSpec((None, None, win // dil, None, 2, ATT_GROUP_HEADS, ATT_HEAD_DIM),
                                  lambda b: (layer, b, 0, 0, 0, 0, 0)))
    return pl.pallas_call(
        _attn_decode_kernel,
        grid=(nb,),
        in_specs=[pl.BlockSpec(slopes_v.shape, lambda b: (0, 0, 0)),
                  pl.BlockSpec((None,) + a_dec.shape[1:], lambda b: (b, 0, 0))] + specs,
        out_specs=pl.BlockSpec((None, ATT_GROUP_HEADS, ATT_HEAD_DIM), lambda b: (b, 0, 0)),
        out_shape=jax.ShapeDtypeStruct((nb, ATT_GROUP_HEADS, ATT_HEAD_DIM), F32),
        compiler_params=_params("arbitrary"),
        name="attn_decode",
    )(slopes_v, a_dec, *views)


def _place_rows_kernel(x_ref, dst_ref, o_ref):
    del dst_ref
    pad = jnp.zeros((o_ref.shape[0] - x_ref.shape[0], o_ref.shape[1]), F32)
    o_ref[...] = jnp.concatenate([x_ref[...], pad], axis=0).astype(o_ref.dtype)


def _place_sample_rows(x, dst, m_real):
    n = dst.shape[1]
    return pl.pallas_call(
        _place_rows_kernel,
        grid=(1,),
        in_specs=[pl.BlockSpec(x.shape, lambda i: (0, 0)), pl.BlockSpec(memory_space=pl.ANY)],
        out_specs=pl.BlockSpec((DEC_ROWS, n), lambda i: (m_real // DEC_ROWS, 0)),
        out_shape=jax.ShapeDtypeStruct(dst.shape, dst.dtype),
        input_output_aliases={1: 0},
        compiler_params=_params("arbitrary"),
        name="place_sample_rows",
    )(x, dst)


def kernel(x_prompt, x_sample, state_gla, cache_w128, cache_w512, cache_w2048, p_prompt, p_sample, norm_ffn1, ffn1_w_in, ffn1_w_out, norm_mix, w_in, gla_w_decay, gla_b_decay, gla_norm, att_q_norm, att_k_norm, w_gla_out, w_att_out, w_out, norm_ffn2, ffn2_w_in, ffn2_w_out, norm_ple, w_ple_gate, w_ple_proj):
    bsz, seq, d = x_prompt.shape
    nb, dec_seq, _ = x_sample.shape
    depth = norm_ffn1.shape[0]
    assert dec_seq == 1 and nb == 8
    m_real = bsz * seq
    assert m_real % DEC_ROWS == 0
    m_pad = m_real + DEC_ROWS
    gla_qk = gla_w_decay.shape[-1]
    dk = gla_qk // GLA_HEADS
    gla_v = d
    dv = gla_v // GLA_HEADS
    caches = (cache_w128, cache_w512, cache_w2048)
    n_g = len(ATT_GROUPS)
    nh, hd = ATT_GROUP_HEADS, ATT_HEAD_DIM

    lr_off = 2 * gla_qk + gla_v
    r_off = lr_off + GLA_RANK
    q_off = r_off + gla_v
    z_off = q_off + 3 * ATT_W
    assert w_in.shape[-1] == z_off + 2 * d and lr_off % LANE == 0 and r_off % 8 == 0
    w_in_t = jnp.swapaxes(w_in, 1, 2)

    zrow = lambda n, w: jnp.zeros((n, w), F32)
    x = jnp.concatenate([x_prompt.reshape(m_real, d), x_sample.reshape(nb, d), zrow(DEC_ROWS - nb, d)], axis=0)
    pdim = p_prompt.shape[-1]
    pe = jnp.concatenate([p_prompt.reshape(depth, m_real, pdim), p_sample.reshape(depth, nb, pdim),
                          jnp.zeros((depth, DEC_ROWS - nb, pdim), F32)], axis=1)

    wd = jnp.zeros((depth, LANE, gla_qk), F32).at[:, :GLA_RANK].set(gla_w_decay)
    bd = gla_b_decay.reshape(depth, 1, gla_qk)
    gn = gla_norm.reshape(depth, 1, dv)
    att_gain = jnp.concatenate([jnp.tile(att_q_norm, (1, ATT_HEADS)), jnp.tile(att_k_norm, (1, ATT_HEADS))],
                               axis=1).reshape(depth, 1, 2 * ATT_W)
    slopes = jnp.exp2(-8.0 * jnp.arange(1, ATT_HEADS + 1, dtype=F32) / ATT_HEADS)
    slopes_s = slopes.reshape(n_g, nh)
    slopes_v = jnp.broadcast_to(slopes.reshape(n_g, nh, 1), (n_g, nh, LANE))
    g3 = lambda a: a.reshape(depth, 1, d)
    n_f1, n_mix, n_f2, n_ple = g3(norm_ffn1), g3(norm_mix), g3(norm_ffn2), g3(norm_ple)

    _, tn_g = _mm_tiles(m_pad, gla_v + 2 * d)
    n_r, n_att = gla_v // tn_g, 3 * ATT_W // tn_g
    gate_cols = lambda j: jnp.where(j < n_r, j, j + n_att)

    gla_p, gla_s, win_p, new_rows = None, None, None, []
    hn = _norm_operand(x, n_f1, 0)
    for i in range(depth):
        x, h = _ffn_out(_swiglu_in(hn, ffn1_w_in, i), ffn1_w_out, i, x, n_mix, i)
        qkv = _matmul(h, w_in_t, i, 0, lr_off, F32, w_rows=True, big=True, name="in_proj_gla")
        lr = _matmul(h, w_in_t, i, lr_off, LANE, F32, tn=LANE, w_rows=True, name="in_proj_decay")
        gates = _matmul(h, w_in_t, i, r_off, gla_v + 2 * d, F32, col_map=gate_cols, w_rows=True, big=True,
                        name="in_proj_gates")
        att = _att_proj(h, w_in_t, i, q_off, att_gain)

        og, gla_p = _gla_prompt(qkv, gates, lr, wd, bd, gn, gla_p, i, depth, bsz, seq, dk, dv, m_pad)
        og, gla_s = _gla_decode(qkv, gates, lr, wd, bd, gn, state_gla, og, gla_s, i, nb, dk, dv, m_real)

        oa, win_p = _attn_prompt(att, slopes_s, win_p, i, depth, bsz, seq, m_pad)
        a_dec = jnp.transpose(att[:, m_real:m_real + nb], (1, 0, 2))
        o_dec = _attn_decode(a_dec, slopes_v, caches, i)
        oa = _place_sample_rows(o_dec.reshape(nb, ATT_GW), oa, m_real)

        merged = _merge(og, oa, w_gla_out, w_att_out, i, gates, gla_v, gla_v + d)
        x, hn = _matmul_residual(merged, w_out, i, x, n_f2, i)
        x, hn = _ffn_out(_swiglu_in(hn, ffn2_w_in, i), ffn2_w_out, i, x, n_ple, i)
        if i == depth - 1:
            x = _ple(hn, pe[i], w_ple_gate, w_ple_proj, i, x, m_real=m_real)
        else:
            x, hn = _ple(hn, pe[i], w_ple_gate, w_ple_proj, i, x, n_f1, i + 1)

        new_rows.append(a_dec.reshape(nb, 3, n_g, nh, hd)[:, 1:])

    y_prompt = x[0].reshape(bsz, seq, d)
    y_sample = x[1][x[1].shape[0] - DEC_ROWS:][:nb].reshape(nb, 1, d)
    new_rows = jnp.stack(new_rows)
    shift_cfg = [(0, 0, 0), (0, 0, 0), (-1, 1, 0), (0, 0, 0), (0, 0, 0), (0, 0, 0)]
    win_s = [lax.dynamic_update_slice(lax.pad(caches[g], jnp.zeros((), F32), shift_cfg), new_rows[:, :, None, :, g],
                                      (0, 0, caches[g].shape[2] - 1, 0, 0, 0)) for g in range(n_g)]
    win_p = [jnp.transpose(w, (0, 3, 4, 1, 2, 5)) for w in win_p]
    return (y_prompt, y_sample, gla_p, win_p[0], win_p[1], win_p[2], gla_s, win_s[0], win_s[1], win_s[2])
```
